```python
import math
import jax, jax.numpy as jnp
from jax import lax
import numpy as np

D_MODEL = 1024
BATCH = 8
SEQ = 2048
DEPTH = 1
DEC_BATCH = 128
DEC_SEQ = 8
PAST_LEN = 2048
PAGE_SIZE = 128

N_DIFF_HEADS = 4
DIFF_HEAD_DIM = 64
DIFF_V_DIM = 2 * DIFF_HEAD_DIM
ATTN_WIDTH = N_DIFF_HEADS * DIFF_V_DIM
POOL_WINDOWS = (2, 4, 8, 16)
N_POOL_GROUPS = len(POOL_WINDOWS)
POOL_GROUP_DIM = 64
POOL_WIDTH = N_POOL_GROUPS * POOL_GROUP_DIM
POOL_STATE = max(POOL_WINDOWS) - 1
N_MEM_HEADS = 4
MEM_HEAD_DIM = 64
MEM_WIDTH = N_MEM_HEADS * MEM_HEAD_DIM
N_MEM = 256
N_BRANCHES = 3
D_FF = 2816
CONV_WIDTH = 3
N_BUCKETS = 32
MAX_DISTANCE = 128
Q_BLOCK = 128
EPS = 1e-6
NEG_INF = -1e30
IN_SIZES = (N_DIFF_HEADS * 2 * DIFF_HEAD_DIM, N_DIFF_HEADS * 2 * DIFF_HEAD_DIM, ATTN_WIDTH,
            POOL_WIDTH, MEM_WIDTH, N_BRANCHES * D_MODEL)
D_IN = sum(IN_SIZES)
IN_SPLITS = tuple(int(s) for s in np.cumsum(IN_SIZES)[:-1])

kernel_name = 'hybrid_diffattn_pool_memxattn_convffn_step'


def rmsnorm(x, g):
    xf = x.astype(jnp.float32)
    y = xf * lax.rsqrt(jnp.mean(xf * xf, axis=-1, keepdims=True) + EPS)
    return (y * g.astype(jnp.float32)).astype(x.dtype)


def lambda_init(layer_idx):
    return 0.8 - 0.6 * math.exp(-0.3 * layer_idx)


def rel_bucket(rel):
    n = jnp.maximum(rel, 0)
    max_exact = N_BUCKETS // 2
    nf = jnp.maximum(n, 1).astype(jnp.float32)
    large = max_exact + (jnp.log(nf / max_exact) / math.log(MAX_DISTANCE / max_exact)
                         * (N_BUCKETS - max_exact)).astype(jnp.int32)
    large = jnp.minimum(large, N_BUCKETS - 1)
    return jnp.where(n < max_exact, n, large)


def diff_attn_block(qb, qpos_b, k, v, k_pos, rel_bias, lam):
    s = jnp.einsum('bqhcd,bkhcd->bchqk', qb.astype(jnp.float32), k.astype(jnp.float32)) * (DIFF_HEAD_DIM ** -0.5)
    rel = qpos_b[:, None] - k_pos[None, :]
    bias = jnp.transpose(rel_bias.astype(jnp.float32)[rel_bucket(rel)], (2, 0, 1))
    s = jnp.where(rel[None, None, None] >= 0, s + bias[None, None], NEG_INF)
    p = jax.nn.softmax(s, axis=-1)
    a = p[:, 0] - lam * p[:, 1]
    return jnp.einsum('bhqk,bkhe->bqhe', a, v.astype(jnp.float32))


def diff_attention(q, k, v, q_pos, k_pos, rel_bias, lam):
    B, Q = q.shape[0], q.shape[1]
    blk = Q_BLOCK if Q % Q_BLOCK == 0 else Q
    nb = Q // blk
    qb = jnp.moveaxis(q.reshape(B, nb, blk, N_DIFF_HEADS, 2, DIFF_HEAD_DIM), 1, 0)
    pb = q_pos.reshape(nb, blk)
    out = lax.map(lambda a: diff_attn_block(a[0], a[1], k, v, k_pos, rel_bias, lam), (qb, pb))
    return jnp.moveaxis(out, 0, 1).reshape(B, Q, N_DIFF_HEADS, DIFF_V_DIM)


def pool_mixer(u, prefix, pos, w_grp, scale):
    B, L, _ = u.shape
    P = prefix.shape[1]
    ext = jnp.concatenate([prefix, u], axis=1)
    c = jnp.pad(jnp.cumsum(ext.astype(jnp.float32), axis=1), ((0, 0), (1, 0), (0, 0)))
    means = []
    for gi, w in enumerate(POOL_WINDOWS):
        sl = slice(gi * POOL_GROUP_DIM, (gi + 1) * POOL_GROUP_DIM)
        win_sum = c[:, P + 1:P + 1 + L, sl] - c[:, P + 1 - w:P + 1 - w + L, sl]
        cnt = jnp.minimum(pos + 1, w).astype(jnp.float32)[None, :, None]
        means.append(win_sum / cnt)
    d = (jnp.concatenate(means, axis=-1) - u.astype(jnp.float32)).reshape(B, L, N_POOL_GROUPS, POOL_GROUP_DIM)
    y = jnp.einsum('blgc,gcd->blgd', d, w_grp.astype(jnp.float32)).reshape(B, L, POOL_WIDTH)
    y = y * scale.astype(jnp.float32)
    return y.astype(u.dtype), ext[:, ext.shape[1] - P:]


def mem_kv(mem, g, w):
    Bm, M, _ = mem.shape
    k, v = jnp.split(rmsnorm(mem, g) @ w, 2, axis=-1)
    return (k.reshape(Bm, M, N_MEM_HEADS, MEM_HEAD_DIM), v.reshape(Bm, M, N_MEM_HEADS, MEM_HEAD_DIM))


def conv_ffn(h, prefix, w_gate, w_up, conv_w, conv_b, w_down):
    L = h.shape[1]
    g = h @ w_gate
    u = h @ w_up
    ext = jnp.concatenate([prefix, g], axis=1)
    gc = conv_b
    for j in range(CONV_WIDTH):
        gc = gc + conv_w[j] * ext[:, j:j + L]
    f = (jax.nn.gelu(gc) * u) @ w_down
    return f, ext[:, ext.shape[1] - (CONV_WIDTH - 1):]


def layer(x, pos, k_past, v_past, k_pos, pool_prefix, conv_prefix, mem_k, mem_v, prm, rel_bias, lam_init):
    (norm1_g, w_in, lam_q1, lam_k1, lam_q2, lam_k2, subln_g, w_pool_grp, pool_scale,
     w_br_attn, w_br_pool, w_br_mem, w_out, norm2_g, w_ffn_gate, w_ffn_up,
     ffn_conv_w, ffn_conv_b, w_ffn_down) = prm
    B, L, _ = x.shape
    h = rmsnorm(x, norm1_g)
    q, k, v, u, qm, gl = jnp.split(h @ w_in, IN_SPLITS, axis=-1)
    q = q.reshape(B, L, N_DIFF_HEADS, 2, DIFF_HEAD_DIM)
    k = k.reshape(B, L, N_DIFF_HEADS, 2 * DIFF_HEAD_DIM)
    v = v.reshape(B, L, N_DIFF_HEADS, DIFF_V_DIM)
    k_all = k if k_past is None else jnp.concatenate([k_past.astype(k.dtype), k], axis=1)
    v_all = v if v_past is None else jnp.concatenate([v_past.astype(v.dtype), v], axis=1)
    K = k_all.shape[1]
    f32 = jnp.float32
    lam = (jnp.exp(jnp.sum(lam_q1.astype(f32) * lam_k1.astype(f32)))
           - jnp.exp(jnp.sum(lam_q2.astype(f32) * lam_k2.astype(f32))) + lam_init)
    o = diff_attention(q, k_all.reshape(B, K, N_DIFF_HEADS, 2, DIFF_HEAD_DIM), v_all, pos, k_pos, rel_bias, lam)
    o = (rmsnorm(o, subln_g) * (1.0 - lam_init)).astype(x.dtype).reshape(B, L, ATTN_WIDTH)
    pool_out, pool_state = pool_mixer(u, pool_prefix.astype(u.dtype), pos, w_pool_grp, pool_scale)
    qm = qm.reshape(B, L, N_MEM_HEADS, MEM_HEAD_DIM)
    sm = jnp.einsum('bqhd,bmhd->bhqm', qm.astype(f32), mem_k.astype(f32)) * (MEM_HEAD_DIM ** -0.5)
    pm = jax.nn.softmax(sm, axis=-1)
    om = jnp.einsum('bhqm,bmhd->bqhd', pm, mem_v.astype(f32)).astype(x.dtype).reshape(B, L, MEM_WIDTH)
    ga, gb, gm = jnp.split(jax.nn.sigmoid(gl), N_BRANCHES, axis=-1)
    merged = ga * (o @ w_br_attn) + gb * (pool_out @ w_br_pool) + gm * (om @ w_br_mem)
    x = x + merged @ w_out
    f, conv_state = conv_ffn(rmsnorm(x, norm2_g), conv_prefix.astype(x.dtype), w_ffn_gate, w_ffn_up,
                             ffn_conv_w, ffn_conv_b, w_ffn_down)
    x = x + f
    return x, k, v, pool_state, conv_state


def setup_inputs(seed: int = 0) -> dict:
    key = jax.random.key(seed)
    ks = jax.random.split(key, 40)
    n_pages = PAST_LEN // PAGE_SIZE
    n_phys = (DEC_BATCH * n_pages * 5) // 4
    nrm = lambda i, shape, s=1.0: jax.random.normal(ks[i], shape, jnp.float32) * s
    gain = lambda i, shape: 1.0 + 0.02 * jax.random.normal(ks[i], shape, jnp.float32)
    kv_shape = (DEPTH, n_phys, PAGE_SIZE, N_DIFF_HEADS, 2 * DIFF_HEAD_DIM)
    page_table = jax.random.permutation(ks[5], n_phys)[:DEC_BATCH * n_pages].reshape(DEC_BATCH, n_pages).astype(jnp.int32)
    return {
        'x_prompt': nrm(0, (BATCH, SEQ, D_MODEL)),
        'x_sample': nrm(1, (DEC_BATCH, DEC_SEQ, D_MODEL)),
        'mem_prompt': nrm(2, (BATCH, N_MEM, D_MODEL)),
        'cache_k': nrm(3, kv_shape),
        'cache_v': nrm(4, kv_shape),
        'page_table': page_table,
        'state_pool': nrm(6, (DEPTH, DEC_BATCH, POOL_STATE, POOL_WIDTH)),
        'state_ffn_conv': nrm(7, (DEPTH, DEC_BATCH, CONV_WIDTH - 1, D_FF)),
        'cache_mem_k': nrm(8, (DEPTH, DEC_BATCH, N_MEM, N_MEM_HEADS, MEM_HEAD_DIM)),
        'cache_mem_v': nrm(9, (DEPTH, DEC_BATCH, N_MEM, N_MEM_HEADS, MEM_HEAD_DIM)),
        'norm1_g': gain(10, (DEPTH, D_MODEL)),
        'w_in': nrm(11, (DEPTH, D_MODEL, D_IN), D_MODEL ** -0.5),
        'lam_q1': nrm(12, (DEPTH, DIFF_HEAD_DIM), 0.1),
        'lam_k1': nrm(13, (DEPTH, DIFF_HEAD_DIM), 0.1),
        'lam_q2': nrm(14, (DEPTH, DIFF_HEAD_DIM), 0.1),
        'lam_k2': nrm(15, (DEPTH, DIFF_HEAD_DIM), 0.1),
        'subln_g': gain(16, (DEPTH, DIFF_V_DIM)),
        'w_pool_grp': nrm(17, (DEPTH, N_POOL_GROUPS, POOL_GROUP_DIM, POOL_GROUP_DIM), POOL_GROUP_DIM ** -0.5),
        'pool_scale': 1.0 + 0.1 * nrm(18, (DEPTH, POOL_WIDTH)),
        'w_br_attn': nrm(19, (DEPTH, ATTN_WIDTH, D_MODEL), ATTN_WIDTH ** -0.5),
        'w_br_pool': nrm(20, (DEPTH, POOL_WIDTH, D_MODEL), POOL_WIDTH ** -0.5),
        'w_br_mem': nrm(21, (DEPTH, MEM_WIDTH, D_MODEL), MEM_WIDTH ** -0.5),
        'mem_norm_g': gain(22, (DEPTH, D_MODEL)),
        'w_mem_kv': nrm(23, (DEPTH, D_MODEL, 2 * MEM_WIDTH), D_MODEL ** -0.5),
        'w_out': nrm(24, (DEPTH, D_MODEL, D_MODEL), D_MODEL ** -0.5),
        'norm2_g': gain(25, (DEPTH, D_MODEL)),
        'w_ffn_gate': nrm(26, (DEPTH, D_MODEL, D_FF), D_MODEL ** -0.5),
        'w_ffn_up': nrm(27, (DEPTH, D_MODEL, D_FF), D_MODEL ** -0.5),
        'ffn_conv_w': nrm(28, (DEPTH, CONV_WIDTH, D_FF), CONV_WIDTH ** -0.5),
        'ffn_conv_b': nrm(29, (DEPTH, D_FF), 0.01),
        'w_ffn_down': nrm(30, (DEPTH, D_FF, D_MODEL), D_FF ** -0.5),
        'rel_bias': nrm(31, (N_BUCKETS, N_DIFF_HEADS), 0.5),
        'final_norm_g': gain(32, (D_MODEL,)),
    }


def reference(x_prompt, x_sample, mem_prompt, cache_k, cache_v, page_table, state_pool, state_ffn_conv,
              cache_mem_k, cache_mem_v, norm1_g, w_in, lam_q1, lam_k1, lam_q2, lam_k2, subln_g,
              w_pool_grp, pool_scale, w_br_attn, w_br_pool, w_br_mem, mem_norm_g, w_mem_kv, w_out,
              norm2_g, w_ffn_gate, w_ffn_up, ffn_conv_w, ffn_conv_b, w_ffn_down, rel_bias, final_norm_g):
    B, S, _ = x_prompt.shape
    DB, DS, _ = x_sample.shape
    n_pages = page_table.shape[1]
    page = cache_k.shape[2]
    past = n_pages * page
    pos_p = jnp.arange(S, dtype=jnp.int32)
    pos_s = past + jnp.arange(DS, dtype=jnp.int32)
    kpos_s = jnp.arange(past + DS, dtype=jnp.int32)
    xp, xs = x_prompt, x_sample
    kp_l, vp_l, ks_l, vs_l, poolp_l, pools_l, convp_l, convs_l, mkp_l, mvp_l = ([] for _ in range(10))
    for l in range(DEPTH):
        lam_init = lambda_init(l)
        prm = (norm1_g[l], w_in[l], lam_q1[l], lam_k1[l], lam_q2[l], lam_k2[l], subln_g[l], w_pool_grp[l],
               pool_scale[l], w_br_attn[l], w_br_pool[l], w_br_mem[l], w_out[l], norm2_g[l], w_ffn_gate[l],
               w_ffn_up[l], ffn_conv_w[l], ffn_conv_b[l], w_ffn_down[l])
        mk_p, mv_p = mem_kv(mem_prompt, mem_norm_g[l], w_mem_kv[l])
        pool0 = jnp.zeros((B, POOL_STATE, POOL_WIDTH), xp.dtype)
        conv0 = jnp.zeros((B, CONV_WIDTH - 1, D_FF), xp.dtype)
        xp, kp, vp, poolp, convp = layer(xp, pos_p, None, None, pos_p, pool0, conv0, mk_p, mv_p,
                                         prm, rel_bias, lam_init)
        k_past = cache_k[l][page_table].reshape(DB, past, N_DIFF_HEADS, 2 * DIFF_HEAD_DIM)
        v_past = cache_v[l][page_table].reshape(DB, past, N_DIFF_HEADS, DIFF_V_DIM)
        xs, ks_, vs_, pools, convs = layer(xs, pos_s, k_past, v_past, kpos_s, state_pool[l], state_ffn_conv[l],
                                           cache_mem_k[l], cache_mem_v[l], prm, rel_bias, lam_init)
        kp_l.append(kp); vp_l.append(vp); ks_l.append(ks_); vs_l.append(vs_)
        poolp_l.append(poolp); pools_l.append(pools); convp_l.append(convp); convs_l.append(convs)
        mkp_l.append(mk_p); mvp_l.append(mv_p)
    y_prompt = rmsnorm(xp, final_norm_g)
    y_sample = rmsnorm(xs, final_norm_g)
    return (y_prompt, y_sample, jnp.stack(kp_l), jnp.stack(vp_l), jnp.stack(ks_l), jnp.stack(vs_l),
            jnp.stack(poolp_l), jnp.stack(pools_l), jnp.stack(convp_l), jnp.stack(convs_l),
            jnp.stack(mkp_l), jnp.stack(mvp_l))
```

```python
import functools
import math

import jax
import jax.numpy as jnp
from jax import lax
from jax.experimental import pallas as pl
from jax.experimental.pallas import tpu as pltpu

F32 = jnp.float32
BF16 = jnp.bfloat16

D_MODEL = 1024
N_HEADS = 4
HEAD_DIM = 64
V_DIM = 2 * HEAD_DIM
ATTN_WIDTH = N_HEADS * V_DIM
POOL_WINDOWS = (2, 4, 8, 16)
POOL_GROUP_DIM = 64
POOL_WIDTH = 256
POOL_STATE = 15
MEM_HEAD_DIM = 64
MEM_WIDTH = 256
N_MEM = 256
D_FF = 2816
N_BUCKETS = 32
MAX_DISTANCE = 128
EPS = 1e-6
NEG_INF = -1e30
QK_SCALE = HEAD_DIM ** -0.5
MEM_SCALE = MEM_HEAD_DIM ** -0.5
LAM_INIT = 0.8 - 0.6 * math.exp(-0.3 * 0)
SUBLN_SCALE = 1.0 - LAM_INIT

ATTN_BLOCK = 256
ROW_BLOCK = 512
FFN_CHUNK = 256
SAMPLE_KPAD = 128
VMEM_LIMIT = 56 * 1024 * 1024


def _cparams(*sem):
    return pltpu.CompilerParams(dimension_semantics=sem, vmem_limit_bytes=VMEM_LIMIT)


def _rmsnorm(x, g):
    return x * lax.rsqrt(jnp.mean(x * x, axis=-1, keepdims=True) + EPS) * g


def _dot(a, b):
    return jnp.dot(a, b, preferred_element_type=F32)


def _dot_nt(a, b):
    return lax.dot_general(a, b, (((1,), (1,)), ((), ())), preferred_element_type=F32)


def _softmax_rows(s):
    m = jnp.max(s, axis=-1, keepdims=True)
    p = jnp.exp(s - m)
    return p / jnp.sum(p, axis=-1, keepdims=True)


def _lam_value(lq1, lk1, lq2, lk2):
    a = jnp.sum(lq1 * lk1, axis=-1, keepdims=True)
    b = jnp.sum(lq2 * lk2, axis=-1, keepdims=True)
    return jnp.exp(a) - jnp.exp(b) + LAM_INIT


def _norm_proj_kernel(x_ref, g_ref, w_ref, *out_refs, groups):
    h = _rmsnorm(x_ref[...], g_ref[...]).astype(BF16)
    k = 0
    for start, size, outs in groups:
        y = _dot(h, w_ref[:, start:start + size])
        for scale in outs:
            o_ref = out_refs[k]
            k += 1
            o_ref[...] = (y if scale == 1.0 else y * scale).astype(o_ref.dtype)


def _norm_proj(x2d, g, w_bf16, groups, out_dtypes, tm):
    rows, d = x2d.shape
    n_cols = w_bf16.shape[1]
    sizes = [size for _, size, outs in groups for _ in outs]
    out_shape = [jax.ShapeDtypeStruct((rows, s), dt) for s, dt in zip(sizes, out_dtypes)]
    return pl.pallas_call(
        functools.partial(_norm_proj_kernel, groups=groups),
        grid=(rows // tm,),
        in_specs=[pl.BlockSpec((tm, d), lambda i: (i, 0)),
                  pl.BlockSpec((1, d), lambda i: (0, 0)),
                  pl.BlockSpec((d, n_cols), lambda i: (0, 0))],
        out_specs=[pl.BlockSpec((tm, s), lambda i: (i, 0)) for s in sizes],
        out_shape=out_shape,
        compiler_params=_cparams("parallel"),
        name="norm_proj",
    )(x2d, g.reshape(1, d), w_bf16)


def _rel_bucket(rel):
    n = jnp.maximum(rel, 0)
    max_exact = N_BUCKETS // 2
    nf = jnp.maximum(n, 1).astype(F32)
    large = max_exact + (jnp.log(nf / max_exact) / math.log(MAX_DISTANCE / max_exact)
                         * (N_BUCKETS - max_exact)).astype(jnp.int32)
    large = jnp.minimum(large, N_BUCKETS - 1)
    return jnp.where(n < max_exact, n, large)


def _bias_from_rel(rel, rb_ref, head):
    bucket = _rel_bucket(rel)
    acc = jnp.zeros(rel.shape, F32)
    for b in range(N_BUCKETS):
        acc = jnp.where(bucket == b, rb_ref[b, head], acc)
    return jnp.where(rel >= 0, acc, NEG_INF)


def _bias_kernel(rb_ref, pb_ref, sb_ref, *, t, past, n_new):
    row = lax.broadcasted_iota(jnp.int32, (t, t), 0)
    col = lax.broadcasted_iota(jnp.int32, (t, t), 1)
    for d in range(3):
        rel = d * t + row - col
        for h in range(N_HEADS):
            pb_ref[h, d] = _bias_from_rel(rel, rb_ref, h)
    rows, n_keys = sb_ref.shape[1], sb_ref.shape[2]
    qi = lax.broadcasted_iota(jnp.int32, (rows, n_keys), 0) % n_new
    key = lax.broadcasted_iota(jnp.int32, (rows, n_keys), 1)
    rel = past + qi - key
    for h in range(N_HEADS):
        sb_ref[h] = _bias_from_rel(rel, rb_ref, h)


def _bias_tables(rel_bias, t, past, n_new):
    n_keys = past + SAMPLE_KPAD
    return pl.pallas_call(
        functools.partial(_bias_kernel, t=t, past=past, n_new=n_new),
        in_specs=[pl.BlockSpec(memory_space=pltpu.SMEM)],
        out_specs=[pl.BlockSpec(memory_space=pltpu.VMEM), pl.BlockSpec(memory_space=pltpu.VMEM)],
        out_shape=[jax.ShapeDtypeStruct((N_HEADS, 3, t, t), F32),
                   jax.ShapeDtypeStruct((N_HEADS, 2 * n_new, n_keys), F32)],
        compiler_params=pltpu.CompilerParams(vmem_limit_bytes=VMEM_LIMIT),
        name="bias_tables",
    )(rel_bias)


def _subln(o, g):
    return _rmsnorm(o, g) * SUBLN_SCALE


def _prompt_attn_kernel(lq1_ref, lk1_ref, lq2_ref, lk2_ref, q_ref, k_ref, v_ref, bias_ref, g_ref, o_ref, *, t):
    i = pl.program_id(1)
    lam = _lam_value(lq1_ref[...], lk1_ref[...], lq2_ref[...], lk2_ref[...])
    for h in range(N_HEADS):
        outs = []
        for c in range(2):
            col = h * V_DIM + c * HEAD_DIM
            q = q_ref[:, col:col + HEAD_DIM]

            def body(j, carry, col=col, q=q, h=h):
                m, l, acc = carry
                r0 = pl.multiple_of(j * t, t)
                kb = k_ref[pl.ds(r0, t), col:col + HEAD_DIM]
                vb = v_ref[pl.ds(r0, t), h * V_DIM:(h + 1) * V_DIM]
                s = _dot_nt(q, kb) + bias_ref[h, jnp.minimum(i - j, 2)]
                m_new = jnp.maximum(m, jnp.max(s, axis=-1, keepdims=True))
                alpha = jnp.exp(m - m_new)
                p = jnp.exp(s - m_new)
                l = alpha * l + jnp.sum(p, axis=-1, keepdims=True)
                acc = alpha * acc + _dot(p.astype(BF16), vb)
                return m_new, l, acc

            init = (jnp.full((t, 1), NEG_INF, F32), jnp.zeros((t, 1), F32), jnp.zeros((t, V_DIM), F32))
            _, l, acc = lax.fori_loop(0, i + 1, body, init)
            outs.append(acc / l)
        o_ref[:, h * V_DIM:(h + 1) * V_DIM] = _subln(outs[0] - lam * outs[1], g_ref[...]).astype(o_ref.dtype)


def _prompt_attn(lams, q, k, v, bias, subln_g, batch, seq, t):
    vec = pl.BlockSpec((1, HEAD_DIM), lambda b, i: (0, 0))
    return pl.pallas_call(
        functools.partial(_prompt_attn_kernel, t=t),
        grid=(batch, seq // t),
        in_specs=[vec, vec, vec, vec,
                  pl.BlockSpec((t, ATTN_WIDTH), lambda b, i: (b * (seq // t) + i, 0)),
                  pl.BlockSpec((seq, ATTN_WIDTH), lambda b, i: (b, 0)),
                  pl.BlockSpec((seq, ATTN_WIDTH), lambda b, i: (b, 0)),
                  pl.BlockSpec((N_HEADS, 3, t, t), lambda b, i: (0, 0, 0, 0)),
                  pl.BlockSpec((1, V_DIM), lambda b, i: (0, 0))],
        out_specs=pl.BlockSpec((t, ATTN_WIDTH), lambda b, i: (b * (seq // t) + i, 0)),
        out_shape=jax.ShapeDtypeStruct((batch * seq, ATTN_WIDTH), BF16),
        compiler_params=_cparams("parallel", "parallel"),
        name="prompt_attn",
    )(*lams, q, k, v, bias, subln_g.reshape(1, V_DIM))


def _pool_branch(ext, u, pos0, wbd, scale):
    n, c = u.shape
    p = ext.shape[0] - n
    s2 = ext + pltpu.roll(ext, 1, 0)
    s4 = s2 + pltpu.roll(s2, 2, 0)
    s8 = s4 + pltpu.roll(s4, 4, 0)
    s16 = s8 + pltpu.roll(s8, 8, 0)
    lane = lax.broadcasted_iota(jnp.int32, (n, c), 1)
    grp = lane // POOL_GROUP_DIM
    win = jnp.where(grp == 0, s2[p:], jnp.where(grp == 1, s4[p:], jnp.where(grp == 2, s8[p:], s16[p:])))
    width = jnp.where(grp == 0, 2, jnp.where(grp == 1, 4, jnp.where(grp == 2, 8, 16)))
    pos = pos0 + lax.broadcasted_iota(jnp.int32, (n, c), 0)
    cnt = jnp.minimum(pos + 1, width).astype(F32)
    d = win / cnt - u
    return _dot(d.astype(BF16), wbd) * scale


def _mem_attention_head(qm_h, mk_h, mv_h):
    p = _softmax_rows(_dot_nt(qm_h, mk_h))
    return _dot(p.astype(BF16), mv_h)


def _prompt_side_kernel(u_ref, uprev_ref, qm_ref, mk_ref, mv_ref, wbd_ref, ps_ref, po_ref, om_ref, *, tm):
    i = pl.program_id(1)
    u = u_ref[...]
    prev = jnp.where(i > 0, uprev_ref[...], 0.0)
    ext = jnp.concatenate([prev, u], axis=0)
    po_ref[...] = _pool_branch(ext, u, i * tm, wbd_ref[...], ps_ref[...]).astype(po_ref.dtype)
    for h in range(N_HEADS):
        sl = slice(h * MEM_HEAD_DIM, (h + 1) * MEM_HEAD_DIM)
        om = _mem_attention_head(qm_ref[:, sl], mk_ref[:, sl].astype(BF16), mv_ref[:, sl].astype(BF16))
        om_ref[:, sl] = om.astype(om_ref.dtype)


def _prompt_side(u, qm, mk, mv, wbd, pool_scale, batch, seq, tm):
    nblk = seq // tm
    halo = 16
    row = lambda b, i: (b * nblk + i, 0)
    return pl.pallas_call(
        functools.partial(_prompt_side_kernel, tm=tm),
        grid=(batch, nblk),
        in_specs=[pl.BlockSpec((tm, POOL_WIDTH), row),
                  pl.BlockSpec((halo, POOL_WIDTH),
                               lambda b, i: (jnp.maximum((b * nblk + i) * (tm // halo) - 1, 0), 0)),
                  pl.BlockSpec((tm, MEM_WIDTH), row),
                  pl.BlockSpec((N_MEM, MEM_WIDTH), lambda b, i: (b, 0)),
                  pl.BlockSpec((N_MEM, MEM_WIDTH), lambda b, i: (b, 0)),
                  pl.BlockSpec((POOL_WIDTH, POOL_WIDTH), lambda b, i: (0, 0)),
                  pl.BlockSpec((1, POOL_WIDTH), lambda b, i: (0, 0))],
        out_specs=[pl.BlockSpec((tm, POOL_WIDTH), row), pl.BlockSpec((tm, MEM_WIDTH), row)],
        out_shape=[jax.ShapeDtypeStruct((batch * seq, POOL_WIDTH), BF16),
                   jax.ShapeDtypeStruct((batch * seq, MEM_WIDTH), BF16)],
        compiler_params=_cparams("parallel", "parallel"),
        name="prompt_side",
    )(u, u, qm, mk, mv, wbd, pool_scale.reshape(1, POOL_WIDTH))


def _sample_step_kernel(pt_ref, lq1_ref, lk1_ref, lq2_ref, lk2_ref, q_ref, kn_ref, vn_ref, bias_ref, g_ref,
                        qm_ref, mk_ref, mv_ref, u_ref, st_ref, wbd_ref, ps_ref, *refs, n_pages, n_new, past):
    del pt_ref
    kp = refs[:n_pages]
    vp = refs[n_pages:2 * n_pages]
    o_ref, om_ref, po_ref = refs[2 * n_pages:]
    lam = _lam_value(lq1_ref[...], lk1_ref[...], lq2_ref[...], lk2_ref[...])
    pad = jnp.zeros((SAMPLE_KPAD - n_new, V_DIM), F32)
    lane = lax.broadcasted_iota(jnp.int32, (n_new, V_DIM), 1)
    for h in range(N_HEADS):
        sl = slice(h * V_DIM, (h + 1) * V_DIM)
        k_h = jnp.concatenate([r[:, h, :] for r in kp] + [kn_ref[:, sl], pad], axis=0).astype(BF16)
        v_h = jnp.concatenate([r[:, h, :] for r in vp] + [vn_ref[:, sl], pad], axis=0).astype(BF16)
        q_h = q_ref[:, sl]
        q2 = jnp.concatenate([jnp.where(lane < HEAD_DIM, q_h, 0.0), jnp.where(lane >= HEAD_DIM, q_h, 0.0)],
                             axis=0).astype(BF16)
        p = _softmax_rows(_dot_nt(q2, k_h) + bias_ref[h])
        a = p[:n_new] - lam * p[n_new:]
        o_ref[:, sl] = _subln(_dot(a.astype(BF16), v_h), g_ref[...]).astype(o_ref.dtype)
    for h in range(N_HEADS):
        sl = slice(h * MEM_HEAD_DIM, (h + 1) * MEM_HEAD_DIM)
        om = _mem_attention_head(qm_ref[:, sl].astype(BF16), mk_ref[:, h, :].astype(BF16),
                                 mv_ref[:, h, :].astype(BF16))
        om_ref[:, sl] = om.astype(om_ref.dtype)
    u = u_ref[...]
    ext = jnp.concatenate([st_ref[...], u], axis=0)
    po_ref[...] = _pool_branch(ext, u, past, wbd_ref[...], ps_ref[...]).astype(po_ref.dtype)


def _sample_step(page_table, lams, q, k_new, v_new, bias, subln_g, qm, mem_k, mem_v, u, state16, wbd, pool_scale,
                 cache_k, cache_v):
    n_seq, n_pages = page_table.shape
    n_new = q.shape[1]
    page = cache_k.shape[2]
    past = n_pages * page
    vec = pl.BlockSpec((1, HEAD_DIM), lambda s, pt: (0, 0))
    per_seq = lambda width: pl.BlockSpec((None, n_new, width), lambda s, pt: (s, 0, 0))
    page_spec = lambda p: pl.BlockSpec((None, None, page, N_HEADS, V_DIM),
                                       lambda s, pt, p=p: (0, pt[s, p], 0, 0, 0))
    mem_spec = pl.BlockSpec((None, None, N_MEM, N_HEADS, MEM_HEAD_DIM), lambda s, pt: (0, s, 0, 0, 0))
    in_specs = [vec, vec, vec, vec,
                per_seq(ATTN_WIDTH), per_seq(ATTN_WIDTH), per_seq(ATTN_WIDTH),
                pl.BlockSpec(bias.shape, lambda s, pt: (0, 0, 0)),
                pl.BlockSpec((1, V_DIM), lambda s, pt: (0, 0)),
                per_seq(MEM_WIDTH), mem_spec, mem_spec,
                per_seq(POOL_WIDTH),
                pl.BlockSpec((None, 16, POOL_WIDTH), lambda s, pt: (s, 0, 0)),
                pl.BlockSpec((POOL_WIDTH, POOL_WIDTH), lambda s, pt: (0, 0)),
                pl.BlockSpec((1, POOL_WIDTH), lambda s, pt: (0, 0))]
    in_specs += [page_spec(p) for p in range(n_pages)] * 2
    grid_spec = pltpu.PrefetchScalarGridSpec(
        num_scalar_prefetch=1,
        grid=(n_seq,),
        in_specs=in_specs,
        out_specs=[per_seq(ATTN_WIDTH), per_seq(MEM_WIDTH), per_seq(POOL_WIDTH)],
    )
    return pl.pallas_call(
        functools.partial(_sample_step_kernel, n_pages=n_pages, n_new=n_new, past=past),
        grid_spec=grid_spec,
        out_shape=[jax.ShapeDtypeStruct((n_seq, n_new, ATTN_WIDTH), F32),
                   jax.ShapeDtypeStruct((n_seq, n_new, MEM_WIDTH), F32),
                   jax.ShapeDtypeStruct((n_seq, n_new, POOL_WIDTH), F32)],
        compiler_params=_cparams("parallel"),
        name="sample_step",
    )(page_table, *lams, q, k_new, v_new, bias, subln_g.reshape(1, V_DIM), qm, mem_k, mem_v, u, state16, wbd,
      pool_scale.reshape(1, POOL_WIDTH), *([cache_k] * n_pages), *([cache_v] * n_pages))


def _merge_kernel(x_ref, o_ref, po_ref, om_ref, g_ref, wgl_ref, wa_ref, wp_ref, wm_ref, wout_ref, x1_ref):
    x = x_ref[...]
    h = _rmsnorm(x, g_ref[...]).astype(BF16)
    d = D_MODEL
    merged = jax.nn.sigmoid(_dot(h, wgl_ref[:, 0:d])) * _dot(o_ref[...].astype(BF16), wa_ref[...])
    merged = merged + jax.nn.sigmoid(_dot(h, wgl_ref[:, d:2 * d])) * _dot(po_ref[...].astype(BF16), wp_ref[...])
    merged = merged + jax.nn.sigmoid(_dot(h, wgl_ref[:, 2 * d:3 * d])) * _dot(om_ref[...].astype(BF16), wm_ref[...])
    x1_ref[...] = x + _dot(merged.astype(BF16), wout_ref[...])


def _merge(x2d, o, po, om, norm_g, wgl, wa, wp, wm, wout, tm):
    rows, d = x2d.shape
    row = lambda width: pl.BlockSpec((tm, width), lambda i: (i, 0))
    full = lambda a: pl.BlockSpec(a.shape, lambda i: (0, 0), pipeline_mode=pl.Buffered(1))
    g = norm_g.reshape(1, d)
    return pl.pallas_call(
        _merge_kernel,
        grid=(rows // tm,),
        in_specs=[row(d), row(ATTN_WIDTH), row(POOL_WIDTH), row(MEM_WIDTH), full(g), full(wgl), full(wa), full(wp),
                  full(wm), full(wout)],
        out_specs=row(d),
        out_shape=jax.ShapeDtypeStruct((rows, d), F32),
        compiler_params=_cparams("parallel"),
        name="merge",
    )(x2d, o, po, om, g, wgl, wa, wp, wm, wout)


def _conv_ffn_kernel(x_ref, g2_ref, wg_ref, wu_ref, cw_ref, cb_ref, wd_ref, gf_ref, *refs, tm, fc, seq_rows):
    if seq_rows:
        pre_ref, y_ref, gate_ref = refs
    else:
        y_ref, tail_ref, carry_ref = refs

        @pl.when(pl.program_id(1) == 0)
        def _():
            carry_ref[...] = jnp.zeros_like(carry_ref)

    x = x_ref[...]
    h = _rmsnorm(x, g2_ref[...]).astype(BF16)
    acc = jnp.zeros((tm, D_MODEL), F32)
    for c0 in range(0, D_FF, fc):
        cs = slice(c0, c0 + fc)
        gate = _dot(h, wg_ref[:, cs])
        up = _dot(h, wu_ref[:, cs])
        if seq_rows:
            pre = pre_ref[:, cs]
            t = lax.broadcasted_iota(jnp.int32, (tm, fc), 0) % seq_rows
            back1 = jnp.where(t >= 1, pltpu.roll(gate, 1, 0), pltpu.roll(pre, tm - 1, 0))
            back2 = jnp.where(t >= 2, pltpu.roll(gate, 2, 0), pre)
            gate_ref[:, cs] = gate
        else:
            ext = jnp.concatenate([carry_ref[:, cs], gate], axis=0)
            back1 = pltpu.roll(ext, 1, 0)[8:]
            back2 = pltpu.roll(ext, 2, 0)[8:]
            carry_ref[:, cs] = gate[tm - 8:]
        gc = cb_ref[:, cs] + cw_ref[0:1, cs] * back2
        gc = gc + cw_ref[1:2, cs] * back1
        gc = gc + cw_ref[2:3, cs] * gate
        act = (jax.nn.gelu(gc) * up).astype(BF16)
        acc = acc + _dot(act, wd_ref[cs, :])
    if not seq_rows:
        tail_ref[...] = carry_ref[...]
    y_ref[...] = _rmsnorm(x + acc, gf_ref[...])


def _conv_ffn(x1, norm_g, wg, wu, conv_w, conv_b, wd, final_g, tm, batch=None, seq=None, prefix=None):
    rows, d = x1.shape
    g2 = norm_g.reshape(1, d)
    gf = final_g.reshape(1, d)
    cb = conv_b.reshape(1, D_FF)
    if prefix is None:
        nblk = seq // tm
        row = lambda width: pl.BlockSpec((tm, width), lambda b, i: (b * nblk + i, 0))
        full = lambda a: pl.BlockSpec(a.shape, lambda b, i: (0, 0), pipeline_mode=pl.Buffered(1))
        return pl.pallas_call(
            functools.partial(_conv_ffn_kernel, tm=tm, fc=FFN_CHUNK, seq_rows=0),
            grid=(batch, nblk),
            in_specs=[row(d), full(g2), full(wg), full(wu), full(conv_w), full(cb), full(wd), full(gf)],
            out_specs=[row(d), pl.BlockSpec((None, 8, D_FF), lambda b, i: (b, 0, 0))],
            out_shape=[jax.ShapeDtypeStruct((rows, d), F32), jax.ShapeDtypeStruct((batch, 8, D_FF), F32)],
            scratch_shapes=[pltpu.VMEM((8, D_FF), F32)],
            compiler_params=_cparams("parallel", "arbitrary"),
            name="conv_ffn_prompt",
        )(x1, g2, wg, wu, conv_w, cb, wd, gf)
    row = lambda width: pl.BlockSpec((tm, width), lambda i: (i, 0))
    full = lambda a: pl.BlockSpec(a.shape, lambda i: (0, 0), pipeline_mode=pl.Buffered(1))
    return pl.pallas_call(
        functools.partial(_conv_ffn_kernel, tm=tm, fc=FFN_CHUNK, seq_rows=8),
        grid=(rows // tm,),
        in_specs=[row(d), full(g2), full(wg), full(wu), full(conv_w), full(cb), full(wd), full(gf), row(D_FF)],
        out_specs=[row(d), row(D_FF)],
        out_shape=[jax.ShapeDtypeStruct((rows, d), F32), jax.ShapeDtypeStruct((rows, D_FF), F32)],
        compiler_params=_cparams("parallel"),
        name="conv_ffn_sample",
    )(x1, g2, wg, wu, conv_w, cb, wd, gf, prefix)


def kernel(x_prompt, x_sample, mem_prompt, cache_k, cache_v, page_table, state_pool, state_ffn_conv, cache_mem_k, cache_mem_v, norm1_g, w_in, lam_q1, lam_k1, lam_q2, lam_k2, subln_g, w_pool_grp, pool_scale, w_br_attn, w_br_pool, w_br_mem, mem_norm_g, w_mem_kv, w_out, norm2_g, w_ffn_gate, w_ffn_up, ffn_conv_w, ffn_conv_b, w_ffn_down, rel_bias, final_norm_g):
    depth = w_in.shape[0]
    assert depth == 1, "single-layer step only"
    B, S, D = x_prompt.shape
    DB, DS, _ = x_sample.shape
    n_pages = page_table.shape[1]
    past = n_pages * cache_k.shape[2]
    assert DS == 8 and S % ROW_BLOCK == 0 and (DB * DS) % ROW_BLOCK == 0

    w_qkv = w_in[0, :, :2048].astype(BF16)
    w_gl = w_in[0, :, 2048:].astype(BF16)
    wa, wp, wm = w_br_attn[0].astype(BF16), w_br_pool[0].astype(BF16), w_br_mem[0].astype(BF16)
    wout = w_out[0].astype(BF16)
    wg, wu, wd = w_ffn_gate[0].astype(BF16), w_ffn_up[0].astype(BF16), w_ffn_down[0].astype(BF16)
    w_mem = w_mem_kv[0].astype(BF16)
    wbd = jnp.zeros((POOL_WIDTH, POOL_WIDTH), F32)
    for gi in range(len(POOL_WINDOWS)):
        sl = slice(gi * POOL_GROUP_DIM, (gi + 1) * POOL_GROUP_DIM)
        wbd = wbd.at[sl, sl].set(w_pool_grp[0, gi])
    wbd = wbd.astype(BF16)
    lams = tuple(a[0].reshape(1, HEAD_DIM) for a in (lam_q1, lam_k1, lam_q2, lam_k2))

    prompt_bias, sample_bias = _bias_tables(rel_bias, ATTN_BLOCK, past, DS)

    xp = x_prompt.reshape(B * S, D)
    groups = ((0, 512, (QK_SCALE,)), (512, 512, (1.0, 1.0)), (1024, 512, (1.0, 1.0)), (1536, 256, (1.0,)),
              (1792, 256, (MEM_SCALE,)))
    qp, kp, kp_b, vp, vp_b, up, qmp = _norm_proj(xp, norm1_g[0], w_qkv, groups, (BF16, F32, BF16, F32, BF16, F32, BF16),
                                                 ROW_BLOCK)
    mk_p, mv_p = _norm_proj(mem_prompt.reshape(B * N_MEM, D), mem_norm_g[0], w_mem,
                            ((0, MEM_WIDTH, (1.0,)), (MEM_WIDTH, MEM_WIDTH, (1.0,))), (F32, F32), ROW_BLOCK)
    op = _prompt_attn(lams, qp, kp_b, vp_b, prompt_bias, subln_g[0], B, S, ATTN_BLOCK)
    pop, omp = _prompt_side(up, qmp, mk_p, mv_p, wbd, pool_scale[0], B, S, ROW_BLOCK)
    x1p = _merge(xp, op, pop, omp, norm1_g[0], w_gl, wa, wp, wm, wout, ROW_BLOCK)
    yp, tail_p = _conv_ffn(x1p, norm2_g[0], wg, wu, ffn_conv_w[0], ffn_conv_b[0], wd, final_norm_g, ROW_BLOCK,
                           batch=B, seq=S)

    xs = x_sample.reshape(DB * DS, D)
    groups_s = ((0, 512, (QK_SCALE,)), (512, 512, (1.0,)), (1024, 512, (1.0,)), (1536, 256, (1.0,)),
                (1792, 256, (MEM_SCALE,)))
    qs, ks, vs, us, qms = _norm_proj(xs, norm1_g[0], w_qkv, groups_s, (F32,) * 5, ROW_BLOCK)
    seq3 = lambda a: a.reshape(DB, DS, a.shape[-1])
    state16 = jnp.pad(state_pool[0], ((0, 0), (16 - POOL_STATE, 0), (0, 0)))
    os_, oms, pos_ = _sample_step(page_table, lams, seq3(qs), seq3(ks), seq3(vs), sample_bias, subln_g[0], seq3(qms),
                                  cache_mem_k, cache_mem_v, seq3(us), state16, wbd, pool_scale[0], cache_k, cache_v)
    flat = lambda a: a.reshape(DB * DS, a.shape[-1])
    x1s = _merge(xs, flat(os_), flat(pos_), flat(oms), norm1_g[0], w_gl, wa, wp, wm, wout, ROW_BLOCK)
    conv_prefix = jnp.pad(state_ffn_conv[0], ((0, 0), (0, DS - 2), (0, 0))).reshape(DB * DS, D_FF)
    ys, gate_s = _conv_ffn(x1s, norm2_g[0], wg, wu, ffn_conv_w[0], ffn_conv_b[0], wd, final_norm_g, ROW_BLOCK,
                           prefix=conv_prefix)

    heads = lambda a, n: a.reshape(1, n, -1, N_HEADS, V_DIM)
    new_pool_p = up.reshape(B, S, POOL_WIDTH)[:, S - POOL_STATE:][None]
    new_pool_s = jnp.concatenate([state_pool[0][:, DS:], seq3(us)], axis=1)[None]
    new_conv_p = tail_p[:, 6:8][None]
    new_conv_s = gate_s.reshape(DB, DS, D_FF)[:, DS - 2:][None]
    mem_heads = lambda a: a.reshape(1, B, N_MEM, N_HEADS, MEM_HEAD_DIM)
    return (yp.reshape(B, S, D), ys.reshape(DB, DS, D), heads(kp, B), heads(vp, B), heads(ks, DB), heads(vs, DB),
            new_pool_p, new_pool_s, new_conv_p, new_conv_s, mem_heads(mk_p), mem_heads(mv_p))
```

```python
import functools
import math

import jax
import jax.numpy as jnp
from jax import lax
from jax.experimental import pallas as pl
from jax.experimental.pallas import tpu as pltpu

F32 = jnp.float32
BF16 = jnp.bfloat16

D_MODEL = 1024
N_HEADS = 4
HEAD_DIM = 64
V_DIM = 2 * HEAD_DIM
ATTN_WIDTH = N_HEADS * V_DIM
POOL_WINDOWS = (2, 4, 8, 16)
POOL_GROUP_DIM = 64
POOL_WIDTH = 256
POOL_STATE = 15
MEM_HEAD_DIM = 64
MEM_WIDTH = 256
N_MEM = 256
D_FF = 2816
N_BUCKETS = 32
MAX_DISTANCE = 128
EPS = 1e-6
NEG_INF = -1e30
QK_SCALE = HEAD_DIM ** -0.5
MEM_SCALE = MEM_HEAD_DIM ** -0.5
LAM_INIT = 0.8 - 0.6 * math.exp(-0.3 * 0)
SUBLN_SCALE = 1.0 - LAM_INIT

ATTN_TQ = 128
ATTN_TK = 256
ROW_BLOCK = 512
FFN_CHUNK = 256
SAMPLE_KPAD = 128
VMEM_LIMIT = 56 * 1024 * 1024


def _cparams(*sem):
    return pltpu.CompilerParams(dimension_semantics=sem, vmem_limit_bytes=VMEM_LIMIT)


def _rmsnorm(x, g):
    return x * lax.rsqrt(jnp.mean(x * x, axis=-1, keepdims=True) + EPS) * g


def _dot(a, b):
    return jnp.dot(a, b, preferred_element_type=F32)


def _dot_nt(a, b):
    return lax.dot_general(a, b, (((1,), (1,)), ((), ())), preferred_element_type=F32)


def _softmax_rows(s):
    m = jnp.max(s, axis=-1, keepdims=True)
    p = jnp.exp(s - m)
    return p / jnp.sum(p, axis=-1, keepdims=True)


def _lam_value(lq1, lk1, lq2, lk2):
    a = jnp.sum(lq1 * lk1, axis=-1, keepdims=True)
    b = jnp.sum(lq2 * lk2, axis=-1, keepdims=True)
    return jnp.exp(a) - jnp.exp(b) + LAM_INIT


def _norm_proj_kernel(x_ref, g_ref, w_ref, *out_refs, groups):
    h = _rmsnorm(x_ref[...], g_ref[...]).astype(BF16)
    k = 0
    for start, size, outs in groups:
        y = _dot(h, w_ref[:, start:start + size])
        for scale, key_block in outs:
            o_ref = out_refs[k]
            k += 1
            ys = y if scale == 1.0 else y * scale
            if key_block:
                yt = ys.T
                for b in range(o_ref.shape[0]):
                    o_ref[b] = yt[:, b * key_block:(b + 1) * key_block].astype(o_ref.dtype)
            else:
                o_ref[...] = ys.astype(o_ref.dtype)


def _norm_proj(x2d, g, w_bf16, groups, out_dtypes, tm):
    rows, d = x2d.shape
    n_cols = w_bf16.shape[1]
    out_shape, out_specs = [], []
    flat_outs = [(size, kb) for _, size, outs in groups for _, kb in outs]
    for (size, kb), dt in zip(flat_outs, out_dtypes):
        if kb:
            out_shape.append(jax.ShapeDtypeStruct((rows // kb, size, kb), dt))
            out_specs.append(pl.BlockSpec((tm // kb, size, kb), lambda i: (i, 0, 0)))
        else:
            out_shape.append(jax.ShapeDtypeStruct((rows, size), dt))
            out_specs.append(pl.BlockSpec((tm, size), lambda i: (i, 0)))
    return pl.pallas_call(
        functools.partial(_norm_proj_kernel, groups=groups),
        grid=(rows // tm,),
        in_specs=[pl.BlockSpec((tm, d), lambda i: (i, 0)),
                  pl.BlockSpec((1, d), lambda i: (0, 0)),
                  pl.BlockSpec((d, n_cols), lambda i: (0, 0), pipeline_mode=pl.Buffered(1))],
        out_specs=out_specs,
        out_shape=out_shape,
        compiler_params=_cparams("parallel"),
        name="norm_proj",
    )(x2d, g.reshape(1, d), w_bf16)


def _rel_bucket(rel):
    n = jnp.maximum(rel, 0)
    max_exact = N_BUCKETS // 2
    nf = jnp.maximum(n, 1).astype(F32)
    large = max_exact + (jnp.log(nf / max_exact) / math.log(MAX_DISTANCE / max_exact)
                         * (N_BUCKETS - max_exact)).astype(jnp.int32)
    large = jnp.minimum(large, N_BUCKETS - 1)
    return jnp.where(n < max_exact, n, large)


def _bias_from_rel(rel, rb_ref, head, visible):
    bucket = _rel_bucket(rel)
    acc = jnp.zeros(rel.shape, F32)
    for b in range(N_BUCKETS):
        acc = jnp.where(bucket == b, rb_ref[b, head], acc)
    return jnp.where(visible, acc, NEG_INF)


def _bias_kernel(rb_ref, pb_ref, sb_ref, *, tq, tk, past, n_new):
    n_kinds = pb_ref.shape[1]
    key = lax.broadcasted_iota(jnp.int32, (tk, 2 * tq), 0)
    qry = lax.broadcasted_iota(jnp.int32, (tk, 2 * tq), 1) % tq
    for kind in range(n_kinds):
        rel = kind * tq + qry - key
        for h in range(N_HEADS):
            pb_ref[h, kind] = _bias_from_rel(rel, rb_ref, h, rel >= 0)
    rows_per_head = 2 * n_new
    n_cols = sb_ref.shape[1]
    qi = lax.broadcasted_iota(jnp.int32, (rows_per_head, n_cols), 0) % n_new
    col = lax.broadcasted_iota(jnp.int32, (rows_per_head, n_cols), 1)
    rel = past + qi - col // N_HEADS
    for h in range(N_HEADS):
        visible = jnp.where(col % N_HEADS == h, rel, -1) >= 0
        sb_ref[h * rows_per_head:(h + 1) * rows_per_head, :] = _bias_from_rel(rel, rb_ref, h, visible)


def _bias_tables(rel_bias, tq, tk, past, n_new):
    n_kinds = -(-(tk + MAX_DISTANCE - 1) // tq) + 1
    n_cols = N_HEADS * past + SAMPLE_KPAD
    return pl.pallas_call(
        functools.partial(_bias_kernel, tq=tq, tk=tk, past=past, n_new=n_new),
        in_specs=[pl.BlockSpec(memory_space=pltpu.SMEM)],
        out_specs=[pl.BlockSpec(memory_space=pltpu.VMEM), pl.BlockSpec(memory_space=pltpu.VMEM)],
        out_shape=[jax.ShapeDtypeStruct((N_HEADS, n_kinds, tk, 2 * tq), F32),
                   jax.ShapeDtypeStruct((N_HEADS * 2 * n_new, n_cols), F32)],
        compiler_params=pltpu.CompilerParams(vmem_limit_bytes=VMEM_LIMIT),
        name="bias_tables",
    )(rel_bias)


def _subln(o, g):
    return _rmsnorm(o, g) * SUBLN_SCALE


def _two_map_queries(q_h):
    lane = lax.broadcasted_iota(jnp.int32, q_h.shape, 1)
    zero = jnp.zeros_like(q_h)
    return jnp.concatenate([jnp.where(lane < HEAD_DIM, q_h, zero), jnp.where(lane >= HEAD_DIM, q_h, zero)], axis=0)


def _prompt_attn_kernel(lq1_ref, lk1_ref, lq2_ref, lk2_ref, q_ref, k_ref, vt_ref, bias_ref, g_ref, o_ref,
                        q2_ref, m_ref, l_ref, acc_ref, s0_ref, *, tq, tk):
    i = pl.program_id(1)
    lam = _lam_value(lq1_ref[...], lk1_ref[...], lq2_ref[...], lk2_ref[...])
    n_kinds = bias_ref.shape[1]
    for h in range(N_HEADS):
        q2_ref[h] = _two_map_queries(q_ref[:, h * V_DIM:(h + 1) * V_DIM])
    m_ref[...] = jnp.full(m_ref.shape, NEG_INF, F32)
    l_ref[...] = jnp.zeros(l_ref.shape, F32)
    acc_ref[...] = jnp.zeros(acc_ref.shape, F32)

    last = (i * tq + tq - 1) // tk

    def scores(j, h):
        r0 = pl.multiple_of(j * tk, tk)
        kind = jnp.minimum((i * tq - j * tk) // tq, n_kinds - 1)
        hs = slice(h * V_DIM, (h + 1) * V_DIM)
        return _dot_nt(k_ref[pl.ds(r0, tk), hs], q2_ref[h]) + bias_ref[h, kind]

    ahead = 2
    for h in range(ahead):
        s0_ref[h] = scores(0, h)

    def body(j, carry):
        pending = [s0_ref[h] for h in range(ahead)]
        j_next = jnp.minimum(j + 1, last)
        for h in range(N_HEADS):
            hs = slice(h * V_DIM, (h + 1) * V_DIM)
            s = pending.pop(0)
            pending.append(scores(j, h + ahead) if h + ahead < N_HEADS else scores(j_next, h + ahead - N_HEADS))
            m_old = m_ref[h]
            m_new = jnp.maximum(m_old, jnp.max(s, axis=0, keepdims=True))
            alpha = jnp.exp(m_old - m_new)
            p = jnp.exp(s - m_new)
            l_ref[h] = alpha * l_ref[h] + jnp.sum(p, axis=0, keepdims=True)
            m_ref[h] = m_new
            acc_ref[h] = alpha * acc_ref[h] + _dot(vt_ref[j, hs, :], p.astype(BF16))
        for h in range(ahead):
            s0_ref[h] = pending[h]
        return carry

    lax.fori_loop(0, last + 1, body, 0)
    for h in range(N_HEADS):
        out_t = acc_ref[h] / l_ref[h]
        o_t = out_t[:, :tq] - lam * out_t[:, tq:]
        o_ref[:, h * V_DIM:(h + 1) * V_DIM] = _subln(o_t.T, g_ref[...]).astype(o_ref.dtype)


def _prompt_attn(lams, q, k, vt, bias, subln_g, batch, seq, tq, tk):
    vec = pl.BlockSpec((1, HEAD_DIM), lambda b, i: (0, 0))
    nq = seq // tq
    return pl.pallas_call(
        functools.partial(_prompt_attn_kernel, tq=tq, tk=tk),
        grid=(batch, nq),
        in_specs=[vec, vec, vec, vec,
                  pl.BlockSpec((tq, ATTN_WIDTH), lambda b, i: (b * nq + i, 0)),
                  pl.BlockSpec((seq, ATTN_WIDTH), lambda b, i: (b, 0)),
                  pl.BlockSpec((seq // tk, ATTN_WIDTH, tk), lambda b, i: (b, 0, 0)),
                  pl.BlockSpec(bias.shape, lambda b, i: (0, 0, 0, 0), pipeline_mode=pl.Buffered(1)),
                  pl.BlockSpec((1, V_DIM), lambda b, i: (0, 0))],
        out_specs=pl.BlockSpec((tq, ATTN_WIDTH), lambda b, i: (b * nq + i, 0)),
        out_shape=jax.ShapeDtypeStruct((batch * seq, ATTN_WIDTH), BF16),
        scratch_shapes=[pltpu.VMEM((N_HEADS, 2 * tq, V_DIM), BF16),
                        pltpu.VMEM((N_HEADS, 1, 2 * tq), F32),
                        pltpu.VMEM((N_HEADS, 1, 2 * tq), F32),
                        pltpu.VMEM((N_HEADS, V_DIM, 2 * tq), F32),
                        pltpu.VMEM((2, tk, 2 * tq), F32)],
        compiler_params=_cparams("parallel", "parallel"),
        name="prompt_attn",
    )(*lams, q, k, vt, bias, subln_g.reshape(1, V_DIM))


def _pool_branch(ext, u, pos0, wbd, scale):
    n, c = u.shape
    p = ext.shape[0] - n
    s2 = ext + pltpu.roll(ext, 1, 0)
    s4 = s2 + pltpu.roll(s2, 2, 0)
    s8 = s4 + pltpu.roll(s4, 4, 0)
    s16 = s8 + pltpu.roll(s8, 8, 0)
    lane = lax.broadcasted_iota(jnp.int32, (n, c), 1)
    grp = lane // POOL_GROUP_DIM
    win = jnp.where(grp == 0, s2[p:], jnp.where(grp == 1, s4[p:], jnp.where(grp == 2, s8[p:], s16[p:])))
    width = jnp.where(grp == 0, 2, jnp.where(grp == 1, 4, jnp.where(grp == 2, 8, 16)))
    pos = pos0 + lax.broadcasted_iota(jnp.int32, (n, c), 0)
    cnt = jnp.minimum(pos + 1, width).astype(F32)
    d = win / cnt - u
    return _dot(d.astype(BF16), wbd) * scale


def _prompt_side_kernel(u_ref, uprev_ref, qm_ref, mk_ref, mv_ref, wbd_ref, ps_ref, po_ref, om_ref, *, tm):
    i = pl.program_id(1)
    u = u_ref[...]
    prev = jnp.where(i > 0, uprev_ref[...], 0.0)
    ext = jnp.concatenate([prev, u], axis=0)
    po_ref[...] = _pool_branch(ext, u, i * tm, wbd_ref[...], ps_ref[...]).astype(po_ref.dtype)
    for h in range(N_HEADS):
        sl = slice(h * MEM_HEAD_DIM, (h + 1) * MEM_HEAD_DIM)
        p = _softmax_rows(_dot_nt(qm_ref[:, sl], mk_ref[:, sl].astype(BF16)))
        om_ref[:, sl] = _dot(p.astype(BF16), mv_ref[:, sl].astype(BF16)).astype(om_ref.dtype)


def _prompt_side(u, qm, mk, mv, wbd, pool_scale, batch, seq, tm):
    nblk = seq // tm
    halo = 16
    row = lambda b, i: (b * nblk + i, 0)
    return pl.pallas_call(
        functools.partial(_prompt_side_kernel, tm=tm),
        grid=(batch, nblk),
        in_specs=[pl.BlockSpec((tm, POOL_WIDTH), row),
                  pl.BlockSpec((halo, POOL_WIDTH),
                               lambda b, i: (jnp.maximum((b * nblk + i) * (tm // halo) - 1, 0), 0)),
                  pl.BlockSpec((tm, MEM_WIDTH), row),
                  pl.BlockSpec((N_MEM, MEM_WIDTH), lambda b, i: (b, 0)),
                  pl.BlockSpec((N_MEM, MEM_WIDTH), lambda b, i: (b, 0)),
                  pl.BlockSpec((POOL_WIDTH, POOL_WIDTH), lambda b, i: (0, 0)),
                  pl.BlockSpec((1, POOL_WIDTH), lambda b, i: (0, 0))],
        out_specs=[pl.BlockSpec((tm, POOL_WIDTH), row), pl.BlockSpec((tm, MEM_WIDTH), row)],
        out_shape=[jax.ShapeDtypeStruct((batch * seq, POOL_WIDTH), BF16),
                   jax.ShapeDtypeStruct((batch * seq, MEM_WIDTH), BF16)],
        compiler_params=_cparams("parallel", "parallel"),
        name="prompt_side",
    )(u, u, qm, mk, mv, wbd, pool_scale.reshape(1, POOL_WIDTH))


def _sample_step_kernel(pt_ref, lq1_ref, lk1_ref, lq2_ref, lk2_ref, q_ref, kn_ref, vn_ref, bias_ref, g_ref,
                        qm_ref, mkt_ref, mvt_ref, u_ref, st_ref, wbd_ref, ps_ref, *refs, n_pages, n_new, past):
    del pt_ref
    kp = refs[:n_pages]
    vp = refs[n_pages:2 * n_pages]
    o_ref, om_ref, po_ref, s_ref = refs[2 * n_pages:]
    lam = _lam_value(lq1_ref[...], lk1_ref[...], lq2_ref[...], lk2_ref[...])
    rows = kp[0].shape[0]
    n_tail = s_ref.shape[1] - n_pages * rows
    q2 = jnp.concatenate([_two_map_queries(q_ref[:, h * V_DIM:(h + 1) * V_DIM]) for h in range(N_HEADS)],
                         axis=0).astype(BF16)
    pad = jnp.zeros((n_tail - kn_ref.shape[0], V_DIM), F32)
    k_tail = jnp.concatenate([kn_ref[...], pad], axis=0).astype(BF16)
    v_tail = jnp.concatenate([vn_ref[...], pad], axis=0).astype(BF16)
    for p in range(n_pages):
        s_ref[:, p * rows:(p + 1) * rows] = _dot_nt(q2, kp[p][...].astype(BF16))
    s_ref[:, n_pages * rows:] = _dot_nt(q2, k_tail)
    prob = _softmax_rows(s_ref[...] + bias_ref[...])
    hr = 2 * n_new
    a = jnp.concatenate([prob[h * hr:h * hr + n_new] - lam * prob[h * hr + n_new:(h + 1) * hr]
                         for h in range(N_HEADS)], axis=0).astype(BF16)
    o_all = _dot(a[:, n_pages * rows:], v_tail)
    for p in range(n_pages):
        o_all = o_all + _dot(a[:, p * rows:(p + 1) * rows], vp[p][...].astype(BF16))
    for h in range(N_HEADS):
        o_ref[:, h * V_DIM:(h + 1) * V_DIM] = _subln(o_all[h * n_new:(h + 1) * n_new], g_ref[...]).astype(o_ref.dtype)
    for h in range(N_HEADS):
        sl = slice(h * MEM_HEAD_DIM, (h + 1) * MEM_HEAD_DIM)
        pm = _softmax_rows(_dot(qm_ref[:, sl].astype(BF16), mkt_ref[h].astype(BF16)))
        om_ref[:, sl] = _dot_nt(pm.astype(BF16), mvt_ref[h].astype(BF16)).astype(om_ref.dtype)
    u = u_ref[...]
    ext = jnp.concatenate([st_ref[...], u], axis=0)
    po_ref[...] = _pool_branch(ext, u, past, wbd_ref[...], ps_ref[...]).astype(po_ref.dtype)


def _sample_step(page_table, lams, q, k_new, v_new, bias, subln_g, qm, mem_kt, mem_vt, u, state16, wbd, pool_scale,
                 cache_k, cache_v):
    n_seq, n_pages = page_table.shape
    n_new = q.shape[1]
    rows = cache_k.shape[1]
    past = n_pages * rows // N_HEADS
    vec = pl.BlockSpec((1, HEAD_DIM), lambda s, pt: (0, 0))
    per_seq = lambda a: pl.BlockSpec((None,) + a.shape[1:], lambda s, pt: (s,) + (0,) * (a.ndim - 1))
    const = lambda a: pl.BlockSpec(a.shape, lambda s, pt: (0,) * a.ndim)
    page_spec = lambda p: pl.BlockSpec((None, rows, V_DIM), lambda s, pt, p=p: (pt[s, p], 0, 0))
    g = subln_g.reshape(1, V_DIM)
    ps = pool_scale.reshape(1, POOL_WIDTH)
    in_specs = [vec, vec, vec, vec, per_seq(q), per_seq(k_new), per_seq(v_new), const(bias), const(g),
                per_seq(qm), per_seq(mem_kt), per_seq(mem_vt), per_seq(u), per_seq(state16), const(wbd), const(ps)]
    in_specs += [page_spec(p) for p in range(n_pages)] * 2
    out3 = lambda width: jax.ShapeDtypeStruct((n_seq, n_new, width), F32)
    out_spec = lambda width: pl.BlockSpec((None, n_new, width), lambda s, pt: (s, 0, 0))
    grid_spec = pltpu.PrefetchScalarGridSpec(
        num_scalar_prefetch=1,
        grid=(n_seq,),
        in_specs=in_specs,
        out_specs=[out_spec(ATTN_WIDTH), out_spec(MEM_WIDTH), out_spec(POOL_WIDTH)],
        scratch_shapes=[pltpu.VMEM(bias.shape, F32)],
    )
    return pl.pallas_call(
        functools.partial(_sample_step_kernel, n_pages=n_pages, n_new=n_new, past=past),
        grid_spec=grid_spec,
        out_shape=[out3(ATTN_WIDTH), out3(MEM_WIDTH), out3(POOL_WIDTH)],
        compiler_params=_cparams("parallel"),
        name="sample_step",
    )(page_table, *lams, q, k_new, v_new, bias, g, qm, mem_kt, mem_vt, u, state16, wbd, ps,
      *([cache_k] * n_pages), *([cache_v] * n_pages))


def _merge_kernel(x_ref, o_ref, po_ref, om_ref, g_ref, wgl_ref, wa_ref, wp_ref, wm_ref, wout_ref, x1_ref):
    x = x_ref[...]
    h = _rmsnorm(x, g_ref[...]).astype(BF16)
    d = D_MODEL
    merged = jax.nn.sigmoid(_dot(h, wgl_ref[:, 0:d])) * _dot(o_ref[...].astype(BF16), wa_ref[...])
    merged = merged + jax.nn.sigmoid(_dot(h, wgl_ref[:, d:2 * d])) * _dot(po_ref[...].astype(BF16), wp_ref[...])
    merged = merged + jax.nn.sigmoid(_dot(h, wgl_ref[:, 2 * d:3 * d])) * _dot(om_ref[...].astype(BF16), wm_ref[...])
    x1_ref[...] = x + _dot(merged.astype(BF16), wout_ref[...])


def _merge(x2d, o, po, om, norm_g, wgl, wa, wp, wm, wout, tm):
    rows, d = x2d.shape
    row = lambda width: pl.BlockSpec((tm, width), lambda i: (i, 0))
    full = lambda a: pl.BlockSpec(a.shape, lambda i: (0, 0), pipeline_mode=pl.Buffered(1))
    g = norm_g.reshape(1, d)
    return pl.pallas_call(
        _merge_kernel,
        grid=(rows // tm,),
        in_specs=[row(d), row(ATTN_WIDTH), row(POOL_WIDTH), row(MEM_WIDTH), full(g), full(wgl), full(wa), full(wp),
                  full(wm), full(wout)],
        out_specs=row(d),
        out_shape=jax.ShapeDtypeStruct((rows, d), F32),
        compiler_params=_cparams("parallel"),
        name="merge",
    )(x2d, o, po, om, g, wgl, wa, wp, wm, wout)


def _conv_ffn_kernel(x_ref, g2_ref, wg_ref, wu_ref, cw_ref, cb_ref, wd_ref, gf_ref, *refs, tm, fc, seq_rows):
    if seq_rows:
        pre_ref, y_ref, gate_ref = refs
    else:
        y_ref, tail_ref, carry_ref = refs

        @pl.when(pl.program_id(1) == 0)
        def _():
            carry_ref[...] = jnp.zeros_like(carry_ref)

    x = x_ref[...]
    h = _rmsnorm(x, g2_ref[...]).astype(BF16)
    acc = jnp.zeros((tm, D_MODEL), F32)
    for c0 in range(0, D_FF, fc):
        cs = slice(c0, c0 + fc)
        gate = _dot(h, wg_ref[:, cs])
        up = _dot(h, wu_ref[:, cs])
        if seq_rows:
            pre = pre_ref[:, cs]
            t = lax.broadcasted_iota(jnp.int32, (tm, fc), 0) % seq_rows
            back1 = jnp.where(t >= 1, pltpu.roll(gate, 1, 0), pltpu.roll(pre, tm - 1, 0))
            back2 = jnp.where(t >= 2, pltpu.roll(gate, 2, 0), pre)
            gate_ref[:, cs] = gate
        else:
            ext = jnp.concatenate([carry_ref[:, cs], gate], axis=0)
            back1 = pltpu.roll(ext, 1, 0)[8:]
            back2 = pltpu.roll(ext, 2, 0)[8:]
            carry_ref[:, cs] = gate[tm - 8:]
        gc = cb_ref[:, cs] + cw_ref[0:1, cs] * back2
        gc = gc + cw_ref[1:2, cs] * back1
        gc = gc + cw_ref[2:3, cs] * gate
        act = (jax.nn.gelu(gc) * up).astype(BF16)
        acc = acc + _dot(act, wd_ref[cs, :])
    if not seq_rows:
        tail_ref[...] = carry_ref[...]
    y_ref[...] = _rmsnorm(x + acc, gf_ref[...])


def _conv_ffn(x1, norm_g, wg, wu, conv_w, conv_b, wd, final_g, tm, batch=None, seq=None, prefix=None):
    rows, d = x1.shape
    g2 = norm_g.reshape(1, d)
    gf = final_g.reshape(1, d)
    cb = conv_b.reshape(1, D_FF)
    if prefix is None:
        nblk = seq // tm
        row = lambda width: pl.BlockSpec((tm, width), lambda b, i: (b * nblk + i, 0))
        full = lambda a: pl.BlockSpec(a.shape, lambda b, i: (0, 0), pipeline_mode=pl.Buffered(1))
        return pl.pallas_call(
            functools.partial(_conv_ffn_kernel, tm=tm, fc=FFN_CHUNK, seq_rows=0),
            grid=(batch, nblk),
            in_specs=[row(d), full(g2), full(wg), full(wu), full(conv_w), full(cb), full(wd), full(gf)],
            out_specs=[row(d), pl.BlockSpec((None, 8, D_FF), lambda b, i: (b, 0, 0))],
            out_shape=[jax.ShapeDtypeStruct((rows, d), F32), jax.ShapeDtypeStruct((batch, 8, D_FF), F32)],
            scratch_shapes=[pltpu.VMEM((8, D_FF), F32)],
            compiler_params=_cparams("parallel", "arbitrary"),
            name="conv_ffn_prompt",
        )(x1, g2, wg, wu, conv_w, cb, wd, gf)
    row = lambda width: pl.BlockSpec((tm, width), lambda i: (i, 0))
    full = lambda a: pl.BlockSpec(a.shape, lambda i: (0, 0), pipeline_mode=pl.Buffered(1))
    return pl.pallas_call(
        functools.partial(_conv_ffn_kernel, tm=tm, fc=FFN_CHUNK, seq_rows=8),
        grid=(rows // tm,),
        in_specs=[row(d), full(g2), full(wg), full(wu), full(conv_w), full(cb), full(wd), full(gf), row(D_FF)],
        out_specs=[row(d), row(D_FF)],
        out_shape=[jax.ShapeDtypeStruct((rows, d), F32), jax.ShapeDtypeStruct((rows, D_FF), F32)],
        compiler_params=_cparams("parallel"),
        name="conv_ffn_sample",
    )(x1, g2, wg, wu, conv_w, cb, wd, gf, prefix)


def kernel(x_prompt, x_sample, mem_prompt, cache_k, cache_v, page_table, state_pool, state_ffn_conv, cache_mem_k, cache_mem_v, norm1_g, w_in, lam_q1, lam_k1, lam_q2, lam_k2, subln_g, w_pool_grp, pool_scale, w_br_attn, w_br_pool, w_br_mem, mem_norm_g, w_mem_kv, w_out, norm2_g, w_ffn_gate, w_ffn_up, ffn_conv_w, ffn_conv_b, w_ffn_down, rel_bias, final_norm_g):
    depth = w_in.shape[0]
    assert depth == 1, "single-layer step only"
    B, S, D = x_prompt.shape
    DB, DS, _ = x_sample.shape
    n_phys, page = cache_k.shape[1], cache_k.shape[2]
    n_pages = page_table.shape[1]
    past = n_pages * page
    assert DS == 8 and S % ROW_BLOCK == 0 and (DB * DS) % ROW_BLOCK == 0
    assert ATTN_TK % ATTN_TQ == 0 and ROW_BLOCK % ATTN_TK == 0 and S % ATTN_TK == 0

    w_qkv = w_in[0, :, :2048].astype(BF16)
    w_gl = w_in[0, :, 2048:].astype(BF16)
    wa, wp, wm = w_br_attn[0].astype(BF16), w_br_pool[0].astype(BF16), w_br_mem[0].astype(BF16)
    wout = w_out[0].astype(BF16)
    wg, wu, wd = w_ffn_gate[0].astype(BF16), w_ffn_up[0].astype(BF16), w_ffn_down[0].astype(BF16)
    w_mem = w_mem_kv[0].astype(BF16)
    wbd = jnp.zeros((POOL_WIDTH, POOL_WIDTH), F32)
    for gi in range(len(POOL_WINDOWS)):
        sl = slice(gi * POOL_GROUP_DIM, (gi + 1) * POOL_GROUP_DIM)
        wbd = wbd.at[sl, sl].set(w_pool_grp[0, gi])
    wbd = wbd.astype(BF16)
    lams = tuple(a[0].reshape(1, HEAD_DIM) for a in (lam_q1, lam_k1, lam_q2, lam_k2))

    prompt_bias, sample_bias = _bias_tables(rel_bias, ATTN_TQ, ATTN_TK, past, DS)

    xp = x_prompt.reshape(B * S, D)
    groups = ((0, 512, ((QK_SCALE, 0),)), (512, 512, ((1.0, 0), (1.0, 0))), (1024, 512, ((1.0, 0), (1.0, ATTN_TK))),
              (1536, 256, ((1.0, 0),)), (1792, 256, ((MEM_SCALE, 0),)))
    qp, kp, kp_b, vp, vp_t, up, qmp = _norm_proj(xp, norm1_g[0], w_qkv, groups,
                                                 (BF16, F32, BF16, F32, BF16, F32, BF16), ROW_BLOCK)
    mk_p, mv_p = _norm_proj(mem_prompt.reshape(B * N_MEM, D), mem_norm_g[0], w_mem,
                            ((0, MEM_WIDTH, ((1.0, 0),)), (MEM_WIDTH, MEM_WIDTH, ((1.0, 0),))), (F32, F32), ROW_BLOCK)
    op = _prompt_attn(lams, qp, kp_b, vp_t, prompt_bias, subln_g[0], B, S, ATTN_TQ, ATTN_TK)
    pop, omp = _prompt_side(up, qmp, mk_p, mv_p, wbd, pool_scale[0], B, S, ROW_BLOCK)
    x1p = _merge(xp, op, pop, omp, norm1_g[0], w_gl, wa, wp, wm, wout, ROW_BLOCK)
    yp, tail_p = _conv_ffn(x1p, norm2_g[0], wg, wu, ffn_conv_w[0], ffn_conv_b[0], wd, final_norm_g, ROW_BLOCK,
                           batch=B, seq=S)

    xs = x_sample.reshape(DB * DS, D)
    groups_s = ((0, 512, ((QK_SCALE, 0),)), (512, 512, ((1.0, 0),)), (1024, 512, ((1.0, 0),)),
                (1536, 256, ((1.0, 0),)), (1792, 256, ((MEM_SCALE, 0),)))
    qs, ks, vs, us, qms = _norm_proj(xs, norm1_g[0], w_qkv, groups_s, (F32,) * 5, ROW_BLOCK)
    seq3 = lambda a: a.reshape(DB, DS, a.shape[-1])
    head_rows = lambda a: a.reshape(DB, DS * N_HEADS, V_DIM)
    state16 = jnp.pad(state_pool[0], ((0, 0), (16 - POOL_STATE, 0), (0, 0)))
    mem_t = lambda a: jnp.transpose(a[0], (0, 2, 3, 1))
    pages = lambda a: a[0].reshape(n_phys, page * N_HEADS, V_DIM)
    os_, oms, pos_ = _sample_step(page_table, lams, seq3(qs), head_rows(ks), head_rows(vs), sample_bias, subln_g[0],
                                  seq3(qms), mem_t(cache_mem_k), mem_t(cache_mem_v), seq3(us), state16, wbd,
                                  pool_scale[0], pages(cache_k), pages(cache_v))
    flat = lambda a: a.reshape(DB * DS, a.shape[-1])
    x1s = _merge(xs, flat(os_), flat(pos_), flat(oms), norm1_g[0], w_gl, wa, wp, wm, wout, ROW_BLOCK)
    conv_prefix = jnp.pad(state_ffn_conv[0], ((0, 0), (0, DS - 2), (0, 0))).reshape(DB * DS, D_FF)
    ys, gate_s = _conv_ffn(x1s, norm2_g[0], wg, wu, ffn_conv_w[0], ffn_conv_b[0], wd, final_norm_g, ROW_BLOCK,
                           prefix=conv_prefix)

    heads = lambda a, n: a.reshape(1, n, -1, N_HEADS, V_DIM)
    new_pool_p = up.reshape(B, S, POOL_WIDTH)[:, S - POOL_STATE:][None]
    new_pool_s = jnp.concatenate([state_pool[0][:, DS:], seq3(us)], axis=1)[None]
    new_conv_p = tail_p[:, 6:8][None]
    new_conv_s = gate_s.reshape(DB, DS, D_FF)[:, DS - 2:][None]
    mem_heads = lambda a: a.reshape(1, B, N_MEM, N_HEADS, MEM_HEAD_DIM)
    return (yp.reshape(B, S, D), ys.reshape(DB, DS, D), heads(kp, B), heads(vp, B), heads(ks, DB), heads(vs, DB),
            new_pool_p, new_pool_s, new_conv_p, new_conv_s, mem_heads(mk_p), mem_heads(mv_p))
```

```python
import functools
import math

import jax
import jax.numpy as jnp
from jax import lax
from jax.experimental import pallas as pl
from jax.experimental.pallas import tpu as pltpu

F32 = jnp.float32
BF16 = jnp.bfloat16

D_MODEL = 1024
N_HEADS = 4
HEAD_DIM = 64
V_DIM = 2 * HEAD_DIM
ATTN_WIDTH = N_HEADS * V_DIM
POOL_WINDOWS = (2, 4, 8, 16)
POOL_GROUP_DIM = 64
POOL_WIDTH = 256
POOL_STATE = 15
MEM_HEAD_DIM = 64
MEM_WIDTH = 256
N_MEM = 256
D_FF = 2816
N_BUCKETS = 32
MAX_DISTANCE = 128
EPS = 1e-6
NEG_INF = -1e30
QK_SCALE = HEAD_DIM ** -0.5
MEM_SCALE = MEM_HEAD_DIM ** -0.5
LAM_INIT = 0.8 - 0.6 * math.exp(-0.3 * 0)
SUBLN_SCALE = 1.0 - LAM_INIT
LOG2E = math.log2(math.e)

ATTN_TQ = 256
ATTN_TK = 256
ROW_BLOCK = 512
FFN_CHUNK = 256
SAMPLE_KPAD = 128
VMEM_LIMIT = 56 * 1024 * 1024
HEAD_ROWS = -1


def _cparams(*sem):
    return pltpu.CompilerParams(dimension_semantics=sem, vmem_limit_bytes=VMEM_LIMIT)


def _rmsnorm(x, g):
    return x * lax.rsqrt(jnp.mean(x * x, axis=-1, keepdims=True) + EPS) * g


def _dot(a, b):
    return jnp.dot(a, b, preferred_element_type=F32)


def _dot_nt(a, b):
    return lax.dot_general(a, b, (((1,), (1,)), ((), ())), preferred_element_type=F32)


def _softmax_rows(s):
    m = jnp.max(s, axis=-1, keepdims=True)
    p = jnp.exp(s - m)
    return p / jnp.sum(p, axis=-1, keepdims=True)


def _lam_value(lq1, lk1, lq2, lk2):
    a = jnp.sum(lq1 * lk1, axis=-1, keepdims=True)
    b = jnp.sum(lq2 * lk2, axis=-1, keepdims=True)
    return jnp.exp(a) - jnp.exp(b) + LAM_INIT


def _norm_proj_kernel(x_ref, g_ref, w_ref, *out_refs, groups):
    h = _rmsnorm(x_ref[...], g_ref[...]).astype(BF16)
    k = 0
    for start, size, outs in groups:
        y = _dot(h, w_ref[:, start:start + size])
        for scale, key_block in outs:
            o_ref = out_refs[k]
            k += 1
            ys = y if scale == 1.0 else y * scale
            if key_block == HEAD_ROWS:
                n = ys.shape[0]
                for hd in range(N_HEADS):
                    o_ref[pl.ds(hd, n, stride=N_HEADS), :] = ys[:, hd * V_DIM:(hd + 1) * V_DIM].astype(o_ref.dtype)
            elif key_block:
                yt = ys.T
                for b in range(o_ref.shape[0]):
                    o_ref[b] = yt[:, b * key_block:(b + 1) * key_block].astype(o_ref.dtype)
            else:
                o_ref[...] = ys.astype(o_ref.dtype)


def _norm_proj(x2d, g, w_bf16, groups, out_dtypes, tm):
    rows, d = x2d.shape
    n_cols = w_bf16.shape[1]
    out_shape, out_specs = [], []
    flat_outs = [(size, kb) for _, size, outs in groups for _, kb in outs]
    for (size, kb), dt in zip(flat_outs, out_dtypes):
        if kb == HEAD_ROWS:
            out_shape.append(jax.ShapeDtypeStruct((rows * N_HEADS, V_DIM), dt))
            out_specs.append(pl.BlockSpec((tm * N_HEADS, V_DIM), lambda i: (i, 0)))
        elif kb:
            out_shape.append(jax.ShapeDtypeStruct((rows // kb, size, kb), dt))
            out_specs.append(pl.BlockSpec((tm // kb, size, kb), lambda i: (i, 0, 0)))
        else:
            out_shape.append(jax.ShapeDtypeStruct((rows, size), dt))
            out_specs.append(pl.BlockSpec((tm, size), lambda i: (i, 0)))
    return pl.pallas_call(
        functools.partial(_norm_proj_kernel, groups=groups),
        grid=(rows // tm,),
        in_specs=[pl.BlockSpec((tm, d), lambda i: (i, 0)),
                  pl.BlockSpec((1, d), lambda i: (0, 0)),
                  pl.BlockSpec((d, n_cols), lambda i: (0, 0), pipeline_mode=pl.Buffered(1))],
        out_specs=out_specs,
        out_shape=out_shape,
        compiler_params=_cparams("parallel"),
        name="norm_proj",
    )(x2d, g.reshape(1, d), w_bf16)


def _rel_bucket(rel):
    n = jnp.maximum(rel, 0)
    max_exact = N_BUCKETS // 2
    nf = jnp.maximum(n, 1).astype(F32)
    large = max_exact + jnp.floor(jnp.log(nf / max_exact) / math.log(MAX_DISTANCE / max_exact)
                                  * (N_BUCKETS - max_exact)).astype(jnp.int32)
    large = jnp.minimum(large, N_BUCKETS - 1)
    return jnp.where(n < max_exact, n, large)


def _bias_from_rel(rel, rb_ref, head, visible):
    bucket = _rel_bucket(rel)
    acc = jnp.zeros(rel.shape, F32)
    for b in range(N_BUCKETS):
        acc = jnp.where(bucket == b, rb_ref[b, head], acc)
    return jnp.where(visible, acc, NEG_INF)


def _bias_kernel(rb_ref, pb_ref, sb_ref, *, tq, tk, past, n_new):
    n_kinds = pb_ref.shape[1]
    key = lax.broadcasted_iota(jnp.int32, (tk, 2 * tq), 0)
    qry = lax.broadcasted_iota(jnp.int32, (tk, 2 * tq), 1) % tq
    for kind in range(n_kinds):
        rel = kind * tq + qry - key
        for h in range(N_HEADS):
            pb_ref[h, kind] = _bias_from_rel(rel, rb_ref, h, rel >= 0) * LOG2E
    rows_per_head = 2 * n_new
    n_cols = sb_ref.shape[1]
    qi = lax.broadcasted_iota(jnp.int32, (rows_per_head, n_cols), 0) % n_new
    key = lax.broadcasted_iota(jnp.int32, (rows_per_head, n_cols), 1)
    rel = past + qi - key
    for h in range(N_HEADS):
        sb_ref[h * rows_per_head:(h + 1) * rows_per_head, :] = _bias_from_rel(rel, rb_ref, h, rel >= 0)


def _bias_tables(rel_bias, tq, tk, past, n_new):
    n_kinds = -(-(tk + MAX_DISTANCE - 1) // tq) + 1
    n_cols = past + SAMPLE_KPAD
    return pl.pallas_call(
        functools.partial(_bias_kernel, tq=tq, tk=tk, past=past, n_new=n_new),
        in_specs=[pl.BlockSpec(memory_space=pltpu.SMEM)],
        out_specs=[pl.BlockSpec(memory_space=pltpu.VMEM), pl.BlockSpec(memory_space=pltpu.VMEM)],
        out_shape=[jax.ShapeDtypeStruct((N_HEADS, n_kinds, tk, 2 * tq), F32),
                   jax.ShapeDtypeStruct((N_HEADS * 2 * n_new, n_cols), F32)],
        compiler_params=pltpu.CompilerParams(vmem_limit_bytes=VMEM_LIMIT),
        name="bias_tables",
    )(rel_bias)


def _subln(o, g):
    return _rmsnorm(o, g) * SUBLN_SCALE


def _two_map_queries(q_h):
    lane = lax.broadcasted_iota(jnp.int32, q_h.shape, 1)
    zero = jnp.zeros_like(q_h)
    return jnp.concatenate([jnp.where(lane < HEAD_DIM, q_h, zero), jnp.where(lane >= HEAD_DIM, q_h, zero)], axis=0)


def _prompt_attn_kernel(lq1_ref, lk1_ref, lq2_ref, lk2_ref, q_ref, k_ref, vt_ref, bias_ref, g_ref, o_ref,
                        q2_ref, m_ref, l_ref, acc_ref, s0_ref, *, tq, tk):
    i = pl.program_id(1)
    lam = _lam_value(lq1_ref[...], lk1_ref[...], lq2_ref[...], lk2_ref[...])
    n_kinds = bias_ref.shape[1]
    for h in range(N_HEADS):
        q2_ref[h] = _two_map_queries(q_ref[:, h * V_DIM:(h + 1) * V_DIM])
    m_ref[...] = jnp.full(m_ref.shape, NEG_INF, F32)
    l_ref[...] = jnp.zeros(l_ref.shape, F32)
    acc_ref[...] = jnp.zeros(acc_ref.shape, F32)

    last = (i * tq + tq - 1) // tk

    def scores(j, h):
        r0 = pl.multiple_of(j * tk, tk)
        kind = jnp.minimum((i * tq - j * tk) // tq, n_kinds - 1)
        hs = slice(h * V_DIM, (h + 1) * V_DIM)
        return _dot_nt(k_ref[pl.ds(r0, tk), hs], q2_ref[h]) + bias_ref[h, kind]

    ahead = 2
    for h in range(ahead):
        s0_ref[h] = scores(0, h)

    def body(j, carry):
        pending = [s0_ref[h] for h in range(ahead)]
        j_next = jnp.minimum(j + 1, last)
        for h in range(N_HEADS):
            hs = slice(h * V_DIM, (h + 1) * V_DIM)
            s = pending.pop(0)
            pending.append(scores(j, h + ahead) if h + ahead < N_HEADS else scores(j_next, h + ahead - N_HEADS))
            m_old = m_ref[h]
            m_new = jnp.maximum(m_old, jnp.max(s, axis=0, keepdims=True))
            alpha = jnp.exp2(m_old - m_new)
            p = jnp.exp2(s - m_new)
            l_ref[h] = alpha * l_ref[h] + jnp.sum(p, axis=0, keepdims=True)
            m_ref[h] = m_new
            acc_ref[h] = alpha * acc_ref[h] + _dot(vt_ref[j, hs, :], p.astype(BF16))
        for h in range(ahead):
            s0_ref[h] = pending[h]
        return carry

    lax.fori_loop(0, last + 1, body, 0)
    for h in range(N_HEADS):
        out_t = acc_ref[h] / l_ref[h]
        o_t = out_t[:, :tq] - lam * out_t[:, tq:]
        o_ref[:, h * V_DIM:(h + 1) * V_DIM] = _subln(o_t.T, g_ref[...]).astype(o_ref.dtype)


def _prompt_attn(lams, q, k, vt, bias, subln_g, batch, seq, tq, tk):
    vec = pl.BlockSpec((1, HEAD_DIM), lambda b, i: (0, 0))
    nq = seq // tq
    return pl.pallas_call(
        functools.partial(_prompt_attn_kernel, tq=tq, tk=tk),
        grid=(batch, nq),
        in_specs=[vec, vec, vec, vec,
                  pl.BlockSpec((tq, ATTN_WIDTH), lambda b, i: (b * nq + i, 0)),
                  pl.BlockSpec((seq, ATTN_WIDTH), lambda b, i: (b, 0)),
                  pl.BlockSpec((seq // tk, ATTN_WIDTH, tk), lambda b, i: (b, 0, 0)),
                  pl.BlockSpec(bias.shape, lambda b, i: (0, 0, 0, 0), pipeline_mode=pl.Buffered(1)),
                  pl.BlockSpec((1, V_DIM), lambda b, i: (0, 0))],
        out_specs=pl.BlockSpec((tq, ATTN_WIDTH), lambda b, i: (b * nq + i, 0)),
        out_shape=jax.ShapeDtypeStruct((batch * seq, ATTN_WIDTH), BF16),
        scratch_shapes=[pltpu.VMEM((N_HEADS, 2 * tq, V_DIM), BF16),
                        pltpu.VMEM((N_HEADS, 1, 2 * tq), F32),
                        pltpu.VMEM((N_HEADS, 1, 2 * tq), F32),
                        pltpu.VMEM((N_HEADS, V_DIM, 2 * tq), F32),
                        pltpu.VMEM((2, tk, 2 * tq), F32)],
        compiler_params=_cparams("parallel", "parallel"),
        name="prompt_attn",
    )(*lams, q, k, vt, bias, subln_g.reshape(1, V_DIM))


def _pool_branch(ext, u, pos0, wbd, scale):
    n, c = u.shape
    p = ext.shape[0] - n
    s2 = ext + pltpu.roll(ext, 1, 0)
    s4 = s2 + pltpu.roll(s2, 2, 0)
    s8 = s4 + pltpu.roll(s4, 4, 0)
    s16 = s8 + pltpu.roll(s8, 8, 0)
    lane = lax.broadcasted_iota(jnp.int32, (n, c), 1)
    grp = lane // POOL_GROUP_DIM
    win = jnp.where(grp == 0, s2[p:], jnp.where(grp == 1, s4[p:], jnp.where(grp == 2, s8[p:], s16[p:])))
    width = jnp.where(grp == 0, 2, jnp.where(grp == 1, 4, jnp.where(grp == 2, 8, 16)))
    pos = pos0 + lax.broadcasted_iota(jnp.int32, (n, c), 0)
    cnt = jnp.minimum(pos + 1, width).astype(F32)
    d = win / cnt - u
    return _dot(d.astype(BF16), wbd) * scale


def _prompt_side_kernel(u_ref, uprev_ref, qm_ref, mk_ref, mv_ref, wbd_ref, ps_ref, po_ref, om_ref, *, tm):
    i = pl.program_id(1)
    u = u_ref[...]
    prev = jnp.where(i > 0, uprev_ref[...], 0.0)
    ext = jnp.concatenate([prev, u], axis=0)
    po_ref[...] = _pool_branch(ext, u, i * tm, wbd_ref[...], ps_ref[...]).astype(po_ref.dtype)
    for h in range(N_HEADS):
        sl = slice(h * MEM_HEAD_DIM, (h + 1) * MEM_HEAD_DIM)
        p = _softmax_rows(_dot_nt(qm_ref[:, sl], mk_ref[:, sl].astype(BF16)))
        om_ref[:, sl] = _dot(p.astype(BF16), mv_ref[:, sl].astype(BF16)).astype(om_ref.dtype)


def _prompt_side(u, qm, mk, mv, wbd, pool_scale, batch, seq, tm):
    nblk = seq // tm
    halo = 16
    row = lambda b, i: (b * nblk + i, 0)
    return pl.pallas_call(
        functools.partial(_prompt_side_kernel, tm=tm),
        grid=(batch, nblk),
        in_specs=[pl.BlockSpec((tm, POOL_WIDTH), row),
                  pl.BlockSpec((halo, POOL_WIDTH),
                               lambda b, i: (jnp.maximum((b * nblk + i) * (tm // halo) - 1, 0), 0)),
                  pl.BlockSpec((tm, MEM_WIDTH), row),
                  pl.BlockSpec((N_MEM, MEM_WIDTH), lambda b, i: (b, 0)),
                  pl.BlockSpec((N_MEM, MEM_WIDTH), lambda b, i: (b, 0)),
                  pl.BlockSpec((POOL_WIDTH, POOL_WIDTH), lambda b, i: (0, 0)),
                  pl.BlockSpec((1, POOL_WIDTH), lambda b, i: (0, 0))],
        out_specs=[pl.BlockSpec((tm, POOL_WIDTH), row), pl.BlockSpec((tm, MEM_WIDTH), row)],
        out_shape=[jax.ShapeDtypeStruct((batch * seq, POOL_WIDTH), BF16),
                   jax.ShapeDtypeStruct((batch * seq, MEM_WIDTH), BF16)],
        compiler_params=_cparams("parallel", "parallel"),
        name="prompt_side",
    )(u, u, qm, mk, mv, wbd, pool_scale.reshape(1, POOL_WIDTH))


def _sample_step_kernel(pt_ref, lq1_ref, lk1_ref, lq2_ref, lk2_ref, q_ref, kn_ref, vn_ref, bias_ref, g_ref,
                        qm_ref, mkt_ref, mvt_ref, u_ref, st_ref, wbd_ref, ps_ref, *refs, n_pages, n_new, past):
    del pt_ref
    kp = refs[:n_pages]
    vp = refs[n_pages:2 * n_pages]
    o_ref, om_ref, po_ref, s_ref = refs[2 * n_pages:]
    lam = _lam_value(lq1_ref[...], lk1_ref[...], lq2_ref[...], lk2_ref[...])
    page = kp[0].shape[0] // N_HEADS
    n_tail = s_ref.shape[1] - n_pages * page
    pad = jnp.zeros((n_tail - n_new, V_DIM), F32)
    hr = 2 * n_new

    def head_rows(ref, h, n):
        return ref[pl.ds(h, n, stride=N_HEADS), :]

    def page_pair(refs_, c, h):
        return jnp.concatenate([head_rows(refs_[c], h, page), head_rows(refs_[c + 1], h, page)], axis=0).astype(BF16)

    for h in range(N_HEADS):
        q2 = _two_map_queries(q_ref[:, h * V_DIM:(h + 1) * V_DIM]).astype(BF16)
        for c in range(0, n_pages, 2):
            s_ref[h * hr:(h + 1) * hr, c * page:(c + 2) * page] = _dot_nt(q2, page_pair(kp, c, h))
        k_tail = jnp.concatenate([head_rows(kn_ref, h, n_new), pad], axis=0).astype(BF16)
        s_ref[h * hr:(h + 1) * hr, n_pages * page:] = _dot_nt(q2, k_tail)
    prob = _softmax_rows(s_ref[...] + bias_ref[...])
    for h in range(N_HEADS):
        a = prob[h * hr:h * hr + n_new] - lam * prob[h * hr + n_new:(h + 1) * hr]
        v_tail = jnp.concatenate([head_rows(vn_ref, h, n_new), pad], axis=0).astype(BF16)
        o_h = _dot(a[:, n_pages * page:].astype(BF16), v_tail)
        for c in range(0, n_pages, 2):
            o_h = o_h + _dot(a[:, c * page:(c + 2) * page].astype(BF16), page_pair(vp, c, h))
        o_ref[:, h * V_DIM:(h + 1) * V_DIM] = _subln(o_h, g_ref[...]).astype(o_ref.dtype)
    qm = qm_ref[...]
    head_of_lane = lax.broadcasted_iota(jnp.int32, qm.shape, 1) // MEM_HEAD_DIM
    qm4 = jnp.concatenate([jnp.where(head_of_lane == h, qm, 0.0) for h in range(N_HEADS)], axis=0).astype(BF16)
    pm = _softmax_rows(_dot(qm4, mkt_ref[...].astype(BF16)))
    om_all = _dot_nt(pm.astype(BF16), mvt_ref[...].astype(BF16))
    om = jnp.zeros(qm.shape, F32)
    for h in range(N_HEADS):
        om = jnp.where(head_of_lane == h, om_all[h * n_new:(h + 1) * n_new], om)
    om_ref[...] = om.astype(om_ref.dtype)
    u = u_ref[...]
    ext = jnp.concatenate([st_ref[...], u], axis=0)
    po_ref[...] = _pool_branch(ext, u, past, wbd_ref[...], ps_ref[...]).astype(po_ref.dtype)


def _sample_step(page_table, lams, q, k_new, v_new, bias, subln_g, qm, mem_kt, mem_vt, u, state16, wbd, pool_scale,
                 cache_k, cache_v):
    n_seq, n_pages = page_table.shape
    n_new = q.shape[1]
    rows = cache_k.shape[1]
    past = n_pages * rows // N_HEADS
    vec = pl.BlockSpec((1, HEAD_DIM), lambda s, pt: (0, 0))
    per_seq = lambda a: pl.BlockSpec((None,) + a.shape[1:], lambda s, pt: (s,) + (0,) * (a.ndim - 1))
    const = lambda a: pl.BlockSpec(a.shape, lambda s, pt: (0,) * a.ndim)
    page_spec = lambda p: pl.BlockSpec((None, rows, V_DIM), lambda s, pt, p=p: (pt[s, p], 0, 0))
    g = subln_g.reshape(1, V_DIM)
    ps = pool_scale.reshape(1, POOL_WIDTH)
    in_specs = [vec, vec, vec, vec, per_seq(q), per_seq(k_new), per_seq(v_new), const(bias), const(g),
                per_seq(qm), per_seq(mem_kt), per_seq(mem_vt), per_seq(u), per_seq(state16), const(wbd), const(ps)]
    in_specs += [page_spec(p) for p in range(n_pages)] * 2
    out3 = lambda width: jax.ShapeDtypeStruct((n_seq, n_new, width), F32)
    out_spec = lambda width: pl.BlockSpec((None, n_new, width), lambda s, pt: (s, 0, 0))
    grid_spec = pltpu.PrefetchScalarGridSpec(
        num_scalar_prefetch=1,
        grid=(n_seq,),
        in_specs=in_specs,
        out_specs=[out_spec(ATTN_WIDTH), out_spec(MEM_WIDTH), out_spec(POOL_WIDTH)],
        scratch_shapes=[pltpu.VMEM(bias.shape, F32)],
    )
    return pl.pallas_call(
        functools.partial(_sample_step_kernel, n_pages=n_pages, n_new=n_new, past=past),
        grid_spec=grid_spec,
        out_shape=[out3(ATTN_WIDTH), out3(MEM_WIDTH), out3(POOL_WIDTH)],
        compiler_params=_cparams("parallel"),
        name="sample_step",
    )(page_table, *lams, q, k_new, v_new, bias, g, qm, mem_kt, mem_vt, u, state16, wbd, ps,
      *([cache_k] * n_pages), *([cache_v] * n_pages))


def _merge_kernel(x_ref, o_ref, po_ref, om_ref, g_ref, wgl_ref, wa_ref, wp_ref, wm_ref, wout_ref, x1_ref):
    x = x_ref[...]
    h = _rmsnorm(x, g_ref[...]).astype(BF16)
    d = D_MODEL
    merged = jax.nn.sigmoid(_dot(h, wgl_ref[:, 0:d])) * _dot(o_ref[...].astype(BF16), wa_ref[...])
    merged = merged + jax.nn.sigmoid(_dot(h, wgl_ref[:, d:2 * d])) * _dot(po_ref[...].astype(BF16), wp_ref[...])
    merged = merged + jax.nn.sigmoid(_dot(h, wgl_ref[:, 2 * d:3 * d])) * _dot(om_ref[...].astype(BF16), wm_ref[...])
    x1_ref[...] = x + _dot(merged.astype(BF16), wout_ref[...])


def _merge(x2d, o, po, om, norm_g, wgl, wa, wp, wm, wout, tm):
    rows, d = x2d.shape
    row = lambda width: pl.BlockSpec((tm, width), lambda i: (i, 0))
    full = lambda a: pl.BlockSpec(a.shape, lambda i: (0, 0), pipeline_mode=pl.Buffered(1))
    g = norm_g.reshape(1, d)
    return pl.pallas_call(
        _merge_kernel,
        grid=(rows // tm,),
        in_specs=[row(d), row(ATTN_WIDTH), row(POOL_WIDTH), row(MEM_WIDTH), full(g), full(wgl), full(wa), full(wp),
                  full(wm), full(wout)],
        out_specs=row(d),
        out_shape=jax.ShapeDtypeStruct((rows, d), F32),
        compiler_params=_cparams("parallel"),
        name="merge",
    )(x2d, o, po, om, g, wgl, wa, wp, wm, wout)


def _conv_ffn_kernel(x_ref, g2_ref, wg_ref, wu_ref, cw_ref, cb_ref, wd_ref, gf_ref, *refs, tm, fc, seq_rows):
    if seq_rows:
        pre_ref, y_ref, gate_ref = refs
    else:
        y_ref, tail_ref, carry_ref = refs

        @pl.when(pl.program_id(1) == 0)
        def _():
            carry_ref[...] = jnp.zeros_like(carry_ref)

    x = x_ref[...]
    h = _rmsnorm(x, g2_ref[...]).astype(BF16)
    acc = jnp.zeros((tm, D_MODEL), F32)

    def project(c0):
        return _dot(h, wg_ref[:, c0:c0 + fc]), _dot(h, wu_ref[:, c0:c0 + fc])

    pending = project(0)
    for c0 in range(0, D_FF, fc):
        cs = slice(c0, c0 + fc)
        gate, up = pending
        if c0 + fc < D_FF:
            pending = project(c0 + fc)
        if seq_rows:
            pre = pre_ref[:, cs]
            t = lax.broadcasted_iota(jnp.int32, (tm, fc), 0) % seq_rows
            back1 = jnp.where(t >= 1, pltpu.roll(gate, 1, 0), pltpu.roll(pre, tm - 1, 0))
            back2 = jnp.where(t >= 2, pltpu.roll(gate, 2, 0), pre)
            gate_ref[:, cs] = gate
        else:
            ext = jnp.concatenate([carry_ref[:, cs], gate], axis=0)
            back1 = pltpu.roll(ext, 1, 0)[8:]
            back2 = pltpu.roll(ext, 2, 0)[8:]
            carry_ref[:, cs] = gate[tm - 8:]
        gc = cb_ref[:, cs] + cw_ref[0:1, cs] * back2
        gc = gc + cw_ref[1:2, cs] * back1
        gc = gc + cw_ref[2:3, cs] * gate
        act = (jax.nn.gelu(gc) * up).astype(BF16)
        acc = acc + _dot(act, wd_ref[cs, :])
    if not seq_rows:
        tail_ref[...] = carry_ref[...]
    y_ref[...] = _rmsnorm(x + acc, gf_ref[...])


def _conv_ffn(x1, norm_g, wg, wu, conv_w, conv_b, wd, final_g, tm, batch=None, seq=None, prefix=None):
    rows, d = x1.shape
    g2 = norm_g.reshape(1, d)
    gf = final_g.reshape(1, d)
    cb = conv_b.reshape(1, D_FF)
    if prefix is None:
        nblk = seq // tm
        row = lambda width: pl.BlockSpec((tm, width), lambda b, i: (b * nblk + i, 0))
        full = lambda a: pl.BlockSpec(a.shape, lambda b, i: (0, 0), pipeline_mode=pl.Buffered(1))
        return pl.pallas_call(
            functools.partial(_conv_ffn_kernel, tm=tm, fc=FFN_CHUNK, seq_rows=0),
            grid=(batch, nblk),
            in_specs=[row(d), full(g2), full(wg), full(wu), full(conv_w), full(cb), full(wd), full(gf)],
            out_specs=[row(d), pl.BlockSpec((None, 8, D_FF), lambda b, i: (b, 0, 0))],
            out_shape=[jax.ShapeDtypeStruct((rows, d), F32), jax.ShapeDtypeStruct((batch, 8, D_FF), F32)],
            scratch_shapes=[pltpu.VMEM((8, D_FF), F32)],
            compiler_params=_cparams("parallel", "arbitrary"),
            name="conv_ffn_prompt",
        )(x1, g2, wg, wu, conv_w, cb, wd, gf)
    row = lambda width: pl.BlockSpec((tm, width), lambda i: (i, 0))
    full = lambda a: pl.BlockSpec(a.shape, lambda i: (0, 0), pipeline_mode=pl.Buffered(1))
    return pl.pallas_call(
        functools.partial(_conv_ffn_kernel, tm=tm, fc=FFN_CHUNK, seq_rows=8),
        grid=(rows // tm,),
        in_specs=[row(d), full(g2), full(wg), full(wu), full(conv_w), full(cb), full(wd), full(gf), row(D_FF)],
        out_specs=[row(d), row(D_FF)],
        out_shape=[jax.ShapeDtypeStruct((rows, d), F32), jax.ShapeDtypeStruct((rows, D_FF), F32)],
        compiler_params=_cparams("parallel"),
        name="conv_ffn_sample",
    )(x1, g2, wg, wu, conv_w, cb, wd, gf, prefix)


def kernel(x_prompt, x_sample, mem_prompt, cache_k, cache_v, page_table, state_pool, state_ffn_conv, cache_mem_k, cache_mem_v, norm1_g, w_in, lam_q1, lam_k1, lam_q2, lam_k2, subln_g, w_pool_grp, pool_scale, w_br_attn, w_br_pool, w_br_mem, mem_norm_g, w_mem_kv, w_out, norm2_g, w_ffn_gate, w_ffn_up, ffn_conv_w, ffn_conv_b, w_ffn_down, rel_bias, final_norm_g):
    depth = w_in.shape[0]
    assert depth == 1, "single-layer step only"
    B, S, D = x_prompt.shape
    DB, DS, _ = x_sample.shape
    n_phys, page = cache_k.shape[1], cache_k.shape[2]
    n_pages = page_table.shape[1]
    past = n_pages * page
    assert DS == 8 and S % ROW_BLOCK == 0 and (DB * DS) % ROW_BLOCK == 0
    assert ATTN_TK % ATTN_TQ == 0 and ROW_BLOCK % ATTN_TK == 0 and S % ATTN_TK == 0

    w_qkv = w_in[0, :, :2048].astype(BF16)
    w_gl = w_in[0, :, 2048:].astype(BF16)
    wa, wp, wm = w_br_attn[0].astype(BF16), w_br_pool[0].astype(BF16), w_br_mem[0].astype(BF16)
    wout = w_out[0].astype(BF16)
    wg, wu, wd = w_ffn_gate[0].astype(BF16), w_ffn_up[0].astype(BF16), w_ffn_down[0].astype(BF16)
    w_mem = w_mem_kv[0].astype(BF16)
    wbd = jnp.zeros((POOL_WIDTH, POOL_WIDTH), F32)
    for gi in range(len(POOL_WINDOWS)):
        sl = slice(gi * POOL_GROUP_DIM, (gi + 1) * POOL_GROUP_DIM)
        wbd = wbd.at[sl, sl].set(w_pool_grp[0, gi])
    wbd = wbd.astype(BF16)
    lams = tuple(a[0].reshape(1, HEAD_DIM) for a in (lam_q1, lam_k1, lam_q2, lam_k2))

    prompt_bias, sample_bias = _bias_tables(rel_bias, ATTN_TQ, ATTN_TK, past, DS)

    xp = x_prompt.reshape(B * S, D)
    groups = ((0, 512, ((QK_SCALE * LOG2E, 0),)), (512, 512, ((1.0, HEAD_ROWS), (1.0, 0))),
              (1024, 512, ((1.0, HEAD_ROWS), (1.0, ATTN_TK))), (1536, 256, ((1.0, 0),)), (1792, 256, ((MEM_SCALE, 0),)))
    qp, kp, kp_b, vp, vp_t, up, qmp = _norm_proj(xp, norm1_g[0], w_qkv, groups,
                                                 (BF16, F32, BF16, F32, BF16, F32, BF16), ROW_BLOCK)
    mk_p, mv_p = _norm_proj(mem_prompt.reshape(B * N_MEM, D), mem_norm_g[0], w_mem,
                            ((0, MEM_WIDTH, ((1.0, 0),)), (MEM_WIDTH, MEM_WIDTH, ((1.0, 0),))), (F32, F32), ROW_BLOCK)
    op = _prompt_attn(lams, qp, kp_b, vp_t, prompt_bias, subln_g[0], B, S, ATTN_TQ, ATTN_TK)
    pop, omp = _prompt_side(up, qmp, mk_p, mv_p, wbd, pool_scale[0], B, S, ROW_BLOCK)
    x1p = _merge(xp, op, pop, omp, norm1_g[0], w_gl, wa, wp, wm, wout, ROW_BLOCK)
    yp, tail_p = _conv_ffn(x1p, norm2_g[0], wg, wu, ffn_conv_w[0], ffn_conv_b[0], wd, final_norm_g, ROW_BLOCK,
                           batch=B, seq=S)

    xs = x_sample.reshape(DB * DS, D)
    groups_s = ((0, 512, ((QK_SCALE, 0),)), (512, 512, ((1.0, HEAD_ROWS),)), (1024, 512, ((1.0, HEAD_ROWS),)),
                (1536, 256, ((1.0, 0),)), (1792, 256, ((MEM_SCALE, 0),)))
    qs, ks, vs, us, qms = _norm_proj(xs, norm1_g[0], w_qkv, groups_s, (F32,) * 5, ROW_BLOCK)
    seq3 = lambda a: a.reshape(DB, DS, a.shape[-1])
    head_rows = lambda a: a.reshape(DB, DS * N_HEADS, V_DIM)
    state16 = jnp.pad(state_pool[0], ((0, 0), (16 - POOL_STATE, 0), (0, 0)))
    mem_t = lambda a: jnp.transpose(a[0], (0, 2, 3, 1)).reshape(DB, MEM_WIDTH, N_MEM)
    pages = lambda a: a[0].reshape(n_phys, page * N_HEADS, V_DIM)
    os_, oms, pos_ = _sample_step(page_table, lams, seq3(qs), head_rows(ks), head_rows(vs), sample_bias, subln_g[0],
                                  seq3(qms), mem_t(cache_mem_k), mem_t(cache_mem_v), seq3(us), state16, wbd,
                                  pool_scale[0], pages(cache_k), pages(cache_v))
    flat = lambda a: a.reshape(DB * DS, a.shape[-1])
    x1s = _merge(xs, flat(os_), flat(pos_), flat(oms), norm1_g[0], w_gl, wa, wp, wm, wout, ROW_BLOCK)
    conv_prefix = jnp.pad(state_ffn_conv[0], ((0, 0), (0, DS - 2), (0, 0))).reshape(DB * DS, D_FF)
    ys, gate_s = _conv_ffn(x1s, norm2_g[0], wg, wu, ffn_conv_w[0], ffn_conv_b[0], wd, final_norm_g, ROW_BLOCK,
                           prefix=conv_prefix)

    heads = lambda a, n: a.reshape(1, n, -1, N_HEADS, V_DIM)
    new_pool_p = up.reshape(B, S, POOL_WIDTH)[:, S - POOL_STATE:][None]
    new_pool_s = jnp.concatenate([state_pool[0][:, DS:], seq3(us)], axis=1)[None]
    new_conv_p = tail_p[:, 6:8][None]
    new_conv_s = gate_s.reshape(DB, DS, D_FF)[:, DS - 2:][None]
    mem_heads = lambda a: a.reshape(1, B, N_MEM, N_HEADS, MEM_HEAD_DIM)
    return (yp.reshape(B, S, D), ys.reshape(DB, DS, D), heads(kp, B), heads(vp, B), heads(ks, DB), heads(vs, DB),
            new_pool_p, new_pool_s, new_conv_p, new_conv_s, mem_heads(mk_p), mem_heads(mv_p))
```

```python
import functools
import math

import jax
import jax.numpy as jnp
from jax import lax
from jax.experimental import pallas as pl
from jax.experimental.pallas import tpu as pltpu

F32 = jnp.float32
BF16 = jnp.bfloat16

D_MODEL = 1024
N_HEADS = 4
HEAD_DIM = 64
V_DIM = 2 * HEAD_DIM
ATTN_WIDTH = N_HEADS * V_DIM
POOL_WINDOWS = (2, 4, 8, 16)
POOL_GROUP_DIM = 64
POOL_WIDTH = 256
POOL_STATE = 15
MEM_HEAD_DIM = 64
MEM_WIDTH = 256
N_MEM = 256
D_FF = 2816
N_BUCKETS = 32
MAX_DISTANCE = 128
EPS = 1e-6
NEG_INF = -1e30
QK_SCALE = HEAD_DIM ** -0.5
MEM_SCALE = MEM_HEAD_DIM ** -0.5
LAM_INIT = 0.8 - 0.6 * math.exp(-0.3 * 0)
SUBLN_SCALE = 1.0 - LAM_INIT
LOG2E = math.log2(math.e)

ATTN_TQ = 256
ATTN_TK = 256
ROW_BLOCK = 512
FFN_CHUNK = 256
SAMPLE_KPAD = 128
VMEM_LIMIT = 56 * 1024 * 1024
HEAD_ROWS = -1


def _cparams(*sem):
    return pltpu.CompilerParams(dimension_semantics=sem, vmem_limit_bytes=VMEM_LIMIT)


def _rmsnorm(x, g):
    return x * lax.rsqrt(jnp.mean(x * x, axis=-1, keepdims=True) + EPS) * g


def _dot(a, b):
    return jnp.dot(a, b, preferred_element_type=F32)


def _dot_nt(a, b):
    return lax.dot_general(a, b, (((1,), (1,)), ((), ())), preferred_element_type=F32)


def _softmax_rows(s):
    m = jnp.max(s, axis=-1, keepdims=True)
    p = jnp.exp(s - m)
    return p / jnp.sum(p, axis=-1, keepdims=True)


def _lam_value(lq1, lk1, lq2, lk2):
    a = jnp.sum(lq1 * lk1, axis=-1, keepdims=True)
    b = jnp.sum(lq2 * lk2, axis=-1, keepdims=True)
    return jnp.exp(a) - jnp.exp(b) + LAM_INIT


def _norm_proj_kernel(x_ref, g_ref, w_ref, *out_refs, groups):
    h = _rmsnorm(x_ref[...], g_ref[...]).astype(BF16)
    k = 0
    for start, size, outs in groups:
        y = _dot(h, w_ref[:, start:start + size])
        for scale, key_block in outs:
            o_ref = out_refs[k]
            k += 1
            ys = y if scale == 1.0 else y * scale
            if key_block == HEAD_ROWS:
                n = ys.shape[0]
                for hd in range(N_HEADS):
                    o_ref[pl.ds(hd, n, stride=N_HEADS), :] = ys[:, hd * V_DIM:(hd + 1) * V_DIM].astype(o_ref.dtype)
            elif key_block:
                yt = ys.T
                for b in range(o_ref.shape[0]):
                    o_ref[b] = yt[:, b * key_block:(b + 1) * key_block].astype(o_ref.dtype)
            else:
                o_ref[...] = ys.astype(o_ref.dtype)


def _norm_proj(x2d, g, w_bf16, groups, out_dtypes, tm):
    rows, d = x2d.shape
    n_cols = w_bf16.shape[1]
    out_shape, out_specs = [], []
    flat_outs = [(size, kb) for _, size, outs in groups for _, kb in outs]
    for (size, kb), dt in zip(flat_outs, out_dtypes):
        if kb == HEAD_ROWS:
            out_shape.append(jax.ShapeDtypeStruct((rows * N_HEADS, V_DIM), dt))
            out_specs.append(pl.BlockSpec((tm * N_HEADS, V_DIM), lambda i: (i, 0)))
        elif kb:
            out_shape.append(jax.ShapeDtypeStruct((rows // kb, size, kb), dt))
            out_specs.append(pl.BlockSpec((tm // kb, size, kb), lambda i: (i, 0, 0)))
        else:
            out_shape.append(jax.ShapeDtypeStruct((rows, size), dt))
            out_specs.append(pl.BlockSpec((tm, size), lambda i: (i, 0)))
    return pl.pallas_call(
        functools.partial(_norm_proj_kernel, groups=groups),
        grid=(rows // tm,),
        in_specs=[pl.BlockSpec((tm, d), lambda i: (i, 0)),
                  pl.BlockSpec((1, d), lambda i: (0, 0)),
                  pl.BlockSpec((d, n_cols), lambda i: (0, 0), pipeline_mode=pl.Buffered(1))],
        out_specs=out_specs,
        out_shape=out_shape,
        compiler_params=_cparams("parallel"),
        name="norm_proj",
    )(x2d, g.reshape(1, d), w_bf16)


def _rel_bucket(rel):
    n = jnp.maximum(rel, 0)
    max_exact = N_BUCKETS // 2
    nf = jnp.maximum(n, 1).astype(F32)
    large = max_exact + jnp.floor(jnp.log(nf / max_exact) / math.log(MAX_DISTANCE / max_exact)
                                  * (N_BUCKETS - max_exact)).astype(jnp.int32)
    large = jnp.minimum(large, N_BUCKETS - 1)
    return jnp.where(n < max_exact, n, large)


def _bias_from_rel(rel, rb_ref, head, visible):
    bucket = _rel_bucket(rel)
    acc = jnp.zeros(rel.shape, F32)
    for b in range(N_BUCKETS):
        acc = jnp.where(bucket == b, rb_ref[b, head], acc)
    return jnp.where(visible, acc, NEG_INF)


def _bias_kernel(rb_ref, pb_ref, sb_ref, *, tq, tk, past, n_new):
    n_kinds = pb_ref.shape[1]
    key = lax.broadcasted_iota(jnp.int32, (tk, 2 * tq), 0)
    qry = lax.broadcasted_iota(jnp.int32, (tk, 2 * tq), 1) % tq
    for kind in range(n_kinds):
        rel = kind * tq + qry - key
        for h in range(N_HEADS):
            pb_ref[h, kind] = _bias_from_rel(rel, rb_ref, h, rel >= 0) * LOG2E
    rows_per_head = 2 * n_new
    n_cols = sb_ref.shape[1]
    qi = lax.broadcasted_iota(jnp.int32, (rows_per_head, n_cols), 0) % n_new
    key = lax.broadcasted_iota(jnp.int32, (rows_per_head, n_cols), 1)
    rel = past + qi - key
    for h in range(N_HEADS):
        sb_ref[h * rows_per_head:(h + 1) * rows_per_head, :] = _bias_from_rel(rel, rb_ref, h, rel >= 0)


def _bias_tables(rel_bias, tq, tk, past, n_new):
    n_kinds = -(-(tk + MAX_DISTANCE - 1) // tq) + 1
    n_cols = past + SAMPLE_KPAD
    return pl.pallas_call(
        functools.partial(_bias_kernel, tq=tq, tk=tk, past=past, n_new=n_new),
        in_specs=[pl.BlockSpec(memory_space=pltpu.SMEM)],
        out_specs=[pl.BlockSpec(memory_space=pltpu.VMEM), pl.BlockSpec(memory_space=pltpu.VMEM)],
        out_shape=[jax.ShapeDtypeStruct((N_HEADS, n_kinds, tk, 2 * tq), F32),
                   jax.ShapeDtypeStruct((N_HEADS * 2 * n_new, n_cols), F32)],
        compiler_params=pltpu.CompilerParams(vmem_limit_bytes=VMEM_LIMIT),
        name="bias_tables",
    )(rel_bias)


def _subln(o, g):
    return _rmsnorm(o, g) * SUBLN_SCALE


def _two_map_queries(q_h):
    lane = lax.broadcasted_iota(jnp.int32, q_h.shape, 1)
    zero = jnp.zeros_like(q_h)
    return jnp.concatenate([jnp.where(lane < HEAD_DIM, q_h, zero), jnp.where(lane >= HEAD_DIM, q_h, zero)], axis=0)


def _prompt_attn_kernel(lq1_ref, lk1_ref, lq2_ref, lk2_ref, q_ref, k_ref, vt_ref, bias_ref, g_ref, o_ref,
                        q2_ref, m_ref, l_ref, acc_ref, s0_ref, *, tq, tk):
    i = pl.program_id(1)
    lam = _lam_value(lq1_ref[...], lk1_ref[...], lq2_ref[...], lk2_ref[...])
    n_kinds = bias_ref.shape[1]
    for h in range(N_HEADS):
        q2_ref[h] = _two_map_queries(q_ref[:, h * V_DIM:(h + 1) * V_DIM])
    m_ref[...] = jnp.full(m_ref.shape, NEG_INF, F32)
    l_ref[...] = jnp.zeros(l_ref.shape, F32)
    acc_ref[...] = jnp.zeros(acc_ref.shape, F32)

    last = (i * tq + tq - 1) // tk

    def scores(j, h):
        r0 = pl.multiple_of(j * tk, tk)
        kind = jnp.minimum((i * tq - j * tk) // tq, n_kinds - 1)
        hs = slice(h * V_DIM, (h + 1) * V_DIM)
        return _dot_nt(k_ref[pl.ds(r0, tk), hs], q2_ref[h]) + bias_ref[h, kind]

    ahead = 2
    for h in range(ahead):
        s0_ref[h] = scores(0, h)

    def body(j, carry):
        pending = [s0_ref[h] for h in range(ahead)]
        j_next = jnp.minimum(j + 1, last)
        for h in range(N_HEADS):
            hs = slice(h * V_DIM, (h + 1) * V_DIM)
            s = pending.pop(0)
            pending.append(scores(j, h + ahead) if h + ahead < N_HEADS else scores(j_next, h + ahead - N_HEADS))
            m_old = m_ref[h]
            m_new = jnp.maximum(m_old, jnp.max(s, axis=0, keepdims=True))
            alpha = jnp.exp2(m_old - m_new)
            p = jnp.exp2(s - m_new)
            l_ref[h] = alpha * l_ref[h] + jnp.sum(p, axis=0, keepdims=True)
            m_ref[h] = m_new
            acc_ref[h] = alpha * acc_ref[h] + _dot(vt_ref[j, hs, :], p.astype(BF16))
        for h in range(ahead):
            s0_ref[h] = pending[h]
        return carry

    lax.fori_loop(0, last + 1, body, 0)
    for h in range(N_HEADS):
        out_t = acc_ref[h] / l_ref[h]
        o_t = out_t[:, :tq] - lam * out_t[:, tq:]
        o_ref[:, h * V_DIM:(h + 1) * V_DIM] = _subln(o_t.T, g_ref[...]).astype(o_ref.dtype)


def _prompt_attn(lams, q, k, vt, bias, subln_g, batch, seq, tq, tk):
    vec = pl.BlockSpec((1, HEAD_DIM), lambda b, i: (0, 0))
    nq = seq // tq
    return pl.pallas_call(
        functools.partial(_prompt_attn_kernel, tq=tq, tk=tk),
        grid=(batch, nq),
        in_specs=[vec, vec, vec, vec,
                  pl.BlockSpec((tq, ATTN_WIDTH), lambda b, i: (b * nq + i, 0)),
                  pl.BlockSpec((seq, ATTN_WIDTH), lambda b, i: (b, 0)),
                  pl.BlockSpec((seq // tk, ATTN_WIDTH, tk), lambda b, i: (b, 0, 0)),
                  pl.BlockSpec(bias.shape, lambda b, i: (0, 0, 0, 0), pipeline_mode=pl.Buffered(1)),
                  pl.BlockSpec((1, V_DIM), lambda b, i: (0, 0))],
        out_specs=pl.BlockSpec((tq, ATTN_WIDTH), lambda b, i: (b * nq + i, 0)),
        out_shape=jax.ShapeDtypeStruct((batch * seq, ATTN_WIDTH), BF16),
        scratch_shapes=[pltpu.VMEM((N_HEADS, 2 * tq, V_DIM), BF16),
                        pltpu.VMEM((N_HEADS, 1, 2 * tq), F32),
                        pltpu.VMEM((N_HEADS, 1, 2 * tq), F32),
                        pltpu.VMEM((N_HEADS, V_DIM, 2 * tq), F32),
                        pltpu.VMEM((2, tk, 2 * tq), F32)],
        compiler_params=_cparams("parallel", "parallel"),
        name="prompt_attn",
    )(*lams, q, k, vt, bias, subln_g.reshape(1, V_DIM))


def _pool_branch(ext, u, pos0, wbd, scale):
    n, c = u.shape
    p = ext.shape[0] - n
    s2 = ext + pltpu.roll(ext, 1, 0)
    s4 = s2 + pltpu.roll(s2, 2, 0)
    s8 = s4 + pltpu.roll(s4, 4, 0)
    s16 = s8 + pltpu.roll(s8, 8, 0)
    lane = lax.broadcasted_iota(jnp.int32, (n, c), 1)
    grp = lane // POOL_GROUP_DIM
    win = jnp.where(grp == 0, s2[p:], jnp.where(grp == 1, s4[p:], jnp.where(grp == 2, s8[p:], s16[p:])))
    width = jnp.where(grp == 0, 2, jnp.where(grp == 1, 4, jnp.where(grp == 2, 8, 16)))
    pos = pos0 + lax.broadcasted_iota(jnp.int32, (n, c), 0)
    cnt = jnp.minimum(pos + 1, width).astype(F32)
    d = win / cnt - u
    return _dot(d.astype(BF16), wbd) * scale


def _prompt_side_kernel(u_ref, uprev_ref, qm_ref, mk_ref, mv_ref, wbd_ref, ps_ref, po_ref, om_ref, *, tm):
    i = pl.program_id(1)
    u = u_ref[...]
    prev = jnp.where(i > 0, uprev_ref[...], 0.0)
    ext = jnp.concatenate([prev, u], axis=0)
    po_ref[...] = _pool_branch(ext, u, i * tm, wbd_ref[...], ps_ref[...]).astype(po_ref.dtype)
    for h in range(N_HEADS):
        sl = slice(h * MEM_HEAD_DIM, (h + 1) * MEM_HEAD_DIM)
        p = _softmax_rows(_dot_nt(qm_ref[:, sl], mk_ref[:, sl].astype(BF16)))
        om_ref[:, sl] = _dot(p.astype(BF16), mv_ref[:, sl].astype(BF16)).astype(om_ref.dtype)


def _prompt_side(u, qm, mk, mv, wbd, pool_scale, batch, seq, tm):
    nblk = seq // tm
    halo = 16
    row = lambda b, i: (b * nblk + i, 0)
    return pl.pallas_call(
        functools.partial(_prompt_side_kernel, tm=tm),
        grid=(batch, nblk),
        in_specs=[pl.BlockSpec((tm, POOL_WIDTH), row),
                  pl.BlockSpec((halo, POOL_WIDTH),
                               lambda b, i: (jnp.maximum((b * nblk + i) * (tm // halo) - 1, 0), 0)),
                  pl.BlockSpec((tm, MEM_WIDTH), row),
                  pl.BlockSpec((N_MEM, MEM_WIDTH), lambda b, i: (b, 0)),
                  pl.BlockSpec((N_MEM, MEM_WIDTH), lambda b, i: (b, 0)),
                  pl.BlockSpec((POOL_WIDTH, POOL_WIDTH), lambda b, i: (0, 0)),
                  pl.BlockSpec((1, POOL_WIDTH), lambda b, i: (0, 0))],
        out_specs=[pl.BlockSpec((tm, POOL_WIDTH), row), pl.BlockSpec((tm, MEM_WIDTH), row)],
        out_shape=[jax.ShapeDtypeStruct((batch * seq, POOL_WIDTH), BF16),
                   jax.ShapeDtypeStruct((batch * seq, MEM_WIDTH), BF16)],
        compiler_params=_cparams("parallel", "parallel"),
        name="prompt_side",
    )(u, u, qm, mk, mv, wbd, pool_scale.reshape(1, POOL_WIDTH))


def _sample_step_kernel(pt_ref, lq1_ref, lk1_ref, lq2_ref, lk2_ref, q_ref, kn_ref, vn_ref, bias_ref, g_ref,
                        qm_ref, mkt_ref, mvt_ref, u_ref, st_ref, wbd_ref, ps_ref, *refs, n_pages, n_new, past):
    del pt_ref
    kp = refs[:n_pages]
    vp = refs[n_pages:2 * n_pages]
    o_ref, om_ref, po_ref, s_ref = refs[2 * n_pages:]
    lam = _lam_value(lq1_ref[...], lk1_ref[...], lq2_ref[...], lk2_ref[...])
    page = kp[0].shape[0] // N_HEADS
    n_tail = s_ref.shape[1] - n_pages * page
    pad = jnp.zeros((n_tail - n_new, V_DIM), F32)
    hr = 2 * n_new

    def head_rows(ref, h, n):
        return ref[pl.ds(h, n, stride=N_HEADS), :]

    def page_pair(refs_, c, h):
        return jnp.concatenate([head_rows(refs_[c], h, page), head_rows(refs_[c + 1], h, page)], axis=0).astype(BF16)

    for h in range(N_HEADS):
        q2 = _two_map_queries(q_ref[:, h * V_DIM:(h + 1) * V_DIM]).astype(BF16)
        for c in range(0, n_pages, 2):
            s_ref[h * hr:(h + 1) * hr, c * page:(c + 2) * page] = _dot_nt(q2, page_pair(kp, c, h))
        k_tail = jnp.concatenate([head_rows(kn_ref, h, n_new), pad], axis=0).astype(BF16)
        s_ref[h * hr:(h + 1) * hr, n_pages * page:] = _dot_nt(q2, k_tail)
    prob = _softmax_rows(s_ref[...] + bias_ref[...])
    for h in range(N_HEADS):
        a = prob[h * hr:h * hr + n_new] - lam * prob[h * hr + n_new:(h + 1) * hr]
        v_tail = jnp.concatenate([head_rows(vn_ref, h, n_new), pad], axis=0).astype(BF16)
        o_h = _dot(a[:, n_pages * page:].astype(BF16), v_tail)
        for c in range(0, n_pages, 2):
            o_h = o_h + _dot(a[:, c * page:(c + 2) * page].astype(BF16), page_pair(vp, c, h))
        o_ref[:, h * V_DIM:(h + 1) * V_DIM] = _subln(o_h, g_ref[...]).astype(o_ref.dtype)
    qm = qm_ref[...]
    head_of_lane = lax.broadcasted_iota(jnp.int32, qm.shape, 1) // MEM_HEAD_DIM
    qm4 = jnp.concatenate([jnp.where(head_of_lane == h, qm, 0.0) for h in range(N_HEADS)], axis=0).astype(BF16)
    pm = _softmax_rows(_dot(qm4, mkt_ref[...].astype(BF16)))
    om_all = _dot_nt(pm.astype(BF16), mvt_ref[...].astype(BF16))
    om = jnp.zeros(qm.shape, F32)
    for h in range(N_HEADS):
        om = jnp.where(head_of_lane == h, om_all[h * n_new:(h + 1) * n_new], om)
    om_ref[...] = om.astype(om_ref.dtype)
    u = u_ref[...]
    ext = jnp.concatenate([st_ref[...], u], axis=0)
    po_ref[...] = _pool_branch(ext, u, past, wbd_ref[...], ps_ref[...]).astype(po_ref.dtype)


def _sample_step(page_table, lams, q, k_new, v_new, bias, subln_g, qm, mem_kt, mem_vt, u, state16, wbd, pool_scale,
                 cache_k, cache_v):
    n_seq, n_pages = page_table.shape
    n_new = q.shape[1]
    rows = cache_k.shape[1]
    past = n_pages * rows // N_HEADS
    vec = pl.BlockSpec((1, HEAD_DIM), lambda s, pt: (0, 0))
    per_seq = lambda a: pl.BlockSpec((None,) + a.shape[1:], lambda s, pt: (s,) + (0,) * (a.ndim - 1))
    const = lambda a: pl.BlockSpec(a.shape, lambda s, pt: (0,) * a.ndim)
    page_spec = lambda p: pl.BlockSpec((None, rows, V_DIM), lambda s, pt, p=p: (pt[s, p], 0, 0))
    g = subln_g.reshape(1, V_DIM)
    ps = pool_scale.reshape(1, POOL_WIDTH)
    in_specs = [vec, vec, vec, vec, per_seq(q), per_seq(k_new), per_seq(v_new), const(bias), const(g),
                per_seq(qm), per_seq(mem_kt), per_seq(mem_vt), per_seq(u), per_seq(state16), const(wbd), const(ps)]
    in_specs += [page_spec(p) for p in range(n_pages)] * 2
    out3 = lambda width: jax.ShapeDtypeStruct((n_seq, n_new, width), F32)
    out_spec = lambda width: pl.BlockSpec((None, n_new, width), lambda s, pt: (s, 0, 0))
    grid_spec = pltpu.PrefetchScalarGridSpec(
        num_scalar_prefetch=1,
        grid=(n_seq,),
        in_specs=in_specs,
        out_specs=[out_spec(ATTN_WIDTH), out_spec(MEM_WIDTH), out_spec(POOL_WIDTH)],
        scratch_shapes=[pltpu.VMEM(bias.shape, F32)],
    )
    return pl.pallas_call(
        functools.partial(_sample_step_kernel, n_pages=n_pages, n_new=n_new, past=past),
        grid_spec=grid_spec,
        out_shape=[out3(ATTN_WIDTH), out3(MEM_WIDTH), out3(POOL_WIDTH)],
        compiler_params=_cparams("parallel"),
        name="sample_step",
    )(page_table, *lams, q, k_new, v_new, bias, g, qm, mem_kt, mem_vt, u, state16, wbd, ps,
      *([cache_k] * n_pages), *([cache_v] * n_pages))


def _merge_kernel(x_ref, o_ref, po_ref, om_ref, g_ref, wgl_ref, wa_ref, wp_ref, wm_ref, wout_ref, x1_ref):
    x = x_ref[...]
    h = _rmsnorm(x, g_ref[...]).astype(BF16)
    d = D_MODEL
    merged = jax.nn.sigmoid(_dot(h, wgl_ref[:, 0:d])) * _dot(o_ref[...].astype(BF16), wa_ref[...])
    merged = merged + jax.nn.sigmoid(_dot(h, wgl_ref[:, d:2 * d])) * _dot(po_ref[...].astype(BF16), wp_ref[...])
    merged = merged + jax.nn.sigmoid(_dot(h, wgl_ref[:, 2 * d:3 * d])) * _dot(om_ref[...].astype(BF16), wm_ref[...])
    x1_ref[...] = x + _dot(merged.astype(BF16), wout_ref[...])


def _merge(x2d, o, po, om, norm_g, wgl, wa, wp, wm, wout, tm):
    rows, d = x2d.shape
    row = lambda width: pl.BlockSpec((tm, width), lambda i: (i, 0))
    full = lambda a: pl.BlockSpec(a.shape, lambda i: (0, 0), pipeline_mode=pl.Buffered(1))
    g = norm_g.reshape(1, d)
    return pl.pallas_call(
        _merge_kernel,
        grid=(rows // tm,),
        in_specs=[row(d), row(ATTN_WIDTH), row(POOL_WIDTH), row(MEM_WIDTH), full(g), full(wgl), full(wa), full(wp),
                  full(wm), full(wout)],
        out_specs=row(d),
        out_shape=jax.ShapeDtypeStruct((rows, d), F32),
        compiler_params=_cparams("parallel"),
        name="merge",
    )(x2d, o, po, om, g, wgl, wa, wp, wm, wout)


def _conv_ffn_kernel(x_ref, g2_ref, wg_ref, wu_ref, cw_ref, cb_ref, wd_ref, gf_ref, *refs, tm, fc, seq_rows):
    if seq_rows:
        pre_ref, y_ref, gate_ref, act_ref = refs
    else:
        y_ref, tail_ref, carry_ref, act_ref = refs

        @pl.when(pl.program_id(1) == 0)
        def _():
            carry_ref[...] = jnp.zeros_like(carry_ref)

    x = x_ref[...]
    h = _rmsnorm(x, g2_ref[...]).astype(BF16)
    for c0 in range(0, D_FF, fc):
        cs = slice(c0, c0 + fc)
        gate = _dot(h, wg_ref[:, cs])
        up = _dot(h, wu_ref[:, cs])
        if seq_rows:
            pre = pre_ref[:, cs]
            t = lax.broadcasted_iota(jnp.int32, (tm, fc), 0) % seq_rows
            back1 = jnp.where(t >= 1, pltpu.roll(gate, 1, 0), pltpu.roll(pre, tm - 1, 0))
            back2 = jnp.where(t >= 2, pltpu.roll(gate, 2, 0), pre)
            gate_ref[:, cs] = gate
        else:
            ext = jnp.concatenate([carry_ref[:, cs], gate], axis=0)
            back1 = pltpu.roll(ext, 1, 0)[8:]
            back2 = pltpu.roll(ext, 2, 0)[8:]
            carry_ref[:, cs] = gate[tm - 8:]
        gc = cb_ref[:, cs] + cw_ref[0:1, cs] * back2
        gc = gc + cw_ref[1:2, cs] * back1
        gc = gc + cw_ref[2:3, cs] * gate
        act_ref[:, cs] = (jax.nn.gelu(gc) * up).astype(BF16)
    if not seq_rows:
        tail_ref[...] = carry_ref[...]
    y_ref[...] = _rmsnorm(x + _dot(act_ref[...], wd_ref[...]), gf_ref[...])


def _conv_ffn(x1, norm_g, wg, wu, conv_w, conv_b, wd, final_g, tm, batch=None, seq=None, prefix=None):
    rows, d = x1.shape
    g2 = norm_g.reshape(1, d)
    gf = final_g.reshape(1, d)
    cb = conv_b.reshape(1, D_FF)
    if prefix is None:
        nblk = seq // tm
        row = lambda width: pl.BlockSpec((tm, width), lambda b, i: (b * nblk + i, 0))
        full = lambda a: pl.BlockSpec(a.shape, lambda b, i: (0, 0), pipeline_mode=pl.Buffered(1))
        return pl.pallas_call(
            functools.partial(_conv_ffn_kernel, tm=tm, fc=FFN_CHUNK, seq_rows=0),
            grid=(batch, nblk),
            in_specs=[row(d), full(g2), full(wg), full(wu), full(conv_w), full(cb), full(wd), full(gf)],
            out_specs=[row(d), pl.BlockSpec((None, 8, D_FF), lambda b, i: (b, 0, 0))],
            out_shape=[jax.ShapeDtypeStruct((rows, d), F32), jax.ShapeDtypeStruct((batch, 8, D_FF), F32)],
            scratch_shapes=[pltpu.VMEM((8, D_FF), F32), pltpu.VMEM((tm, D_FF), BF16)],
            compiler_params=_cparams("parallel", "arbitrary"),
            name="conv_ffn_prompt",
        )(x1, g2, wg, wu, conv_w, cb, wd, gf)
    row = lambda width: pl.BlockSpec((tm, width), lambda i: (i, 0))
    full = lambda a: pl.BlockSpec(a.shape, lambda i: (0, 0), pipeline_mode=pl.Buffered(1))
    return pl.pallas_call(
        functools.partial(_conv_ffn_kernel, tm=tm, fc=FFN_CHUNK, seq_rows=8),
        grid=(rows // tm,),
        in_specs=[row(d), full(g2), full(wg), full(wu), full(conv_w), full(cb), full(wd), full(gf), row(D_FF)],
        out_specs=[row(d), row(D_FF)],
        out_shape=[jax.ShapeDtypeStruct((rows, d), F32), jax.ShapeDtypeStruct((rows, D_FF), F32)],
        scratch_shapes=[pltpu.VMEM((tm, D_FF), BF16)],
        compiler_params=_cparams("parallel"),
        name="conv_ffn_sample",
    )(x1, g2, wg, wu, conv_w, cb, wd, gf, prefix)


def kernel(x_prompt, x_sample, mem_prompt, cache_k, cache_v, page_table, state_pool, state_ffn_conv, cache_mem_k, cache_mem_v, norm1_g, w_in, lam_q1, lam_k1, lam_q2, lam_k2, subln_g, w_pool_grp, pool_scale, w_br_attn, w_br_pool, w_br_mem, mem_norm_g, w_mem_kv, w_out, norm2_g, w_ffn_gate, w_ffn_up, ffn_conv_w, ffn_conv_b, w_ffn_down, rel_bias, final_norm_g):
    depth = w_in.shape[0]
    assert depth == 1, "single-layer step only"
    B, S, D = x_prompt.shape
    DB, DS, _ = x_sample.shape
    n_phys, page = cache_k.shape[1], cache_k.shape[2]
    n_pages = page_table.shape[1]
    past = n_pages * page
    assert DS == 8 and S % ROW_BLOCK == 0 and (DB * DS) % ROW_BLOCK == 0
    assert ATTN_TK % ATTN_TQ == 0 and ROW_BLOCK % ATTN_TK == 0 and S % ATTN_TK == 0

    w_qkv = w_in[0, :, :2048].astype(BF16)
    w_gl = w_in[0, :, 2048:].astype(BF16)
    wa, wp, wm = w_br_attn[0].astype(BF16), w_br_pool[0].astype(BF16), w_br_mem[0].astype(BF16)
    wout = w_out[0].astype(BF16)
    wg, wu, wd = w_ffn_gate[0].astype(BF16), w_ffn_up[0].astype(BF16), w_ffn_down[0].astype(BF16)
    w_mem = w_mem_kv[0].astype(BF16)
    wbd = jnp.zeros((POOL_WIDTH, POOL_WIDTH), F32)
    for gi in range(len(POOL_WINDOWS)):
        sl = slice(gi * POOL_GROUP_DIM, (gi + 1) * POOL_GROUP_DIM)
        wbd = wbd.at[sl, sl].set(w_pool_grp[0, gi])
    wbd = wbd.astype(BF16)
    lams = tuple(a[0].reshape(1, HEAD_DIM) for a in (lam_q1, lam_k1, lam_q2, lam_k2))

    prompt_bias, sample_bias = _bias_tables(rel_bias, ATTN_TQ, ATTN_TK, past, DS)

    xp = x_prompt.reshape(B * S, D)
    groups = ((0, 512, ((QK_SCALE * LOG2E, 0),)), (512, 512, ((1.0, HEAD_ROWS), (1.0, 0))),
              (1024, 512, ((1.0, HEAD_ROWS), (1.0, ATTN_TK))), (1536, 256, ((1.0, 0),)), (1792, 256, ((MEM_SCALE, 0),)))
    qp, kp, kp_b, vp, vp_t, up, qmp = _norm_proj(xp, norm1_g[0], w_qkv, groups,
                                                 (BF16, F32, BF16, F32, BF16, F32, BF16), ROW_BLOCK)
    mk_p, mv_p = _norm_proj(mem_prompt.reshape(B * N_MEM, D), mem_norm_g[0], w_mem,
                            ((0, MEM_WIDTH, ((1.0, 0),)), (MEM_WIDTH, MEM_WIDTH, ((1.0, 0),))), (F32, F32), ROW_BLOCK)
    op = _prompt_attn(lams, qp, kp_b, vp_t, prompt_bias, subln_g[0], B, S, ATTN_TQ, ATTN_TK)
    pop, omp = _prompt_side(up, qmp, mk_p, mv_p, wbd, pool_scale[0], B, S, ROW_BLOCK)
    x1p = _merge(xp, op, pop, omp, norm1_g[0], w_gl, wa, wp, wm, wout, ROW_BLOCK)
    yp, tail_p = _conv_ffn(x1p, norm2_g[0], wg, wu, ffn_conv_w[0], ffn_conv_b[0], wd, final_norm_g, ROW_BLOCK,
                           batch=B, seq=S)

    xs = x_sample.reshape(DB * DS, D)
    groups_s = ((0, 512, ((QK_SCALE, 0),)), (512, 512, ((1.0, HEAD_ROWS),)), (1024, 512, ((1.0, HEAD_ROWS),)),
                (1536, 256, ((1.0, 0),)), (1792, 256, ((MEM_SCALE, 0),)))
    qs, ks, vs, us, qms = _norm_proj(xs, norm1_g[0], w_qkv, groups_s, (F32,) * 5, ROW_BLOCK)
    seq3 = lambda a: a.reshape(DB, DS, a.shape[-1])
    head_rows = lambda a: a.reshape(DB, DS * N_HEADS, V_DIM)
    state16 = jnp.pad(state_pool[0], ((0, 0), (16 - POOL_STATE, 0), (0, 0)))
    mem_t = lambda a: jnp.transpose(a[0], (0, 2, 3, 1)).reshape(DB, MEM_WIDTH, N_MEM)
    pages = lambda a: a[0].reshape(n_phys, page * N_HEADS, V_DIM)
    os_, oms, pos_ = _sample_step(page_table, lams, seq3(qs), head_rows(ks), head_rows(vs), sample_bias, subln_g[0],
                                  seq3(qms), mem_t(cache_mem_k), mem_t(cache_mem_v), seq3(us), state16, wbd,
                                  pool_scale[0], pages(cache_k), pages(cache_v))
    flat = lambda a: a.reshape(DB * DS, a.shape[-1])
    x1s = _merge(xs, flat(os_), flat(pos_), flat(oms), norm1_g[0], w_gl, wa, wp, wm, wout, ROW_BLOCK)
    conv_prefix = jnp.pad(state_ffn_conv[0], ((0, 0), (0, DS - 2), (0, 0))).reshape(DB * DS, D_FF)
    ys, gate_s = _conv_ffn(x1s, norm2_g[0], wg, wu, ffn_conv_w[0], ffn_conv_b[0], wd, final_norm_g, ROW_BLOCK,
                           prefix=conv_prefix)

    heads = lambda a, n: a.reshape(1, n, -1, N_HEADS, V_DIM)
    new_pool_p = up.reshape(B, S, POOL_WIDTH)[:, S - POOL_STATE:][None]
    new_pool_s = jnp.concatenate([state_pool[0][:, DS:], seq3(us)], axis=1)[None]
    new_conv_p = tail_p[:, 6:8][None]
    new_conv_s = gate_s.reshape(DB, DS, D_FF)[:, DS - 2:][None]
    mem_heads = lambda a: a.reshape(1, B, N_MEM, N_HEADS, MEM_HEAD_DIM)
    return (yp.reshape(B, S, D), ys.reshape(DB, DS, D), heads(kp, B), heads(vp, B), heads(ks, DB), heads(vs, DB),
            new_pool_p, new_pool_s, new_conv_p, new_conv_s, mem_heads(mk_p), mem_heads(mv_p))
```

```python
import functools
import math

import jax
import jax.numpy as jnp
from jax import lax
from jax.experimental import pallas as pl
from jax.experimental.pallas import tpu as pltpu

F32 = jnp.float32
BF16 = jnp.bfloat16

D_MODEL = 1024
N_HEADS = 4
HEAD_DIM = 64
V_DIM = 2 * HEAD_DIM
ATTN_WIDTH = N_HEADS * V_DIM
POOL_WINDOWS = (2, 4, 8, 16)
POOL_GROUP_DIM = 64
POOL_WIDTH = 256
POOL_STATE = 15
MEM_HEAD_DIM = 64
MEM_WIDTH = 256
N_MEM = 256
D_FF = 2816
N_BUCKETS = 32
MAX_DISTANCE = 128
EPS = 1e-6
NEG_INF = -1e30
QK_SCALE = HEAD_DIM ** -0.5
MEM_SCALE = MEM_HEAD_DIM ** -0.5
LAM_INIT = 0.8 - 0.6 * math.exp(-0.3 * 0)
SUBLN_SCALE = 1.0 - LAM_INIT
LOG2E = math.log2(math.e)

ATTN_TQ = 256
ATTN_TK = 256
ROW_BLOCK = 512
FFN_CHUNK = 256
SAMPLE_KPAD = 128
SEQS_PER_STEP = 2
VMEM_LIMIT = 56 * 1024 * 1024
HEAD_ROWS = -1


def _cparams(*sem):
    return pltpu.CompilerParams(dimension_semantics=sem, vmem_limit_bytes=VMEM_LIMIT)


def _rmsnorm(x, g):
    return x * lax.rsqrt(jnp.mean(x * x, axis=-1, keepdims=True) + EPS) * g


def _dot(a, b):
    return jnp.dot(a, b, preferred_element_type=F32)


def _dot_nt(a, b):
    return lax.dot_general(a, b, (((1,), (1,)), ((), ())), preferred_element_type=F32)


def _softmax_rows(s):
    m = jnp.max(s, axis=-1, keepdims=True)
    p = jnp.exp(s - m)
    return p / jnp.sum(p, axis=-1, keepdims=True)


def _lam_value(lq1, lk1, lq2, lk2):
    a = jnp.sum(lq1 * lk1, axis=-1, keepdims=True)
    b = jnp.sum(lq2 * lk2, axis=-1, keepdims=True)
    return jnp.exp(a) - jnp.exp(b) + LAM_INIT


def _norm_proj_kernel(x_ref, g_ref, w_ref, *out_refs, groups):
    h = _rmsnorm(x_ref[...], g_ref[...]).astype(BF16)
    k = 0
    for start, size, outs in groups:
        y = _dot(h, w_ref[:, start:start + size])
        for scale, key_block in outs:
            o_ref = out_refs[k]
            k += 1
            ys = y if scale == 1.0 else y * scale
            if key_block == HEAD_ROWS:
                n = ys.shape[0]
                for hd in range(N_HEADS):
                    o_ref[pl.ds(hd, n, stride=N_HEADS), :] = ys[:, hd * V_DIM:(hd + 1) * V_DIM].astype(o_ref.dtype)
            elif key_block:
                yt = ys.T
                for b in range(o_ref.shape[0]):
                    o_ref[b] = yt[:, b * key_block:(b + 1) * key_block].astype(o_ref.dtype)
            else:
                o_ref[...] = ys.astype(o_ref.dtype)


def _norm_proj(x2d, g, w_bf16, groups, out_dtypes, tm):
    rows, d = x2d.shape
    n_cols = w_bf16.shape[1]
    out_shape, out_specs = [], []
    flat_outs = [(size, kb) for _, size, outs in groups for _, kb in outs]
    for (size, kb), dt in zip(flat_outs, out_dtypes):
        if kb == HEAD_ROWS:
            out_shape.append(jax.ShapeDtypeStruct((rows * N_HEADS, V_DIM), dt))
            out_specs.append(pl.BlockSpec((tm * N_HEADS, V_DIM), lambda i: (i, 0)))
        elif kb:
            out_shape.append(jax.ShapeDtypeStruct((rows // kb, size, kb), dt))
            out_specs.append(pl.BlockSpec((tm // kb, size, kb), lambda i: (i, 0, 0)))
        else:
            out_shape.append(jax.ShapeDtypeStruct((rows, size), dt))
            out_specs.append(pl.BlockSpec((tm, size), lambda i: (i, 0)))
    return pl.pallas_call(
        functools.partial(_norm_proj_kernel, groups=groups),
        grid=(rows // tm,),
        in_specs=[pl.BlockSpec((tm, d), lambda i: (i, 0)),
                  pl.BlockSpec((1, d), lambda i: (0, 0)),
                  pl.BlockSpec((d, n_cols), lambda i: (0, 0), pipeline_mode=pl.Buffered(1))],
        out_specs=out_specs,
        out_shape=out_shape,
        compiler_params=_cparams("parallel"),
        name="norm_proj",
    )(x2d, g.reshape(1, d), w_bf16)


def _rel_bucket(rel):
    n = jnp.maximum(rel, 0)
    max_exact = N_BUCKETS // 2
    nf = jnp.maximum(n, 1).astype(F32)
    large = max_exact + jnp.floor(jnp.log(nf / max_exact) / math.log(MAX_DISTANCE / max_exact)
                                  * (N_BUCKETS - max_exact)).astype(jnp.int32)
    large = jnp.minimum(large, N_BUCKETS - 1)
    return jnp.where(n < max_exact, n, large)


def _bias_from_rel(rel, rb_ref, head, visible):
    bucket = _rel_bucket(rel)
    acc = jnp.zeros(rel.shape, F32)
    for b in range(N_BUCKETS):
        acc = jnp.where(bucket == b, rb_ref[b, head], acc)
    return jnp.where(visible, acc, NEG_INF)


def _bias_kernel(rb_ref, pb_ref, sb_ref, *, tq, tk, past, n_new):
    n_kinds = pb_ref.shape[1]
    key = lax.broadcasted_iota(jnp.int32, (tk, 2 * tq), 0)
    qry = lax.broadcasted_iota(jnp.int32, (tk, 2 * tq), 1) % tq
    for kind in range(n_kinds):
        rel = kind * tq + qry - key
        for h in range(N_HEADS):
            pb_ref[h, kind] = _bias_from_rel(rel, rb_ref, h, rel >= 0) * LOG2E
    rows_per_head = 2 * n_new
    n_cols = sb_ref.shape[1]
    qi = lax.broadcasted_iota(jnp.int32, (rows_per_head, n_cols), 0) % n_new
    key = lax.broadcasted_iota(jnp.int32, (rows_per_head, n_cols), 1)
    rel = past + qi - key
    for h in range(N_HEADS):
        sb_ref[h * rows_per_head:(h + 1) * rows_per_head, :] = _bias_from_rel(rel, rb_ref, h, rel >= 0)


def _bias_tables(rel_bias, tq, tk, past, n_new):
    n_kinds = -(-(tk + MAX_DISTANCE - 1) // tq) + 1
    n_cols = past + SAMPLE_KPAD
    return pl.pallas_call(
        functools.partial(_bias_kernel, tq=tq, tk=tk, past=past, n_new=n_new),
        in_specs=[pl.BlockSpec(memory_space=pltpu.SMEM)],
        out_specs=[pl.BlockSpec(memory_space=pltpu.VMEM), pl.BlockSpec(memory_space=pltpu.VMEM)],
        out_shape=[jax.ShapeDtypeStruct((N_HEADS, n_kinds, tk, 2 * tq), F32),
                   jax.ShapeDtypeStruct((N_HEADS * 2 * n_new, n_cols), F32)],
        compiler_params=pltpu.CompilerParams(vmem_limit_bytes=VMEM_LIMIT),
        name="bias_tables",
    )(rel_bias)


def _subln(o, g):
    return _rmsnorm(o, g) * SUBLN_SCALE


def _two_map_queries(q_h):
    lane = lax.broadcasted_iota(jnp.int32, q_h.shape, 1)
    zero = jnp.zeros_like(q_h)
    return jnp.concatenate([jnp.where(lane < HEAD_DIM, q_h, zero), jnp.where(lane >= HEAD_DIM, q_h, zero)], axis=0)


def _prompt_attn_kernel(lq1_ref, lk1_ref, lq2_ref, lk2_ref, q_ref, k_ref, vt_ref, bias_ref, g_ref, o_ref,
                        q2_ref, m_ref, l_ref, acc_ref, s0_ref, *, tq, tk):
    i = pl.program_id(1)
    lam = _lam_value(lq1_ref[...], lk1_ref[...], lq2_ref[...], lk2_ref[...])
    n_kinds = bias_ref.shape[1]
    for h in range(N_HEADS):
        q2_ref[h] = _two_map_queries(q_ref[:, h * V_DIM:(h + 1) * V_DIM])
    m_ref[...] = jnp.full(m_ref.shape, NEG_INF, F32)
    l_ref[...] = jnp.zeros(l_ref.shape, F32)
    acc_ref[...] = jnp.zeros(acc_ref.shape, F32)

    last = (i * tq + tq - 1) // tk

    def scores(j, h):
        r0 = pl.multiple_of(j * tk, tk)
        kind = jnp.minimum((i * tq - j * tk) // tq, n_kinds - 1)
        hs = slice(h * V_DIM, (h + 1) * V_DIM)
        return _dot_nt(k_ref[pl.ds(r0, tk), hs], q2_ref[h]) + bias_ref[h, kind]

    ahead = 2
    for h in range(ahead):
        s0_ref[h] = scores(0, h)

    def body(j, carry):
        pending = [s0_ref[h] for h in range(ahead)]
        j_next = jnp.minimum(j + 1, last)
        for h in range(N_HEADS):
            hs = slice(h * V_DIM, (h + 1) * V_DIM)
            s = pending.pop(0)
            pending.append(scores(j, h + ahead) if h + ahead < N_HEADS else scores(j_next, h + ahead - N_HEADS))
            m_old = m_ref[h]
            m_new = jnp.maximum(m_old, jnp.max(s, axis=0, keepdims=True))
            alpha = jnp.exp2(m_old - m_new)
            p = jnp.exp2(s - m_new)
            l_ref[h] = alpha * l_ref[h] + jnp.sum(p, axis=0, keepdims=True)
            m_ref[h] = m_new
            acc_ref[h] = alpha * acc_ref[h] + _dot(vt_ref[j, hs, :], p.astype(BF16))
        for h in range(ahead):
            s0_ref[h] = pending[h]
        return carry

    lax.fori_loop(0, last + 1, body, 0)
    for h in range(N_HEADS):
        out_t = acc_ref[h] / l_ref[h]
        o_t = out_t[:, :tq] - lam * out_t[:, tq:]
        o_ref[:, h * V_DIM:(h + 1) * V_DIM] = _subln(o_t.T, g_ref[...]).astype(o_ref.dtype)


def _prompt_attn(lams, q, k, vt, bias, subln_g, batch, seq, tq, tk):
    vec = pl.BlockSpec((1, HEAD_DIM), lambda b, i: (0, 0))
    nq = seq // tq
    return pl.pallas_call(
        functools.partial(_prompt_attn_kernel, tq=tq, tk=tk),
        grid=(batch, nq),
        in_specs=[vec, vec, vec, vec,
                  pl.BlockSpec((tq, ATTN_WIDTH), lambda b, i: (b * nq + i, 0)),
                  pl.BlockSpec((seq, ATTN_WIDTH), lambda b, i: (b, 0)),
                  pl.BlockSpec((seq // tk, ATTN_WIDTH, tk), lambda b, i: (b, 0, 0)),
                  pl.BlockSpec(bias.shape, lambda b, i: (0, 0, 0, 0), pipeline_mode=pl.Buffered(1)),
                  pl.BlockSpec((1, V_DIM), lambda b, i: (0, 0))],
        out_specs=pl.BlockSpec((tq, ATTN_WIDTH), lambda b, i: (b * nq + i, 0)),
        out_shape=jax.ShapeDtypeStruct((batch * seq, ATTN_WIDTH), BF16),
        scratch_shapes=[pltpu.VMEM((N_HEADS, 2 * tq, V_DIM), BF16),
                        pltpu.VMEM((N_HEADS, 1, 2 * tq), F32),
                        pltpu.VMEM((N_HEADS, 1, 2 * tq), F32),
                        pltpu.VMEM((N_HEADS, V_DIM, 2 * tq), F32),
                        pltpu.VMEM((2, tk, 2 * tq), F32)],
        compiler_params=_cparams("parallel", "parallel"),
        name="prompt_attn",
    )(*lams, q, k, vt, bias, subln_g.reshape(1, V_DIM))


def _pool_branch(ext, u, pos0, wbd, scale):
    n, c = u.shape
    p = ext.shape[0] - n
    s2 = ext + pltpu.roll(ext, 1, 0)
    s4 = s2 + pltpu.roll(s2, 2, 0)
    s8 = s4 + pltpu.roll(s4, 4, 0)
    s16 = s8 + pltpu.roll(s8, 8, 0)
    lane = lax.broadcasted_iota(jnp.int32, (n, c), 1)
    grp = lane // POOL_GROUP_DIM
    win = jnp.where(grp == 0, s2[p:], jnp.where(grp == 1, s4[p:], jnp.where(grp == 2, s8[p:], s16[p:])))
    width = jnp.where(grp == 0, 2, jnp.where(grp == 1, 4, jnp.where(grp == 2, 8, 16)))
    pos = pos0 + lax.broadcasted_iota(jnp.int32, (n, c), 0)
    cnt = jnp.minimum(pos + 1, width).astype(F32)
    d = win / cnt - u
    return _dot(d.astype(BF16), wbd) * scale


def _prompt_side_kernel(u_ref, uprev_ref, qm_ref, mk_ref, mv_ref, wbd_ref, ps_ref, po_ref, om_ref, *, tm):
    i = pl.program_id(1)
    u = u_ref[...]
    prev = jnp.where(i > 0, uprev_ref[...], 0.0)
    ext = jnp.concatenate([prev, u], axis=0)
    po_ref[...] = _pool_branch(ext, u, i * tm, wbd_ref[...], ps_ref[...]).astype(po_ref.dtype)
    for h in range(N_HEADS):
        sl = slice(h * MEM_HEAD_DIM, (h + 1) * MEM_HEAD_DIM)
        p = _softmax_rows(_dot_nt(qm_ref[:, sl], mk_ref[:, sl].astype(BF16)))
        om_ref[:, sl] = _dot(p.astype(BF16), mv_ref[:, sl].astype(BF16)).astype(om_ref.dtype)


def _prompt_side(u, qm, mk, mv, wbd, pool_scale, batch, seq, tm):
    nblk = seq // tm
    halo = 16
    row = lambda b, i: (b * nblk + i, 0)
    return pl.pallas_call(
        functools.partial(_prompt_side_kernel, tm=tm),
        grid=(batch, nblk),
        in_specs=[pl.BlockSpec((tm, POOL_WIDTH), row),
                  pl.BlockSpec((halo, POOL_WIDTH),
                               lambda b, i: (jnp.maximum((b * nblk + i) * (tm // halo) - 1, 0), 0)),
                  pl.BlockSpec((tm, MEM_WIDTH), row),
                  pl.BlockSpec((N_MEM, MEM_WIDTH), lambda b, i: (b, 0)),
                  pl.BlockSpec((N_MEM, MEM_WIDTH), lambda b, i: (b, 0)),
                  pl.BlockSpec((POOL_WIDTH, POOL_WIDTH), lambda b, i: (0, 0)),
                  pl.BlockSpec((1, POOL_WIDTH), lambda b, i: (0, 0))],
        out_specs=[pl.BlockSpec((tm, POOL_WIDTH), row), pl.BlockSpec((tm, MEM_WIDTH), row)],
        out_shape=[jax.ShapeDtypeStruct((batch * seq, POOL_WIDTH), BF16),
                   jax.ShapeDtypeStruct((batch * seq, MEM_WIDTH), BF16)],
        compiler_params=_cparams("parallel", "parallel"),
        name="prompt_side",
    )(u, u, qm, mk, mv, wbd, pool_scale.reshape(1, POOL_WIDTH))


def _sample_step_kernel(pt_ref, lq1_ref, lk1_ref, lq2_ref, lk2_ref, q_ref, kn_ref, vn_ref, bias_ref, g_ref,
                        qm_ref, mkt_ref, mvt_ref, u_ref, st_ref, wbd_ref, ps_ref, ck_hbm, cv_hbm,
                        o_ref, om_ref, po_ref, s_ref, kbuf, vbuf, sem, *, n_pages, n_new, past, n_seq):
    step = pl.program_id(0)
    lam = _lam_value(lq1_ref[...], lk1_ref[...], lq2_ref[...], lk2_ref[...])
    page = kbuf.shape[2] // N_HEADS
    n_tail = s_ref.shape[1] - n_pages * page
    pad = jnp.zeros((n_tail - n_new, V_DIM), F32)
    hr = 2 * n_new
    caches = ((ck_hbm, kbuf), (cv_hbm, vbuf))

    def page_copy(which, slot, p, page_id):
        hbm, buf = caches[which]
        return pltpu.make_async_copy(hbm.at[page_id], buf.at[slot, p], sem.at[slot, which])

    def start_page(seq, slot, p):
        page_id = pt_ref[seq, p]
        page_copy(0, slot, p, page_id).start()
        page_copy(1, slot, p, page_id).start()

    def wait_pages(which, slot):
        for p in range(n_pages):
            page_copy(which, slot, p, 0).wait()

    @pl.when(step == 0)
    def _():
        for p in range(n_pages):
            start_page(0, 0, p)

    for slot in range(SEQS_PER_STEP):
        seq = step * SEQS_PER_STEP + slot
        nxt = jnp.minimum(seq + 1, n_seq - 1)
        other = 1 - slot

        def page_pair(buf, c, h, slot=slot):
            rows = pl.ds(h, page, stride=N_HEADS)
            return jnp.concatenate([buf[slot, c, rows, :], buf[slot, c + 1, rows, :]], axis=0).astype(BF16)

        wait_pages(0, slot)
        for h in range(N_HEADS):
            q2 = _two_map_queries(q_ref[slot, :, h * V_DIM:(h + 1) * V_DIM]).astype(BF16)
            for c in range(0, n_pages, 2):
                s_ref[h * hr:(h + 1) * hr, c * page:(c + 2) * page] = _dot_nt(q2, page_pair(kbuf, c, h))
                if h == 0:
                    start_page(nxt, other, c)
                    start_page(nxt, other, c + 1)
            k_tail = jnp.concatenate([kn_ref[slot, pl.ds(h, n_new, stride=N_HEADS), :], pad], axis=0).astype(BF16)
            s_ref[h * hr:(h + 1) * hr, n_pages * page:] = _dot_nt(q2, k_tail)
        prob = _softmax_rows(s_ref[...] + bias_ref[...])
        wait_pages(1, slot)
        for h in range(N_HEADS):
            a = prob[h * hr:h * hr + n_new] - lam * prob[h * hr + n_new:(h + 1) * hr]
            v_tail = jnp.concatenate([vn_ref[slot, pl.ds(h, n_new, stride=N_HEADS), :], pad], axis=0).astype(BF16)
            o_h = _dot(a[:, n_pages * page:].astype(BF16), v_tail)
            for c in range(0, n_pages, 2):
                o_h = o_h + _dot(a[:, c * page:(c + 2) * page].astype(BF16), page_pair(vbuf, c, h))
            o_ref[slot, :, h * V_DIM:(h + 1) * V_DIM] = _subln(o_h, g_ref[...]).astype(o_ref.dtype)
        qm = qm_ref[slot]
        head_of_lane = lax.broadcasted_iota(jnp.int32, qm.shape, 1) // MEM_HEAD_DIM
        qm4 = jnp.concatenate([jnp.where(head_of_lane == h, qm, 0.0) for h in range(N_HEADS)], axis=0).astype(BF16)
        pm = _softmax_rows(_dot(qm4, mkt_ref[slot].astype(BF16)))
        om_all = _dot_nt(pm.astype(BF16), mvt_ref[slot].astype(BF16))
        om = jnp.zeros(qm.shape, F32)
        for h in range(N_HEADS):
            om = jnp.where(head_of_lane == h, om_all[h * n_new:(h + 1) * n_new], om)
        om_ref[slot] = om.astype(om_ref.dtype)
        u = u_ref[slot]
        ext = jnp.concatenate([st_ref[slot], u], axis=0)
        po_ref[slot] = _pool_branch(ext, u, past, wbd_ref[...], ps_ref[...]).astype(po_ref.dtype)

    @pl.when(step == pl.num_programs(0) - 1)
    def _():
        wait_pages(0, 0)
        wait_pages(1, 0)


def _sample_step(page_table, lams, q, k_new, v_new, bias, subln_g, qm, mem_kt, mem_vt, u, state16, wbd, pool_scale,
                 cache_k, cache_v):
    n_seq, n_pages = page_table.shape
    n_new = q.shape[1]
    rows = cache_k.shape[1]
    past = n_pages * rows // N_HEADS
    assert n_seq % SEQS_PER_STEP == 0 and n_pages % 2 == 0
    vec = pl.BlockSpec((1, HEAD_DIM), lambda s, pt: (0, 0))
    per_seq = lambda a: pl.BlockSpec((SEQS_PER_STEP,) + a.shape[1:], lambda s, pt: (s,) + (0,) * (a.ndim - 1))
    const = lambda a: pl.BlockSpec(a.shape, lambda s, pt: (0,) * a.ndim)
    hbm = pl.BlockSpec(memory_space=pl.ANY)
    g = subln_g.reshape(1, V_DIM)
    ps = pool_scale.reshape(1, POOL_WIDTH)
    in_specs = [vec, vec, vec, vec, per_seq(q), per_seq(k_new), per_seq(v_new), const(bias), const(g),
                per_seq(qm), per_seq(mem_kt), per_seq(mem_vt), per_seq(u), per_seq(state16), const(wbd), const(ps),
                hbm, hbm]
    out3 = lambda width: jax.ShapeDtypeStruct((n_seq, n_new, width), F32)
    out_spec = lambda width: pl.BlockSpec((SEQS_PER_STEP, n_new, width), lambda s, pt: (s, 0, 0))
    page_buffers = pltpu.VMEM((SEQS_PER_STEP, n_pages, rows, V_DIM), F32)
    grid_spec = pltpu.PrefetchScalarGridSpec(
        num_scalar_prefetch=1,
        grid=(n_seq // SEQS_PER_STEP,),
        in_specs=in_specs,
        out_specs=[out_spec(ATTN_WIDTH), out_spec(MEM_WIDTH), out_spec(POOL_WIDTH)],
        scratch_shapes=[pltpu.VMEM(bias.shape, F32), page_buffers, page_buffers,
                        pltpu.SemaphoreType.DMA((SEQS_PER_STEP, 2))],
    )
    return pl.pallas_call(
        functools.partial(_sample_step_kernel, n_pages=n_pages, n_new=n_new, past=past, n_seq=n_seq),
        grid_spec=grid_spec,
        out_shape=[out3(ATTN_WIDTH), out3(MEM_WIDTH), out3(POOL_WIDTH)],
        compiler_params=_cparams("arbitrary"),
        name="sample_step",
    )(page_table, *lams, q, k_new, v_new, bias, g, qm, mem_kt, mem_vt, u, state16, wbd, ps, cache_k, cache_v)


def _merge_kernel(x_ref, o_ref, po_ref, om_ref, g_ref, wgl_ref, wa_ref, wp_ref, wm_ref, wout_ref, x1_ref):
    x = x_ref[...]
    h = _rmsnorm(x, g_ref[...]).astype(BF16)
    d = D_MODEL
    merged = jax.nn.sigmoid(_dot(h, wgl_ref[:, 0:d])) * _dot(o_ref[...].astype(BF16), wa_ref[...])
    merged = merged + jax.nn.sigmoid(_dot(h, wgl_ref[:, d:2 * d])) * _dot(po_ref[...].astype(BF16), wp_ref[...])
    merged = merged + jax.nn.sigmoid(_dot(h, wgl_ref[:, 2 * d:3 * d])) * _dot(om_ref[...].astype(BF16), wm_ref[...])
    x1_ref[...] = x + _dot(merged.astype(BF16), wout_ref[...])


def _merge(x2d, o, po, om, norm_g, wgl, wa, wp, wm, wout, tm):
    rows, d = x2d.shape
    row = lambda width: pl.BlockSpec((tm, width), lambda i: (i, 0))
    full = lambda a: pl.BlockSpec(a.shape, lambda i: (0, 0), pipeline_mode=pl.Buffered(1))
    g = norm_g.reshape(1, d)
    return pl.pallas_call(
        _merge_kernel,
        grid=(rows // tm,),
        in_specs=[row(d), row(ATTN_WIDTH), row(POOL_WIDTH), row(MEM_WIDTH), full(g), full(wgl), full(wa), full(wp),
                  full(wm), full(wout)],
        out_specs=row(d),
        out_shape=jax.ShapeDtypeStruct((rows, d), F32),
        compiler_params=_cparams("parallel"),
        name="merge",
    )(x2d, o, po, om, g, wgl, wa, wp, wm, wout)


def _conv_ffn_kernel(x_ref, g2_ref, wg_ref, wu_ref, cw_ref, cb_ref, wd_ref, gf_ref, *refs, tm, fc, seq_rows):
    if seq_rows:
        pre_ref, y_ref, gate_ref, act_ref = refs
    else:
        y_ref, tail_ref, carry_ref, act_ref = refs

        @pl.when(pl.program_id(1) == 0)
        def _():
            carry_ref[...] = jnp.zeros_like(carry_ref)

    x = x_ref[...]
    h = _rmsnorm(x, g2_ref[...]).astype(BF16)
    for c0 in range(0, D_FF, fc):
        cs = slice(c0, c0 + fc)
        gate = _dot(h, wg_ref[:, cs])
        up = _dot(h, wu_ref[:, cs])
        if seq_rows:
            pre = pre_ref[:, cs]
            t = lax.broadcasted_iota(jnp.int32, (tm, fc), 0) % seq_rows
            back1 = jnp.where(t >= 1, pltpu.roll(gate, 1, 0), pltpu.roll(pre, tm - 1, 0))
            back2 = jnp.where(t >= 2, pltpu.roll(gate, 2, 0), pre)
            gate_ref[:, cs] = gate
        else:
            ext = jnp.concatenate([carry_ref[:, cs], gate], axis=0)
            back1 = pltpu.roll(ext, 1, 0)[8:]
            back2 = pltpu.roll(ext, 2, 0)[8:]
            carry_ref[:, cs] = gate[tm - 8:]
        gc = cb_ref[:, cs] + cw_ref[0:1, cs] * back2
        gc = gc + cw_ref[1:2, cs] * back1
        gc = gc + cw_ref[2:3, cs] * gate
        act_ref[:, cs] = (jax.nn.gelu(gc) * up).astype(BF16)
    if not seq_rows:
        tail_ref[...] = carry_ref[...]
    y_ref[...] = _rmsnorm(x + _dot(act_ref[...], wd_ref[...]), gf_ref[...])


def _conv_ffn(x1, norm_g, wg, wu, conv_w, conv_b, wd, final_g, tm, batch=None, seq=None, prefix=None):
    rows, d = x1.shape
    g2 = norm_g.reshape(1, d)
    gf = final_g.reshape(1, d)
    cb = conv_b.reshape(1, D_FF)
    if prefix is None:
        nblk = seq // tm
        row = lambda width: pl.BlockSpec((tm, width), lambda b, i: (b * nblk + i, 0))
        full = lambda a: pl.BlockSpec(a.shape, lambda b, i: (0, 0), pipeline_mode=pl.Buffered(1))
        return pl.pallas_call(
            functools.partial(_conv_ffn_kernel, tm=tm, fc=FFN_CHUNK, seq_rows=0),
            grid=(batch, nblk),
            in_specs=[row(d), full(g2), full(wg), full(wu), full(conv_w), full(cb), full(wd), full(gf)],
            out_specs=[row(d), pl.BlockSpec((None, 8, D_FF), lambda b, i: (b, 0, 0))],
            out_shape=[jax.ShapeDtypeStruct((rows, d), F32), jax.ShapeDtypeStruct((batch, 8, D_FF), F32)],
            scratch_shapes=[pltpu.VMEM((8, D_FF), F32), pltpu.VMEM((tm, D_FF), BF16)],
            compiler_params=_cparams("parallel", "arbitrary"),
            name="conv_ffn_prompt",
        )(x1, g2, wg, wu, conv_w, cb, wd, gf)
    row = lambda width: pl.BlockSpec((tm, width), lambda i: (i, 0))
    full = lambda a: pl.BlockSpec(a.shape, lambda i: (0, 0), pipeline_mode=pl.Buffered(1))
    return pl.pallas_call(
        functools.partial(_conv_ffn_kernel, tm=tm, fc=FFN_CHUNK, seq_rows=8),
        grid=(rows // tm,),
        in_specs=[row(d), full(g2), full(wg), full(wu), full(conv_w), full(cb), full(wd), full(gf), row(D_FF)],
        out_specs=[row(d), row(D_FF)],
        out_shape=[jax.ShapeDtypeStruct((rows, d), F32), jax.ShapeDtypeStruct((rows, D_FF), F32)],
        scratch_shapes=[pltpu.VMEM((tm, D_FF), BF16)],
        compiler_params=_cparams("parallel"),
        name="conv_ffn_sample",
    )(x1, g2, wg, wu, conv_w, cb, wd, gf, prefix)


def kernel(x_prompt, x_sample, mem_prompt, cache_k, cache_v, page_table, state_pool, state_ffn_conv, cache_mem_k, cache_mem_v, norm1_g, w_in, lam_q1, lam_k1, lam_q2, lam_k2, subln_g, w_pool_grp, pool_scale, w_br_attn, w_br_pool, w_br_mem, mem_norm_g, w_mem_kv, w_out, norm2_g, w_ffn_gate, w_ffn_up, ffn_conv_w, ffn_conv_b, w_ffn_down, rel_bias, final_norm_g):
    depth = w_in.shape[0]
    assert depth == 1, "single-layer step only"
    B, S, D = x_prompt.shape
    DB, DS, _ = x_sample.shape
    n_phys, page = cache_k.shape[1], cache_k.shape[2]
    n_pages = page_table.shape[1]
    past = n_pages * page
    assert DS == 8 and S % ROW_BLOCK == 0 and (DB * DS) % ROW_BLOCK == 0
    assert ATTN_TK % ATTN_TQ == 0 and ROW_BLOCK % ATTN_TK == 0 and S % ATTN_TK == 0

    w_qkv = w_in[0, :, :2048].astype(BF16)
    w_gl = w_in[0, :, 2048:].astype(BF16)
    wa, wp, wm = w_br_attn[0].astype(BF16), w_br_pool[0].astype(BF16), w_br_mem[0].astype(BF16)
    wout = w_out[0].astype(BF16)
    wg, wu, wd = w_ffn_gate[0].astype(BF16), w_ffn_up[0].astype(BF16), w_ffn_down[0].astype(BF16)
    w_mem = w_mem_kv[0].astype(BF16)
    wbd = jnp.zeros((POOL_WIDTH, POOL_WIDTH), F32)
    for gi in range(len(POOL_WINDOWS)):
        sl = slice(gi * POOL_GROUP_DIM, (gi + 1) * POOL_GROUP_DIM)
        wbd = wbd.at[sl, sl].set(w_pool_grp[0, gi])
    wbd = wbd.astype(BF16)
    lams = tuple(a[0].reshape(1, HEAD_DIM) for a in (lam_q1, lam_k1, lam_q2, lam_k2))

    prompt_bias, sample_bias = _bias_tables(rel_bias, ATTN_TQ, ATTN_TK, past, DS)

    xp = x_prompt.reshape(B * S, D)
    groups = ((0, 512, ((QK_SCALE * LOG2E, 0),)), (512, 512, ((1.0, HEAD_ROWS), (1.0, 0))),
              (1024, 512, ((1.0, HEAD_ROWS), (1.0, ATTN_TK))), (1536, 256, ((1.0, 0),)), (1792, 256, ((MEM_SCALE, 0),)))
    qp, kp, kp_b, vp, vp_t, up, qmp = _norm_proj(xp, norm1_g[0], w_qkv, groups,
                                                 (BF16, F32, BF16, F32, BF16, F32, BF16), ROW_BLOCK)
    mk_p, mv_p = _norm_proj(mem_prompt.reshape(B * N_MEM, D), mem_norm_g[0], w_mem,
                            ((0, MEM_WIDTH, ((1.0, 0),)), (MEM_WIDTH, MEM_WIDTH, ((1.0, 0),))), (F32, F32), ROW_BLOCK)
    op = _prompt_attn(lams, qp, kp_b, vp_t, prompt_bias, subln_g[0], B, S, ATTN_TQ, ATTN_TK)
    pop, omp = _prompt_side(up, qmp, mk_p, mv_p, wbd, pool_scale[0], B, S, ROW_BLOCK)
    x1p = _merge(xp, op, pop, omp, norm1_g[0], w_gl, wa, wp, wm, wout, ROW_BLOCK)
    yp, tail_p = _conv_ffn(x1p, norm2_g[0], wg, wu, ffn_conv_w[0], ffn_conv_b[0], wd, final_norm_g, ROW_BLOCK,
                           batch=B, seq=S)

    xs = x_sample.reshape(DB * DS, D)
    groups_s = ((0, 512, ((QK_SCALE, 0),)), (512, 512, ((1.0, HEAD_ROWS),)), (1024, 512, ((1.0, HEAD_ROWS),)),
                (1536, 256, ((1.0, 0),)), (1792, 256, ((MEM_SCALE, 0),)))
    qs, ks, vs, us, qms = _norm_proj(xs, norm1_g[0], w_qkv, groups_s, (F32,) * 5, ROW_BLOCK)
    seq3 = lambda a: a.reshape(DB, DS, a.shape[-1])
    head_rows = lambda a: a.reshape(DB, DS * N_HEADS, V_DIM)
    state16 = jnp.pad(state_pool[0], ((0, 0), (16 - POOL_STATE, 0), (0, 0)))
    mem_t = lambda a: jnp.transpose(a[0], (0, 2, 3, 1)).reshape(DB, MEM_WIDTH, N_MEM)
    pages = lambda a: a[0].reshape(n_phys, page * N_HEADS, V_DIM)
    os_, oms, pos_ = _sample_step(page_table, lams, seq3(qs), head_rows(ks), head_rows(vs), sample_bias, subln_g[0],
                                  seq3(qms), mem_t(cache_mem_k), mem_t(cache_mem_v), seq3(us), state16, wbd,
                                  pool_scale[0], pages(cache_k), pages(cache_v))
    flat = lambda a: a.reshape(DB * DS, a.shape[-1])
    x1s = _merge(xs, flat(os_), flat(pos_), flat(oms), norm1_g[0], w_gl, wa, wp, wm, wout, ROW_BLOCK)
    conv_prefix = jnp.pad(state_ffn_conv[0], ((0, 0), (0, DS - 2), (0, 0))).reshape(DB * DS, D_FF)
    ys, gate_s = _conv_ffn(x1s, norm2_g[0], wg, wu, ffn_conv_w[0], ffn_conv_b[0], wd, final_norm_g, ROW_BLOCK,
                           prefix=conv_prefix)

    heads = lambda a, n: a.reshape(1, n, -1, N_HEADS, V_DIM)
    new_pool_p = up.reshape(B, S, POOL_WIDTH)[:, S - POOL_STATE:][None]
    new_pool_s = jnp.concatenate([state_pool[0][:, DS:], seq3(us)], axis=1)[None]
    new_conv_p = tail_p[:, 6:8][None]
    new_conv_s = gate_s.reshape(DB, DS, D_FF)[:, DS - 2:][None]
    mem_heads = lambda a: a.reshape(1, B, N_MEM, N_HEADS, MEM_HEAD_DIM)
    return (yp.reshape(B, S, D), ys.reshape(DB, DS, D), heads(kp, B), heads(vp, B), heads(ks, DB), heads(vs, DB),
            new_pool_p, new_pool_s, new_conv_p, new_conv_s, mem_heads(mk_p), mem_heads(mv_p))
```

```python
import functools
import math

import jax
import jax.numpy as jnp
from jax import lax
from jax.experimental import pallas as pl
from jax.experimental.pallas import tpu as pltpu

F32 = jnp.float32
BF16 = jnp.bfloat16

D_MODEL = 1024
N_HEADS = 4
HEAD_DIM = 64
V_DIM = 2 * HEAD_DIM
ATTN_WIDTH = N_HEADS * V_DIM
POOL_WINDOWS = (2, 4, 8, 16)
POOL_GROUP_DIM = 64
POOL_WIDTH = 256
POOL_STATE = 15
MEM_HEAD_DIM = 64
MEM_WIDTH = 256
N_MEM = 256
D_FF = 2816
N_BUCKETS = 32
MAX_DISTANCE = 128
EPS = 1e-6
NEG_INF = -1e30
QK_SCALE = HEAD_DIM ** -0.5
MEM_SCALE = MEM_HEAD_DIM ** -0.5
LAM_INIT = 0.8 - 0.6 * math.exp(-0.3 * 0)
SUBLN_SCALE = 1.0 - LAM_INIT
LOG2E = math.log2(math.e)

ATTN_TQ = 256
ATTN_TK = 256
ROW_BLOCK = 512
FFN_CHUNK = 256
SAMPLE_KPAD = 128
VMEM_LIMIT = 56 * 1024 * 1024
HEAD_ROWS = -1


def _cparams(*sem):
    return pltpu.CompilerParams(dimension_semantics=sem, vmem_limit_bytes=VMEM_LIMIT)


def _rmsnorm(x, g):
    return x * lax.rsqrt(jnp.mean(x * x, axis=-1, keepdims=True) + EPS) * g


def _dot(a, b):
    return jnp.dot(a, b, preferred_element_type=F32)


def _dot_nt(a, b):
    return lax.dot_general(a, b, (((1,), (1,)), ((), ())), preferred_element_type=F32)


def _softmax_rows(s):
    m = jnp.max(s, axis=-1, keepdims=True)
    p = jnp.exp(s - m)
    return p * (1.0 / jnp.sum(p, axis=-1, keepdims=True))


def _lam_value(lq1, lk1, lq2, lk2):
    a = jnp.sum(lq1 * lk1, axis=-1, keepdims=True)
    b = jnp.sum(lq2 * lk2, axis=-1, keepdims=True)
    return jnp.exp(a) - jnp.exp(b) + LAM_INIT


def _norm_proj_kernel(x_ref, g_ref, w_ref, *out_refs, groups):
    h = _rmsnorm(x_ref[...], g_ref[...]).astype(BF16)
    k = 0
    for start, size, outs in groups:
        y = _dot(h, w_ref[:, start:start + size])
        for scale, key_block in outs:
            o_ref = out_refs[k]
            k += 1
            ys = y if scale == 1.0 else y * scale
            if key_block == HEAD_ROWS:
                n = ys.shape[0]
                for hd in range(N_HEADS):
                    o_ref[pl.ds(hd, n, stride=N_HEADS), :] = ys[:, hd * V_DIM:(hd + 1) * V_DIM].astype(o_ref.dtype)
            elif key_block:
                yt = ys.T
                for b in range(o_ref.shape[0]):
                    o_ref[b] = yt[:, b * key_block:(b + 1) * key_block].astype(o_ref.dtype)
            else:
                o_ref[...] = ys.astype(o_ref.dtype)


def _norm_proj(x2d, g, w_bf16, groups, out_dtypes, tm):
    rows, d = x2d.shape
    n_cols = w_bf16.shape[1]
    out_shape, out_specs = [], []
    flat_outs = [(size, kb) for _, size, outs in groups for _, kb in outs]
    for (size, kb), dt in zip(flat_outs, out_dtypes):
        if kb == HEAD_ROWS:
            out_shape.append(jax.ShapeDtypeStruct((rows * N_HEADS, V_DIM), dt))
            out_specs.append(pl.BlockSpec((tm * N_HEADS, V_DIM), lambda i: (i, 0)))
        elif kb:
            out_shape.append(jax.ShapeDtypeStruct((rows // kb, size, kb), dt))
            out_specs.append(pl.BlockSpec((tm // kb, size, kb), lambda i: (i, 0, 0)))
        else:
            out_shape.append(jax.ShapeDtypeStruct((rows, size), dt))
            out_specs.append(pl.BlockSpec((tm, size), lambda i: (i, 0)))
    return pl.pallas_call(
        functools.partial(_norm_proj_kernel, groups=groups),
        grid=(rows // tm,),
        in_specs=[pl.BlockSpec((tm, d), lambda i: (i, 0)),
                  pl.BlockSpec((1, d), lambda i: (0, 0)),
                  pl.BlockSpec((d, n_cols), lambda i: (0, 0), pipeline_mode=pl.Buffered(1))],
        out_specs=out_specs,
        out_shape=out_shape,
        compiler_params=_cparams("parallel"),
        name="norm_proj",
    )(x2d, g.reshape(1, d), w_bf16)


def _rel_bucket(rel):
    n = jnp.maximum(rel, 0)
    max_exact = N_BUCKETS // 2
    nf = jnp.maximum(n, 1).astype(F32)
    large = max_exact + jnp.floor(jnp.log(nf / max_exact) / math.log(MAX_DISTANCE / max_exact)
                                  * (N_BUCKETS - max_exact)).astype(jnp.int32)
    large = jnp.minimum(large, N_BUCKETS - 1)
    return jnp.where(n < max_exact, n, large)


def _bias_from_rel(rel, rb_ref, head, visible):
    bucket = _rel_bucket(rel)
    acc = jnp.zeros(rel.shape, F32)
    for b in range(N_BUCKETS):
        acc = jnp.where(bucket == b, rb_ref[b, head], acc)
    return jnp.where(visible, acc, NEG_INF)


def _bias_kernel(rb_ref, pb_ref, sb_ref, *, tq, tk, past, n_new):
    n_kinds = pb_ref.shape[1]
    key = lax.broadcasted_iota(jnp.int32, (tk, 2 * tq), 0)
    qry = lax.broadcasted_iota(jnp.int32, (tk, 2 * tq), 1) % tq
    for kind in range(n_kinds):
        rel = kind * tq + qry - key
        for h in range(N_HEADS):
            pb_ref[h, kind] = _bias_from_rel(rel, rb_ref, h, rel >= 0) * LOG2E
    rows_per_head = 2 * n_new
    n_cols = sb_ref.shape[1]
    qi = lax.broadcasted_iota(jnp.int32, (rows_per_head, n_cols), 0) % n_new
    key = lax.broadcasted_iota(jnp.int32, (rows_per_head, n_cols), 1)
    rel = past + qi - key
    for h in range(N_HEADS):
        sb_ref[h * rows_per_head:(h + 1) * rows_per_head, :] = _bias_from_rel(rel, rb_ref, h, rel >= 0)


def _bias_tables(rel_bias, tq, tk, past, n_new):
    n_kinds = -(-(tk + MAX_DISTANCE - 1) // tq) + 1
    n_cols = past + SAMPLE_KPAD
    return pl.pallas_call(
        functools.partial(_bias_kernel, tq=tq, tk=tk, past=past, n_new=n_new),
        in_specs=[pl.BlockSpec(memory_space=pltpu.SMEM)],
        out_specs=[pl.BlockSpec(memory_space=pltpu.VMEM), pl.BlockSpec(memory_space=pltpu.VMEM)],
        out_shape=[jax.ShapeDtypeStruct((N_HEADS, n_kinds, tk, 2 * tq), F32),
                   jax.ShapeDtypeStruct((N_HEADS * 2 * n_new, n_cols), F32)],
        compiler_params=pltpu.CompilerParams(vmem_limit_bytes=VMEM_LIMIT),
        name="bias_tables",
    )(rel_bias)


def _subln(o, g):
    return _rmsnorm(o, g) * SUBLN_SCALE


def _two_map_queries(q_h):
    lane = lax.broadcasted_iota(jnp.int32, q_h.shape, 1)
    zero = jnp.zeros_like(q_h)
    return jnp.concatenate([jnp.where(lane < HEAD_DIM, q_h, zero), jnp.where(lane >= HEAD_DIM, q_h, zero)], axis=0)


def _prompt_attn_kernel(lq1_ref, lk1_ref, lq2_ref, lk2_ref, q_ref, k_ref, vt_ref, bias_ref, g_ref, o_ref,
                        q2_ref, m_ref, l_ref, acc_ref, s0_ref, *, tq, tk):
    i = pl.program_id(1)
    lam = _lam_value(lq1_ref[...], lk1_ref[...], lq2_ref[...], lk2_ref[...])
    n_kinds = bias_ref.shape[1]
    for h in range(N_HEADS):
        q2_ref[h] = _two_map_queries(q_ref[:, h * V_DIM:(h + 1) * V_DIM])
    m_ref[...] = jnp.full(m_ref.shape, NEG_INF, F32)
    l_ref[...] = jnp.zeros(l_ref.shape, F32)
    acc_ref[...] = jnp.zeros(acc_ref.shape, F32)

    last = (i * tq + tq - 1) // tk

    def scores(j, h):
        r0 = pl.multiple_of(j * tk, tk)
        kind = jnp.minimum((i * tq - j * tk) // tq, n_kinds - 1)
        hs = slice(h * V_DIM, (h + 1) * V_DIM)
        return _dot_nt(k_ref[pl.ds(r0, tk), hs], q2_ref[h]) + bias_ref[h, kind]

    ahead = 2
    for h in range(ahead):
        s0_ref[h] = scores(0, h)

    ones_rows = jnp.ones((16, tk), BF16)

    def body(j, carry):
        pending = [s0_ref[h] for h in range(ahead)]
        j_next = jnp.minimum(j + 1, last)
        for h in range(N_HEADS):
            hs = slice(h * V_DIM, (h + 1) * V_DIM)
            s = pending.pop(0)
            pending.append(scores(j, h + ahead) if h + ahead < N_HEADS else scores(j_next, h + ahead - N_HEADS))
            m_old = m_ref[h]
            m_new = jnp.maximum(m_old, jnp.max(s, axis=0, keepdims=True))
            alpha = jnp.exp2(m_old - m_new)
            p = jnp.exp2(s - m_new).astype(BF16)
            pv = _dot(jnp.concatenate([vt_ref[j, hs, :], ones_rows], axis=0), p)
            l_ref[h] = alpha * l_ref[h] + pv[V_DIM:V_DIM + 1]
            m_ref[h] = m_new
            acc_ref[h] = alpha * acc_ref[h] + pv[:V_DIM]
        for h in range(ahead):
            s0_ref[h] = pending[h]
        return carry

    lax.fori_loop(0, last + 1, body, 0)
    for h in range(N_HEADS):
        out_t = acc_ref[h] / l_ref[h]
        o_t = out_t[:, :tq] - lam * out_t[:, tq:]
        o_ref[:, h * V_DIM:(h + 1) * V_DIM] = _subln(o_t.T, g_ref[...]).astype(o_ref.dtype)


def _prompt_attn(lams, q, k, vt, bias, subln_g, batch, seq, tq, tk):
    vec = pl.BlockSpec((1, HEAD_DIM), lambda b, i: (0, 0))
    nq = seq // tq
    return pl.pallas_call(
        functools.partial(_prompt_attn_kernel, tq=tq, tk=tk),
        grid=(batch, nq),
        in_specs=[vec, vec, vec, vec,
                  pl.BlockSpec((tq, ATTN_WIDTH), lambda b, i: (b * nq + i, 0)),
                  pl.BlockSpec((seq, ATTN_WIDTH), lambda b, i: (b, 0)),
                  pl.BlockSpec((seq // tk, ATTN_WIDTH, tk), lambda b, i: (b, 0, 0)),
                  pl.BlockSpec(bias.shape, lambda b, i: (0, 0, 0, 0), pipeline_mode=pl.Buffered(1)),
                  pl.BlockSpec((1, V_DIM), lambda b, i: (0, 0))],
        out_specs=pl.BlockSpec((tq, ATTN_WIDTH), lambda b, i: (b * nq + i, 0)),
        out_shape=jax.ShapeDtypeStruct((batch * seq, ATTN_WIDTH), BF16),
        scratch_shapes=[pltpu.VMEM((N_HEADS, 2 * tq, V_DIM), BF16),
                        pltpu.VMEM((N_HEADS, 1, 2 * tq), F32),
                        pltpu.VMEM((N_HEADS, 1, 2 * tq), F32),
                        pltpu.VMEM((N_HEADS, V_DIM, 2 * tq), F32),
                        pltpu.VMEM((2, tk, 2 * tq), F32)],
        compiler_params=_cparams("parallel", "parallel"),
        name="prompt_attn",
    )(*lams, q, k, vt, bias, subln_g.reshape(1, V_DIM))


def _pool_branch(ext, u, pos0, wbd, scale):
    n, c = u.shape
    p = ext.shape[0] - n
    s2 = ext + pltpu.roll(ext, 1, 0)
    s4 = s2 + pltpu.roll(s2, 2, 0)
    s8 = s4 + pltpu.roll(s4, 4, 0)
    s16 = s8 + pltpu.roll(s8, 8, 0)
    lane = lax.broadcasted_iota(jnp.int32, (n, c), 1)
    grp = lane // POOL_GROUP_DIM
    win = jnp.where(grp == 0, s2[p:], jnp.where(grp == 1, s4[p:], jnp.where(grp == 2, s8[p:], s16[p:])))
    width = jnp.where(grp == 0, 2, jnp.where(grp == 1, 4, jnp.where(grp == 2, 8, 16)))
    pos = pos0 + lax.broadcasted_iota(jnp.int32, (n, c), 0)
    cnt = jnp.minimum(pos + 1, width).astype(F32)
    d = win / cnt - u
    return _dot(d.astype(BF16), wbd) * scale


def _prompt_side_kernel(u_ref, uprev_ref, qm_ref, mk_ref, mv_ref, wbd_ref, ps_ref, po_ref, om_ref, *, tm):
    i = pl.program_id(1)
    u = u_ref[...]
    prev = jnp.where(i > 0, uprev_ref[...], 0.0)
    ext = jnp.concatenate([prev, u], axis=0)
    po_ref[...] = _pool_branch(ext, u, i * tm, wbd_ref[...], ps_ref[...]).astype(po_ref.dtype)
    heads = [slice(h * MEM_HEAD_DIM, (h + 1) * MEM_HEAD_DIM) for h in range(N_HEADS)]
    scores = [_dot_nt(qm_ref[:, sl], mk_ref[:, sl].astype(BF16)) for sl in heads]
    probs = [_softmax_rows(s).astype(BF16) for s in scores]
    for sl, p in zip(heads, probs):
        om_ref[:, sl] = _dot(p, mv_ref[:, sl].astype(BF16)).astype(om_ref.dtype)


def _prompt_side(u, qm, mk, mv, wbd, pool_scale, batch, seq, tm):
    nblk = seq // tm
    halo = 16
    row = lambda b, i: (b * nblk + i, 0)
    return pl.pallas_call(
        functools.partial(_prompt_side_kernel, tm=tm),
        grid=(batch, nblk),
        in_specs=[pl.BlockSpec((tm, POOL_WIDTH), row),
                  pl.BlockSpec((halo, POOL_WIDTH),
                               lambda b, i: (jnp.maximum((b * nblk + i) * (tm // halo) - 1, 0), 0)),
                  pl.BlockSpec((tm, MEM_WIDTH), row),
                  pl.BlockSpec((N_MEM, MEM_WIDTH), lambda b, i: (b, 0)),
                  pl.BlockSpec((N_MEM, MEM_WIDTH), lambda b, i: (b, 0)),
                  pl.BlockSpec((POOL_WIDTH, POOL_WIDTH), lambda b, i: (0, 0)),
                  pl.BlockSpec((1, POOL_WIDTH), lambda b, i: (0, 0))],
        out_specs=[pl.BlockSpec((tm, POOL_WIDTH), row), pl.BlockSpec((tm, MEM_WIDTH), row)],
        out_shape=[jax.ShapeDtypeStruct((batch * seq, POOL_WIDTH), BF16),
                   jax.ShapeDtypeStruct((batch * seq, MEM_WIDTH), BF16)],
        compiler_params=_cparams("parallel", "parallel"),
        name="prompt_side",
    )(u, u, qm, mk, mv, wbd, pool_scale.reshape(1, POOL_WIDTH))


def _sample_step_kernel(pt_ref, lq1_ref, lk1_ref, lq2_ref, lk2_ref, q_ref, kn_ref, vn_ref, bias_ref, g_ref,
                        qm_ref, mkt_ref, mvt_ref, u_ref, st_ref, wbd_ref, ps_ref, *refs, n_pages, n_new, past):
    del pt_ref
    kp = refs[:n_pages]
    vp = refs[n_pages:2 * n_pages]
    o_ref, om_ref, po_ref, s_ref = refs[2 * n_pages:]
    lam = _lam_value(lq1_ref[...], lk1_ref[...], lq2_ref[...], lk2_ref[...])
    page = kp[0].shape[0] // N_HEADS
    n_tail = s_ref.shape[1] - n_pages * page
    pad = jnp.zeros((n_tail - n_new, V_DIM), F32)
    hr = 2 * n_new

    def head_rows(ref, h, n):
        return ref[pl.ds(h, n, stride=N_HEADS), :]

    def page_pair(refs_, c, h):
        return jnp.concatenate([head_rows(refs_[c], h, page), head_rows(refs_[c + 1], h, page)], axis=0).astype(BF16)

    for h in range(N_HEADS):
        q2 = _two_map_queries(q_ref[:, h * V_DIM:(h + 1) * V_DIM]).astype(BF16)
        for c in range(0, n_pages, 2):
            s_ref[h * hr:(h + 1) * hr, c * page:(c + 2) * page] = _dot_nt(q2, page_pair(kp, c, h))
        k_tail = jnp.concatenate([head_rows(kn_ref, h, n_new), pad], axis=0).astype(BF16)
        s_ref[h * hr:(h + 1) * hr, n_pages * page:] = _dot_nt(q2, k_tail)
    prob = _softmax_rows(s_ref[...] + bias_ref[...])
    for h in range(N_HEADS):
        a = prob[h * hr:h * hr + n_new] - lam * prob[h * hr + n_new:(h + 1) * hr]
        v_tail = jnp.concatenate([head_rows(vn_ref, h, n_new), pad], axis=0).astype(BF16)
        o_h = _dot(a[:, n_pages * page:].astype(BF16), v_tail)
        for c in range(0, n_pages, 2):
            o_h = o_h + _dot(a[:, c * page:(c + 2) * page].astype(BF16), page_pair(vp, c, h))
        o_ref[:, h * V_DIM:(h + 1) * V_DIM] = _subln(o_h, g_ref[...]).astype(o_ref.dtype)
    qm = qm_ref[...]
    head_of_lane = lax.broadcasted_iota(jnp.int32, qm.shape, 1) // MEM_HEAD_DIM
    qm4 = jnp.concatenate([jnp.where(head_of_lane == h, qm, 0.0) for h in range(N_HEADS)], axis=0).astype(BF16)
    pm = _softmax_rows(_dot(qm4, mkt_ref[...].astype(BF16)))
    om_all = _dot_nt(pm.astype(BF16), mvt_ref[...].astype(BF16))
    om = jnp.zeros(qm.shape, F32)
    for h in range(N_HEADS):
        om = jnp.where(head_of_lane == h, om_all[h * n_new:(h + 1) * n_new], om)
    om_ref[...] = om.astype(om_ref.dtype)
    u = u_ref[...]
    ext = jnp.concatenate([st_ref[...], u], axis=0)
    po_ref[...] = _pool_branch(ext, u, past, wbd_ref[...], ps_ref[...]).astype(po_ref.dtype)


def _sample_step(page_table, lams, q, k_new, v_new, bias, subln_g, qm, mem_kt, mem_vt, u, state16, wbd, pool_scale,
                 cache_k, cache_v):
    n_seq, n_pages = page_table.shape
    n_new = q.shape[1]
    rows = cache_k.shape[1]
    past = n_pages * rows // N_HEADS
    assert n_pages % 2 == 0
    vec = pl.BlockSpec((1, HEAD_DIM), lambda s, pt: (0, 0))
    per_seq = lambda a: pl.BlockSpec((None,) + a.shape[1:], lambda s, pt: (s,) + (0,) * (a.ndim - 1))
    const = lambda a: pl.BlockSpec(a.shape, lambda s, pt: (0,) * a.ndim)
    page_spec = lambda p: pl.BlockSpec((None, rows, V_DIM), lambda s, pt, p=p: (pt[s, p], 0, 0))
    g = subln_g.reshape(1, V_DIM)
    ps = pool_scale.reshape(1, POOL_WIDTH)
    in_specs = [vec, vec, vec, vec, per_seq(q), per_seq(k_new), per_seq(v_new), const(bias), const(g),
                per_seq(qm), per_seq(mem_kt), per_seq(mem_vt), per_seq(u), per_seq(state16), const(wbd), const(ps)]
    in_specs += [page_spec(p) for p in range(n_pages)] * 2
    out3 = lambda width: jax.ShapeDtypeStruct((n_seq, n_new, width), F32)
    out_spec = lambda width: pl.BlockSpec((None, n_new, width), lambda s, pt: (s, 0, 0))
    grid_spec = pltpu.PrefetchScalarGridSpec(
        num_scalar_prefetch=1,
        grid=(n_seq,),
        in_specs=in_specs,
        out_specs=[out_spec(ATTN_WIDTH), out_spec(MEM_WIDTH), out_spec(POOL_WIDTH)],
        scratch_shapes=[pltpu.VMEM(bias.shape, F32)],
    )
    return pl.pallas_call(
        functools.partial(_sample_step_kernel, n_pages=n_pages, n_new=n_new, past=past),
        grid_spec=grid_spec,
        out_shape=[out3(ATTN_WIDTH), out3(MEM_WIDTH), out3(POOL_WIDTH)],
        compiler_params=_cparams("parallel"),
        name="sample_step",
    )(page_table, *lams, q, k_new, v_new, bias, g, qm, mem_kt, mem_vt, u, state16, wbd, ps,
      *([cache_k] * n_pages), *([cache_v] * n_pages))


def _merge_kernel(x_ref, o_ref, po_ref, om_ref, g_ref, wgl_ref, wa_ref, wp_ref, wm_ref, wout_ref, x1_ref):
    x = x_ref[...]
    h = _rmsnorm(x, g_ref[...]).astype(BF16)
    d = D_MODEL
    merged = jax.nn.sigmoid(_dot(h, wgl_ref[:, 0:d])) * _dot(o_ref[...].astype(BF16), wa_ref[...])
    merged = merged + jax.nn.sigmoid(_dot(h, wgl_ref[:, d:2 * d])) * _dot(po_ref[...].astype(BF16), wp_ref[...])
    merged = merged + jax.nn.sigmoid(_dot(h, wgl_ref[:, 2 * d:3 * d])) * _dot(om_ref[...].astype(BF16), wm_ref[...])
    x1_ref[...] = x + _dot(merged.astype(BF16), wout_ref[...])


def _merge(x2d, o, po, om, norm_g, wgl, wa, wp, wm, wout, tm):
    rows, d = x2d.shape
    row = lambda width: pl.BlockSpec((tm, width), lambda i: (i, 0))
    full = lambda a: pl.BlockSpec(a.shape, lambda i: (0, 0), pipeline_mode=pl.Buffered(1))
    g = norm_g.reshape(1, d)
    return pl.pallas_call(
        _merge_kernel,
        grid=(rows // tm,),
        in_specs=[row(d), row(ATTN_WIDTH), row(POOL_WIDTH), row(MEM_WIDTH), full(g), full(wgl), full(wa), full(wp),
                  full(wm), full(wout)],
        out_specs=row(d),
        out_shape=jax.ShapeDtypeStruct((rows, d), F32),
        compiler_params=_cparams("parallel"),
        name="merge",
    )(x2d, o, po, om, g, wgl, wa, wp, wm, wout)


def _conv_ffn_kernel(x_ref, g2_ref, wg_ref, wu_ref, cw_ref, cb_ref, wd_ref, gf_ref, *refs, tm, fc, seq_rows):
    if seq_rows:
        pre_ref, y_ref, gate_ref, act_ref = refs
    else:
        y_ref, tail_ref, carry_ref, act_ref = refs

        @pl.when(pl.program_id(1) == 0)
        def _():
            carry_ref[...] = jnp.zeros_like(carry_ref)

    x = x_ref[...]
    h = _rmsnorm(x, g2_ref[...]).astype(BF16)
    for c0 in range(0, D_FF, fc):
        cs = slice(c0, c0 + fc)
        gate = _dot(h, wg_ref[:, cs])
        up = _dot(h, wu_ref[:, cs])
        if seq_rows:
            pre = pre_ref[:, cs]
            t = lax.broadcasted_iota(jnp.int32, (tm, fc), 0) % seq_rows
            back1 = jnp.where(t >= 1, pltpu.roll(gate, 1, 0), pltpu.roll(pre, tm - 1, 0))
            back2 = jnp.where(t >= 2, pltpu.roll(gate, 2, 0), pre)
            gate_ref[:, cs] = gate
        else:
            ext = jnp.concatenate([carry_ref[:, cs], gate], axis=0)
            back1 = pltpu.roll(ext, 1, 0)[8:]
            back2 = pltpu.roll(ext, 2, 0)[8:]
            carry_ref[:, cs] = gate[tm - 8:]
        gc = cb_ref[:, cs] + cw_ref[0:1, cs] * back2
        gc = gc + cw_ref[1:2, cs] * back1
        gc = gc + cw_ref[2:3, cs] * gate
        act_ref[:, cs] = (jax.nn.gelu(gc) * up).astype(BF16)
    if not seq_rows:
        tail_ref[...] = carry_ref[...]
    y_ref[...] = _rmsnorm(x + _dot(act_ref[...], wd_ref[...]), gf_ref[...])


def _conv_ffn(x1, norm_g, wg, wu, conv_w, conv_b, wd, final_g, tm, batch=None, seq=None, prefix=None):
    rows, d = x1.shape
    g2 = norm_g.reshape(1, d)
    gf = final_g.reshape(1, d)
    cb = conv_b.reshape(1, D_FF)
    if prefix is None:
        nblk = seq // tm
        row = lambda width: pl.BlockSpec((tm, width), lambda b, i: (b * nblk + i, 0))
        full = lambda a: pl.BlockSpec(a.shape, lambda b, i: (0, 0), pipeline_mode=pl.Buffered(1))
        return pl.pallas_call(
            functools.partial(_conv_ffn_kernel, tm=tm, fc=FFN_CHUNK, seq_rows=0),
            grid=(batch, nblk),
            in_specs=[row(d), full(g2), full(wg), full(wu), full(conv_w), full(cb), full(wd), full(gf)],
            out_specs=[row(d), pl.BlockSpec((None, 8, D_FF), lambda b, i: (b, 0, 0))],
            out_shape=[jax.ShapeDtypeStruct((rows, d), F32), jax.ShapeDtypeStruct((batch, 8, D_FF), F32)],
            scratch_shapes=[pltpu.VMEM((8, D_FF), F32), pltpu.VMEM((tm, D_FF), BF16)],
            compiler_params=_cparams("parallel", "arbitrary"),
            name="conv_ffn_prompt",
        )(x1, g2, wg, wu, conv_w, cb, wd, gf)
    row = lambda width: pl.BlockSpec((tm, width), lambda i: (i, 0))
    full = lambda a: pl.BlockSpec(a.shape, lambda i: (0, 0), pipeline_mode=pl.Buffered(1))
    return pl.pallas_call(
        functools.partial(_conv_ffn_kernel, tm=tm, fc=FFN_CHUNK, seq_rows=8),
        grid=(rows // tm,),
        in_specs=[row(d), full(g2), full(wg), full(wu), full(conv_w), full(cb), full(wd), full(gf), row(D_FF)],
        out_specs=[row(d), row(D_FF)],
        out_shape=[jax.ShapeDtypeStruct((rows, d), F32), jax.ShapeDtypeStruct((rows, D_FF), F32)],
        scratch_shapes=[pltpu.VMEM((tm, D_FF), BF16)],
        compiler_params=_cparams("parallel"),
        name="conv_ffn_sample",
    )(x1, g2, wg, wu, conv_w, cb, wd, gf, prefix)


def kernel(x_prompt, x_sample, mem_prompt, cache_k, cache_v, page_table, state_pool, state_ffn_conv, cache_mem_k, cache_mem_v, norm1_g, w_in, lam_q1, lam_k1, lam_q2, lam_k2, subln_g, w_pool_grp, pool_scale, w_br_attn, w_br_pool, w_br_mem, mem_norm_g, w_mem_kv, w_out, norm2_g, w_ffn_gate, w_ffn_up, ffn_conv_w, ffn_conv_b, w_ffn_down, rel_bias, final_norm_g):
    depth = w_in.shape[0]
    assert depth == 1, "single-layer step only"
    B, S, D = x_prompt.shape
    DB, DS, _ = x_sample.shape
    n_phys, page = cache_k.shape[1], cache_k.shape[2]
    n_pages = page_table.shape[1]
    past = n_pages * page
    assert DS == 8 and S % ROW_BLOCK == 0 and (DB * DS) % ROW_BLOCK == 0
    assert ATTN_TK % ATTN_TQ == 0 and ROW_BLOCK % ATTN_TK == 0 and S % ATTN_TK == 0

    w_qkv = w_in[0, :, :2048].astype(BF16)
    w_gl = w_in[0, :, 2048:].astype(BF16)
    wa, wp, wm = w_br_attn[0].astype(BF16), w_br_pool[0].astype(BF16), w_br_mem[0].astype(BF16)
    wout = w_out[0].astype(BF16)
    wg, wu, wd = w_ffn_gate[0].astype(BF16), w_ffn_up[0].astype(BF16), w_ffn_down[0].astype(BF16)
    w_mem = w_mem_kv[0].astype(BF16)
    wbd = jnp.zeros((POOL_WIDTH, POOL_WIDTH), F32)
    for gi in range(len(POOL_WINDOWS)):
        sl = slice(gi * POOL_GROUP_DIM, (gi + 1) * POOL_GROUP_DIM)
        wbd = wbd.at[sl, sl].set(w_pool_grp[0, gi])
    wbd = wbd.astype(BF16)
    lams = tuple(a[0].reshape(1, HEAD_DIM) for a in (lam_q1, lam_k1, lam_q2, lam_k2))

    prompt_bias, sample_bias = _bias_tables(rel_bias, ATTN_TQ, ATTN_TK, past, DS)

    xp = x_prompt.reshape(B * S, D)
    groups = ((0, 512, ((QK_SCALE * LOG2E, 0),)), (512, 512, ((1.0, HEAD_ROWS), (1.0, 0))),
              (1024, 512, ((1.0, HEAD_ROWS), (1.0, ATTN_TK))), (1536, 256, ((1.0, 0),)), (1792, 256, ((MEM_SCALE, 0),)))
    qp, kp, kp_b, vp, vp_t, up, qmp = _norm_proj(xp, norm1_g[0], w_qkv, groups,
                                                 (BF16, F32, BF16, F32, BF16, F32, BF16), ROW_BLOCK)
    mk_p, mv_p = _norm_proj(mem_prompt.reshape(B * N_MEM, D), mem_norm_g[0], w_mem,
                            ((0, MEM_WIDTH, ((1.0, 0),)), (MEM_WIDTH, MEM_WIDTH, ((1.0, 0),))), (F32, F32), ROW_BLOCK)
    op = _prompt_attn(lams, qp, kp_b, vp_t, prompt_bias, subln_g[0], B, S, ATTN_TQ, ATTN_TK)
    pop, omp = _prompt_side(up, qmp, mk_p, mv_p, wbd, pool_scale[0], B, S, ROW_BLOCK)
    x1p = _merge(xp, op, pop, omp, norm1_g[0], w_gl, wa, wp, wm, wout, ROW_BLOCK)
    yp, tail_p = _conv_ffn(x1p, norm2_g[0], wg, wu, ffn_conv_w[0], ffn_conv_b[0], wd, final_norm_g, ROW_BLOCK,
                           batch=B, seq=S)

    xs = x_sample.reshape(DB * DS, D)
    groups_s = ((0, 512, ((QK_SCALE, 0),)), (512, 512, ((1.0, HEAD_ROWS),)), (1024, 512, ((1.0, HEAD_ROWS),)),
                (1536, 256, ((1.0, 0),)), (1792, 256, ((MEM_SCALE, 0),)))
    qs, ks, vs, us, qms = _norm_proj(xs, norm1_g[0], w_qkv, groups_s, (F32,) * 5, ROW_BLOCK)
    seq3 = lambda a: a.reshape(DB, DS, a.shape[-1])
    head_rows = lambda a: a.reshape(DB, DS * N_HEADS, V_DIM)
    state16 = jnp.pad(state_pool[0], ((0, 0), (16 - POOL_STATE, 0), (0, 0)))
    mem_t = lambda a: jnp.transpose(a[0], (0, 2, 3, 1)).reshape(DB, MEM_WIDTH, N_MEM)
    pages = lambda a: a[0].reshape(n_phys, page * N_HEADS, V_DIM)
    os_, oms, pos_ = _sample_step(page_table, lams, seq3(qs), head_rows(ks), head_rows(vs), sample_bias, subln_g[0],
                                  seq3(qms), mem_t(cache_mem_k), mem_t(cache_mem_v), seq3(us), state16, wbd,
                                  pool_scale[0], pages(cache_k), pages(cache_v))
    flat = lambda a: a.reshape(DB * DS, a.shape[-1])
    x1s = _merge(xs, flat(os_), flat(pos_), flat(oms), norm1_g[0], w_gl, wa, wp, wm, wout, ROW_BLOCK)
    conv_prefix = jnp.pad(state_ffn_conv[0], ((0, 0), (0, DS - 2), (0, 0))).reshape(DB * DS, D_FF)
    ys, gate_s = _conv_ffn(x1s, norm2_g[0], wg, wu, ffn_conv_w[0], ffn_conv_b[0], wd, final_norm_g, ROW_BLOCK,
                           prefix=conv_prefix)

    heads = lambda a, n: a.reshape(1, n, -1, N_HEADS, V_DIM)
    new_pool_p = up.reshape(B, S, POOL_WIDTH)[:, S - POOL_STATE:][None]
    new_pool_s = jnp.concatenate([state_pool[0][:, DS:], seq3(us)], axis=1)[None]
    new_conv_p = tail_p[:, 6:8][None]
    new_conv_s = gate_s.reshape(DB, DS, D_FF)[:, DS - 2:][None]
    mem_heads = lambda a: a.reshape(1, B, N_MEM, N_HEADS, MEM_HEAD_DIM)
    return (yp.reshape(B, S, D), ys.reshape(DB, DS, D), heads(kp, B), heads(vp, B), heads(ks, DB), heads(vs, DB),
            new_pool_p, new_pool_s, new_conv_p, new_conv_s, mem_heads(mk_p), mem_heads(mv_p))
```

```python
import functools
import math

import jax
import jax.numpy as jnp
from jax import lax
from jax.experimental import pallas as pl
from jax.experimental.pallas import tpu as pltpu

F32 = jnp.float32
BF16 = jnp.bfloat16

D_MODEL = 1024
N_HEADS = 4
HEAD_DIM = 64
V_DIM = 2 * HEAD_DIM
ATTN_WIDTH = N_HEADS * V_DIM
POOL_WINDOWS = (2, 4, 8, 16)
POOL_GROUP_DIM = 64
POOL_WIDTH = 256
POOL_STATE = 15
MEM_HEAD_DIM = 64
MEM_WIDTH = 256
N_MEM = 256
D_FF = 2816
N_BUCKETS = 32
MAX_DISTANCE = 128
EPS = 1e-6
NEG_INF = -1e30
QK_SCALE = HEAD_DIM ** -0.5
MEM_SCALE = MEM_HEAD_DIM ** -0.5
LAM_INIT = 0.8 - 0.6 * math.exp(-0.3 * 0)
SUBLN_SCALE = 1.0 - LAM_INIT
LOG2E = math.log2(math.e)

ATTN_TQ = 256
ATTN_TK = 256
ROW_BLOCK = 512
FFN_CHUNK = 256
SAMPLE_KPAD = 128
PAGE_SLOTS = 3
VMEM_LIMIT = 56 * 1024 * 1024
HEAD_ROWS = -1


def _cparams(*sem):
    return pltpu.CompilerParams(dimension_semantics=sem, vmem_limit_bytes=VMEM_LIMIT)


def _rmsnorm(x, g):
    return x * lax.rsqrt(jnp.mean(x * x, axis=-1, keepdims=True) + EPS) * g


def _dot(a, b):
    return jnp.dot(a, b, preferred_element_type=F32)


def _dot_nt(a, b):
    return lax.dot_general(a, b, (((1,), (1,)), ((), ())), preferred_element_type=F32)


def _softmax_rows(s):
    m = jnp.max(s, axis=-1, keepdims=True)
    p = jnp.exp(s - m)
    return p * (1.0 / jnp.sum(p, axis=-1, keepdims=True))


def _lam_value(lq1, lk1, lq2, lk2):
    a = jnp.sum(lq1 * lk1, axis=-1, keepdims=True)
    b = jnp.sum(lq2 * lk2, axis=-1, keepdims=True)
    return jnp.exp(a) - jnp.exp(b) + LAM_INIT


def _norm_proj_kernel(x_ref, g_ref, w_ref, *out_refs, groups):
    h = _rmsnorm(x_ref[...], g_ref[...]).astype(BF16)
    k = 0
    for start, size, outs in groups:
        y = _dot(h, w_ref[:, start:start + size])
        for scale, key_block in outs:
            o_ref = out_refs[k]
            k += 1
            ys = y if scale == 1.0 else y * scale
            if key_block == HEAD_ROWS:
                n = ys.shape[0]
                for hd in range(N_HEADS):
                    o_ref[pl.ds(hd, n, stride=N_HEADS), :] = ys[:, hd * V_DIM:(hd + 1) * V_DIM].astype(o_ref.dtype)
            elif key_block:
                yt = ys.T
                for b in range(o_ref.shape[0]):
                    o_ref[b] = yt[:, b * key_block:(b + 1) * key_block].astype(o_ref.dtype)
            else:
                o_ref[...] = ys.astype(o_ref.dtype)


def _norm_proj(x2d, g, w_bf16, groups, out_dtypes, tm):
    rows, d = x2d.shape
    n_cols = w_bf16.shape[1]
    out_shape, out_specs = [], []
    flat_outs = [(size, kb) for _, size, outs in groups for _, kb in outs]
    for (size, kb), dt in zip(flat_outs, out_dtypes):
        if kb == HEAD_ROWS:
            out_shape.append(jax.ShapeDtypeStruct((rows * N_HEADS, V_DIM), dt))
            out_specs.append(pl.BlockSpec((tm * N_HEADS, V_DIM), lambda i: (i, 0)))
        elif kb:
            out_shape.append(jax.ShapeDtypeStruct((rows // kb, size, kb), dt))
            out_specs.append(pl.BlockSpec((tm // kb, size, kb), lambda i: (i, 0, 0)))
        else:
            out_shape.append(jax.ShapeDtypeStruct((rows, size), dt))
            out_specs.append(pl.BlockSpec((tm, size), lambda i: (i, 0)))
    return pl.pallas_call(
        functools.partial(_norm_proj_kernel, groups=groups),
        grid=(rows // tm,),
        in_specs=[pl.BlockSpec((tm, d), lambda i: (i, 0)),
                  pl.BlockSpec((1, d), lambda i: (0, 0)),
                  pl.BlockSpec((d, n_cols), lambda i: (0, 0), pipeline_mode=pl.Buffered(1))],
        out_specs=out_specs,
        out_shape=out_shape,
        compiler_params=_cparams("parallel"),
        name="norm_proj",
    )(x2d, g.reshape(1, d), w_bf16)


def _rel_bucket(rel):
    n = jnp.maximum(rel, 0)
    max_exact = N_BUCKETS // 2
    nf = jnp.maximum(n, 1).astype(F32)
    large = max_exact + jnp.floor(jnp.log(nf / max_exact) / math.log(MAX_DISTANCE / max_exact)
                                  * (N_BUCKETS - max_exact)).astype(jnp.int32)
    large = jnp.minimum(large, N_BUCKETS - 1)
    return jnp.where(n < max_exact, n, large)


def _bias_from_rel(rel, rb_ref, head, visible):
    bucket = _rel_bucket(rel)
    acc = jnp.zeros(rel.shape, F32)
    for b in range(N_BUCKETS):
        acc = jnp.where(bucket == b, rb_ref[b, head], acc)
    return jnp.where(visible, acc, NEG_INF)


def _bias_kernel(rb_ref, pb_ref, sb_ref, *, tq, tk, past, n_new):
    n_kinds = pb_ref.shape[1]
    key = lax.broadcasted_iota(jnp.int32, (tk, tq), 0)
    qry = lax.broadcasted_iota(jnp.int32, (tk, tq), 1)
    for kind in range(n_kinds):
        rel = kind * tq + qry - key
        for h in range(N_HEADS):
            if kind * tq - (tk - 1) >= MAX_DISTANCE:
                tile = jnp.full((tk, tq), rb_ref[N_BUCKETS - 1, h], F32) * LOG2E
            else:
                tile = _bias_from_rel(rel, rb_ref, h, rel >= 0) * LOG2E
            pb_ref[h, kind, :, :tq] = tile
            pb_ref[h, kind, :, tq:] = tile
    rows_per_head = 2 * n_new
    n_cols = sb_ref.shape[1]
    qi = lax.broadcasted_iota(jnp.int32, (rows_per_head, n_cols), 0) % n_new
    key = lax.broadcasted_iota(jnp.int32, (rows_per_head, n_cols), 1)
    rel = past + qi - key
    for h in range(N_HEADS):
        sb_ref[h * rows_per_head:(h + 1) * rows_per_head, :] = _bias_from_rel(rel, rb_ref, h, rel >= 0)


def _bias_tables(rel_bias, tq, tk, past, n_new):
    n_kinds = -(-(tk + MAX_DISTANCE - 1) // tq) + 1
    n_cols = past + SAMPLE_KPAD
    return pl.pallas_call(
        functools.partial(_bias_kernel, tq=tq, tk=tk, past=past, n_new=n_new),
        in_specs=[pl.BlockSpec(memory_space=pltpu.SMEM)],
        out_specs=[pl.BlockSpec(memory_space=pltpu.VMEM), pl.BlockSpec(memory_space=pltpu.VMEM)],
        out_shape=[jax.ShapeDtypeStruct((N_HEADS, n_kinds, tk, 2 * tq), F32),
                   jax.ShapeDtypeStruct((N_HEADS * 2 * n_new, n_cols), F32)],
        compiler_params=pltpu.CompilerParams(vmem_limit_bytes=VMEM_LIMIT),
        name="bias_tables",
    )(rel_bias)


def _subln(o, g):
    return _rmsnorm(o, g) * SUBLN_SCALE


def _two_map_queries(q_h):
    lane = lax.broadcasted_iota(jnp.int32, q_h.shape, 1)
    zero = jnp.zeros_like(q_h)
    return jnp.concatenate([jnp.where(lane < HEAD_DIM, q_h, zero), jnp.where(lane >= HEAD_DIM, q_h, zero)], axis=0)


def _prompt_attn_kernel(lq1_ref, lk1_ref, lq2_ref, lk2_ref, q_ref, k_ref, vt_ref, bias_ref, g_ref, o_ref,
                        q2_ref, m_ref, l_ref, acc_ref, s0_ref, *, tq, tk):
    i = pl.program_id(1)
    lam = _lam_value(lq1_ref[...], lk1_ref[...], lq2_ref[...], lk2_ref[...])
    n_kinds = bias_ref.shape[1]
    for h in range(N_HEADS):
        q2_ref[h] = _two_map_queries(q_ref[:, h * V_DIM:(h + 1) * V_DIM])
    m_ref[...] = jnp.full(m_ref.shape, NEG_INF, F32)
    l_ref[...] = jnp.zeros(l_ref.shape, F32)
    acc_ref[...] = jnp.zeros(acc_ref.shape, F32)

    last = (i * tq + tq - 1) // tk

    def scores(j, h):
        r0 = pl.multiple_of(j * tk, tk)
        kind = jnp.minimum((i * tq - j * tk) // tq, n_kinds - 1)
        hs = slice(h * V_DIM, (h + 1) * V_DIM)
        return _dot_nt(k_ref[pl.ds(r0, tk), hs], q2_ref[h]) + bias_ref[h, kind]

    ahead = 2
    for h in range(ahead):
        s0_ref[h] = scores(0, h)

    ones_rows = jnp.ones((16, tk), BF16)

    def body(j, carry):
        pending = [s0_ref[h] for h in range(ahead)]
        j_next = jnp.minimum(j + 1, last)
        for h in range(N_HEADS):
            hs = slice(h * V_DIM, (h + 1) * V_DIM)
            s = pending.pop(0)
            pending.append(scores(j, h + ahead) if h + ahead < N_HEADS else scores(j_next, h + ahead - N_HEADS))
            m_old = m_ref[h]
            m_new = jnp.maximum(m_old, jnp.max(s, axis=0, keepdims=True))
            alpha = jnp.exp2(m_old - m_new)
            p = jnp.exp2(s - m_new).astype(BF16)
            pv = _dot(jnp.concatenate([vt_ref[j, hs, :], ones_rows], axis=0), p)
            l_ref[h] = alpha * l_ref[h] + pv[V_DIM:V_DIM + 1]
            m_ref[h] = m_new
            acc_ref[h] = alpha * acc_ref[h] + pv[:V_DIM]
        for h in range(ahead):
            s0_ref[h] = pending[h]
        return carry

    lax.fori_loop(0, last + 1, body, 0)
    for h in range(N_HEADS):
        out_t = acc_ref[h] / l_ref[h]
        o_t = out_t[:, :tq] - lam * out_t[:, tq:]
        o_ref[:, h * V_DIM:(h + 1) * V_DIM] = _subln(o_t.T, g_ref[...]).astype(o_ref.dtype)


def _prompt_attn(lams, q, k, vt, bias, subln_g, batch, seq, tq, tk):
    vec = pl.BlockSpec((1, HEAD_DIM), lambda b, i: (0, 0))
    nq = seq // tq
    return pl.pallas_call(
        functools.partial(_prompt_attn_kernel, tq=tq, tk=tk),
        grid=(batch, nq),
        in_specs=[vec, vec, vec, vec,
                  pl.BlockSpec((tq, ATTN_WIDTH), lambda b, i: (b * nq + i, 0)),
                  pl.BlockSpec((seq, ATTN_WIDTH), lambda b, i: (b, 0)),
                  pl.BlockSpec((seq // tk, ATTN_WIDTH, tk), lambda b, i: (b, 0, 0)),
                  pl.BlockSpec(bias.shape, lambda b, i: (0, 0, 0, 0), pipeline_mode=pl.Buffered(1)),
                  pl.BlockSpec((1, V_DIM), lambda b, i: (0, 0))],
        out_specs=pl.BlockSpec((tq, ATTN_WIDTH), lambda b, i: (b * nq + i, 0)),
        out_shape=jax.ShapeDtypeStruct((batch * seq, ATTN_WIDTH), BF16),
        scratch_shapes=[pltpu.VMEM((N_HEADS, 2 * tq, V_DIM), BF16),
                        pltpu.VMEM((N_HEADS, 1, 2 * tq), F32),
                        pltpu.VMEM((N_HEADS, 1, 2 * tq), F32),
                        pltpu.VMEM((N_HEADS, V_DIM, 2 * tq), F32),
                        pltpu.VMEM((2, tk, 2 * tq), F32)],
        compiler_params=_cparams("parallel", "parallel"),
        name="prompt_attn",
    )(*lams, q, k, vt, bias, subln_g.reshape(1, V_DIM))


def _pool_branch(ext, u, pos0, wbd, scale):
    n, c = u.shape
    p = ext.shape[0] - n
    s2 = ext + pltpu.roll(ext, 1, 0)
    s4 = s2 + pltpu.roll(s2, 2, 0)
    s8 = s4 + pltpu.roll(s4, 4, 0)
    s16 = s8 + pltpu.roll(s8, 8, 0)
    lane = lax.broadcasted_iota(jnp.int32, (n, c), 1)
    grp = lane // POOL_GROUP_DIM
    win = jnp.where(grp == 0, s2[p:], jnp.where(grp == 1, s4[p:], jnp.where(grp == 2, s8[p:], s16[p:])))
    width = jnp.where(grp == 0, 2, jnp.where(grp == 1, 4, jnp.where(grp == 2, 8, 16)))
    pos = pos0 + lax.broadcasted_iota(jnp.int32, (n, c), 0)
    cnt = jnp.minimum(pos + 1, width).astype(F32)
    d = win / cnt - u
    return _dot(d.astype(BF16), wbd) * scale


def _prompt_side_kernel(u_ref, uprev_ref, qm_ref, mk_ref, mv_ref, wbd_ref, ps_ref, po_ref, om_ref, *, tm):
    i = pl.program_id(1)
    u = u_ref[...]
    prev = jnp.where(i > 0, uprev_ref[...], 0.0)
    ext = jnp.concatenate([prev, u], axis=0)
    po_ref[...] = _pool_branch(ext, u, i * tm, wbd_ref[...], ps_ref[...]).astype(po_ref.dtype)
    heads = [slice(h * MEM_HEAD_DIM, (h + 1) * MEM_HEAD_DIM) for h in range(N_HEADS)]
    scores = [_dot_nt(qm_ref[:, sl], mk_ref[:, sl].astype(BF16)) for sl in heads]
    probs = [_softmax_rows(s).astype(BF16) for s in scores]
    for sl, p in zip(heads, probs):
        om_ref[:, sl] = _dot(p, mv_ref[:, sl].astype(BF16)).astype(om_ref.dtype)


def _prompt_side(u, qm, mk, mv, wbd, pool_scale, batch, seq, tm):
    nblk = seq // tm
    halo = 16
    row = lambda b, i: (b * nblk + i, 0)
    return pl.pallas_call(
        functools.partial(_prompt_side_kernel, tm=tm),
        grid=(batch, nblk),
        in_specs=[pl.BlockSpec((tm, POOL_WIDTH), row),
                  pl.BlockSpec((halo, POOL_WIDTH),
                               lambda b, i: (jnp.maximum((b * nblk + i) * (tm // halo) - 1, 0), 0)),
                  pl.BlockSpec((tm, MEM_WIDTH), row),
                  pl.BlockSpec((N_MEM, MEM_WIDTH), lambda b, i: (b, 0)),
                  pl.BlockSpec((N_MEM, MEM_WIDTH), lambda b, i: (b, 0)),
                  pl.BlockSpec((POOL_WIDTH, POOL_WIDTH), lambda b, i: (0, 0)),
                  pl.BlockSpec((1, POOL_WIDTH), lambda b, i: (0, 0))],
        out_specs=[pl.BlockSpec((tm, POOL_WIDTH), row), pl.BlockSpec((tm, MEM_WIDTH), row)],
        out_shape=[jax.ShapeDtypeStruct((batch * seq, POOL_WIDTH), BF16),
                   jax.ShapeDtypeStruct((batch * seq, MEM_WIDTH), BF16)],
        compiler_params=_cparams("parallel", "parallel"),
        name="prompt_side",
    )(u, u, qm, mk, mv, wbd, pool_scale.reshape(1, POOL_WIDTH))


def _sample_step_kernel(pt_ref, lq1_ref, lk1_ref, lq2_ref, lk2_ref, q_ref, kn_ref, vn_ref, bias_ref, g_ref,
                        qm_ref, mkt_ref, mvt_ref, u_ref, st_ref, wbd_ref, ps_ref, ck_hbm, cv_hbm,
                        o_ref, om_ref, po_ref, s_ref, kbuf, vbuf, sem, *, n_pages, n_new, past):
    seq = pl.program_id(0)
    slot = seq % PAGE_SLOTS
    lam = _lam_value(lq1_ref[...], lk1_ref[...], lq2_ref[...], lk2_ref[...])
    page = kbuf.shape[2] // N_HEADS
    n_tail = s_ref.shape[1] - n_pages * page
    pad = jnp.zeros((n_tail - n_new, V_DIM), F32)
    hr = 2 * n_new
    caches = ((ck_hbm, kbuf), (cv_hbm, vbuf))

    def page_copy(which, slot_, p, page_id):
        hbm, buf = caches[which]
        return pltpu.make_async_copy(hbm.at[page_id], buf.at[slot_, p], sem.at[slot_, which])

    def start_pages(seq_, slot_):
        for p in range(n_pages):
            page_id = pt_ref[seq_, p]
            page_copy(0, slot_, p, page_id).start()
            page_copy(1, slot_, p, page_id).start()

    def wait_pages(which):
        for p in range(n_pages):
            page_copy(which, slot, p, 0).wait()

    ahead = PAGE_SLOTS - 1

    @pl.when(seq == 0)
    def _():
        for s0 in range(ahead):
            start_pages(s0, s0)

    @pl.when(seq + ahead < pl.num_programs(0))
    def _():
        start_pages(seq + ahead, (seq + ahead) % PAGE_SLOTS)

    def page_pair(buf, c, h):
        rows = pl.ds(h, page, stride=N_HEADS)
        return jnp.concatenate([buf[slot, c, rows, :], buf[slot, c + 1, rows, :]], axis=0).astype(BF16)

    def head_rows(ref, h):
        return ref[pl.ds(h, n_new, stride=N_HEADS), :]

    wait_pages(0)
    for h in range(N_HEADS):
        q2 = _two_map_queries(q_ref[:, h * V_DIM:(h + 1) * V_DIM]).astype(BF16)
        for c in range(0, n_pages, 2):
            s_ref[h * hr:(h + 1) * hr, c * page:(c + 2) * page] = _dot_nt(q2, page_pair(kbuf, c, h))
        k_tail = jnp.concatenate([head_rows(kn_ref, h), pad], axis=0).astype(BF16)
        s_ref[h * hr:(h + 1) * hr, n_pages * page:] = _dot_nt(q2, k_tail)
    prob = _softmax_rows(s_ref[...] + bias_ref[...])
    wait_pages(1)
    for h in range(N_HEADS):
        a = prob[h * hr:h * hr + n_new] - lam * prob[h * hr + n_new:(h + 1) * hr]
        v_tail = jnp.concatenate([head_rows(vn_ref, h), pad], axis=0).astype(BF16)
        o_h = _dot(a[:, n_pages * page:].astype(BF16), v_tail)
        for c in range(0, n_pages, 2):
            o_h = o_h + _dot(a[:, c * page:(c + 2) * page].astype(BF16), page_pair(vbuf, c, h))
        o_ref[:, h * V_DIM:(h + 1) * V_DIM] = _subln(o_h, g_ref[...]).astype(o_ref.dtype)
    qm = qm_ref[...]
    head_of_lane = lax.broadcasted_iota(jnp.int32, qm.shape, 1) // MEM_HEAD_DIM
    qm4 = jnp.concatenate([jnp.where(head_of_lane == h, qm, 0.0) for h in range(N_HEADS)], axis=0).astype(BF16)
    pm = _softmax_rows(_dot(qm4, mkt_ref[...].astype(BF16)))
    om_all = _dot_nt(pm.astype(BF16), mvt_ref[...].astype(BF16))
    om = jnp.zeros(qm.shape, F32)
    for h in range(N_HEADS):
        om = jnp.where(head_of_lane == h, om_all[h * n_new:(h + 1) * n_new], om)
    om_ref[...] = om.astype(om_ref.dtype)
    u = u_ref[...]
    ext = jnp.concatenate([st_ref[...], u], axis=0)
    po_ref[...] = _pool_branch(ext, u, past, wbd_ref[...], ps_ref[...]).astype(po_ref.dtype)


def _sample_step(page_table, lams, q, k_new, v_new, bias, subln_g, qm, mem_kt, mem_vt, u, state16, wbd, pool_scale,
                 cache_k, cache_v):
    n_seq, n_pages = page_table.shape
    n_new = q.shape[1]
    rows = cache_k.shape[1]
    past = n_pages * rows // N_HEADS
    assert n_pages % 2 == 0 and n_seq >= PAGE_SLOTS
    vec = pl.BlockSpec((1, HEAD_DIM), lambda s, pt: (0, 0))
    per_seq = lambda a: pl.BlockSpec((None,) + a.shape[1:], lambda s, pt: (s,) + (0,) * (a.ndim - 1))
    const = lambda a: pl.BlockSpec(a.shape, lambda s, pt: (0,) * a.ndim)
    hbm = pl.BlockSpec(memory_space=pl.ANY)
    g = subln_g.reshape(1, V_DIM)
    ps = pool_scale.reshape(1, POOL_WIDTH)
    in_specs = [vec, vec, vec, vec, per_seq(q), per_seq(k_new), per_seq(v_new), const(bias), const(g),
                per_seq(qm), per_seq(mem_kt), per_seq(mem_vt), per_seq(u), per_seq(state16), const(wbd), const(ps),
                hbm, hbm]
    out3 = lambda width: jax.ShapeDtypeStruct((n_seq, n_new, width), F32)
    out_spec = lambda width: pl.BlockSpec((None, n_new, width), lambda s, pt: (s, 0, 0))
    page_buffers = pltpu.VMEM((PAGE_SLOTS, n_pages, rows, V_DIM), F32)
    grid_spec = pltpu.PrefetchScalarGridSpec(
        num_scalar_prefetch=1,
        grid=(n_seq,),
        in_specs=in_specs,
        out_specs=[out_spec(ATTN_WIDTH), out_spec(MEM_WIDTH), out_spec(POOL_WIDTH)],
        scratch_shapes=[pltpu.VMEM(bias.shape, F32), page_buffers, page_buffers,
                        pltpu.SemaphoreType.DMA((PAGE_SLOTS, 2))],
    )
    return pl.pallas_call(
        functools.partial(_sample_step_kernel, n_pages=n_pages, n_new=n_new, past=past),
        grid_spec=grid_spec,
        out_shape=[out3(ATTN_WIDTH), out3(MEM_WIDTH), out3(POOL_WIDTH)],
        compiler_params=_cparams("arbitrary"),
        name="sample_step",
    )(page_table, *lams, q, k_new, v_new, bias, g, qm, mem_kt, mem_vt, u, state16, wbd, ps, cache_k, cache_v)


def _merge_kernel(x_ref, o_ref, po_ref, om_ref, g_ref, wgl_ref, wa_ref, wp_ref, wm_ref, wout_ref, x1_ref):
    x = x_ref[...]
    h = _rmsnorm(x, g_ref[...]).astype(BF16)
    d = D_MODEL
    merged = jax.nn.sigmoid(_dot(h, wgl_ref[:, 0:d])) * _dot(o_ref[...].astype(BF16), wa_ref[...])
    merged = merged + jax.nn.sigmoid(_dot(h, wgl_ref[:, d:2 * d])) * _dot(po_ref[...].astype(BF16), wp_ref[...])
    merged = merged + jax.nn.sigmoid(_dot(h, wgl_ref[:, 2 * d:3 * d])) * _dot(om_ref[...].astype(BF16), wm_ref[...])
    x1_ref[...] = x + _dot(merged.astype(BF16), wout_ref[...])


def _merge(x2d, o, po, om, norm_g, wgl, wa, wp, wm, wout, tm):
    rows, d = x2d.shape
    row = lambda width: pl.BlockSpec((tm, width), lambda i: (i, 0))
    full = lambda a: pl.BlockSpec(a.shape, lambda i: (0, 0), pipeline_mode=pl.Buffered(1))
    g = norm_g.reshape(1, d)
    return pl.pallas_call(
        _merge_kernel,
        grid=(rows // tm,),
        in_specs=[row(d), row(ATTN_WIDTH), row(POOL_WIDTH), row(MEM_WIDTH), full(g), full(wgl), full(wa), full(wp),
                  full(wm), full(wout)],
        out_specs=row(d),
        out_shape=jax.ShapeDtypeStruct((rows, d), F32),
        compiler_params=_cparams("parallel"),
        name="merge",
    )(x2d, o, po, om, g, wgl, wa, wp, wm, wout)


def _conv_ffn_kernel(x_ref, g2_ref, wg_ref, wu_ref, cw_ref, cb_ref, wd_ref, gf_ref, *refs, tm, fc, seq_rows):
    if seq_rows:
        pre_ref, y_ref, gate_ref, act_ref = refs
    else:
        y_ref, tail_ref, carry_ref, act_ref = refs

        @pl.when(pl.program_id(1) == 0)
        def _():
            carry_ref[...] = jnp.zeros_like(carry_ref)

    x = x_ref[...]
    h = _rmsnorm(x, g2_ref[...]).astype(BF16)
    for c0 in range(0, D_FF, fc):
        cs = slice(c0, c0 + fc)
        gate = _dot(h, wg_ref[:, cs])
        up = _dot(h, wu_ref[:, cs])
        if seq_rows:
            pre = pre_ref[:, cs]
            t = lax.broadcasted_iota(jnp.int32, (tm, fc), 0) % seq_rows
            back1 = jnp.where(t >= 1, pltpu.roll(gate, 1, 0), pltpu.roll(pre, tm - 1, 0))
            back2 = jnp.where(t >= 2, pltpu.roll(gate, 2, 0), pre)
            gate_ref[:, cs] = gate
        else:
            ext = jnp.concatenate([carry_ref[:, cs], gate], axis=0)
            back1 = pltpu.roll(ext, 1, 0)[8:]
            back2 = pltpu.roll(ext, 2, 0)[8:]
            carry_ref[:, cs] = gate[tm - 8:]
        gc = cb_ref[:, cs] + cw_ref[0:1, cs] * back2
        gc = gc + cw_ref[1:2, cs] * back1
        gc = gc + cw_ref[2:3, cs] * gate
        act_ref[:, cs] = (jax.nn.gelu(gc) * up).astype(BF16)
    if not seq_rows:
        tail_ref[...] = carry_ref[...]
    y_ref[...] = _rmsnorm(x + _dot(act_ref[...], wd_ref[...]), gf_ref[...])


def _conv_ffn(x1, norm_g, wg, wu, conv_w, conv_b, wd, final_g, tm, batch=None, seq=None, prefix=None):
    rows, d = x1.shape
    g2 = norm_g.reshape(1, d)
    gf = final_g.reshape(1, d)
    cb = conv_b.reshape(1, D_FF)
    if prefix is None:
        nblk = seq // tm
        row = lambda width: pl.BlockSpec((tm, width), lambda b, i: (b * nblk + i, 0))
        full = lambda a: pl.BlockSpec(a.shape, lambda b, i: (0, 0), pipeline_mode=pl.Buffered(1))
        return pl.pallas_call(
            functools.partial(_conv_ffn_kernel, tm=tm, fc=FFN_CHUNK, seq_rows=0),
            grid=(batch, nblk),
            in_specs=[row(d), full(g2), full(wg), full(wu), full(conv_w), full(cb), full(wd), full(gf)],
            out_specs=[row(d), pl.BlockSpec((None, 8, D_FF), lambda b, i: (b, 0, 0))],
            out_shape=[jax.ShapeDtypeStruct((rows, d), F32), jax.ShapeDtypeStruct((batch, 8, D_FF), F32)],
            scratch_shapes=[pltpu.VMEM((8, D_FF), F32), pltpu.VMEM((tm, D_FF), BF16)],
            compiler_params=_cparams("parallel", "arbitrary"),
            name="conv_ffn_prompt",
        )(x1, g2, wg, wu, conv_w, cb, wd, gf)
    row = lambda width: pl.BlockSpec((tm, width), lambda i: (i, 0))
    full = lambda a: pl.BlockSpec(a.shape, lambda i: (0, 0), pipeline_mode=pl.Buffered(1))
    return pl.pallas_call(
        functools.partial(_conv_ffn_kernel, tm=tm, fc=FFN_CHUNK, seq_rows=8),
        grid=(rows // tm,),
        in_specs=[row(d), full(g2), full(wg), full(wu), full(conv_w), full(cb), full(wd), full(gf), row(D_FF)],
        out_specs=[row(d), row(D_FF)],
        out_shape=[jax.ShapeDtypeStruct((rows, d), F32), jax.ShapeDtypeStruct((rows, D_FF), F32)],
        scratch_shapes=[pltpu.VMEM((tm, D_FF), BF16)],
        compiler_params=_cparams("parallel"),
        name="conv_ffn_sample",
    )(x1, g2, wg, wu, conv_w, cb, wd, gf, prefix)


def kernel(x_prompt, x_sample, mem_prompt, cache_k, cache_v, page_table, state_pool, state_ffn_conv, cache_mem_k, cache_mem_v, norm1_g, w_in, lam_q1, lam_k1, lam_q2, lam_k2, subln_g, w_pool_grp, pool_scale, w_br_attn, w_br_pool, w_br_mem, mem_norm_g, w_mem_kv, w_out, norm2_g, w_ffn_gate, w_ffn_up, ffn_conv_w, ffn_conv_b, w_ffn_down, rel_bias, final_norm_g):
    depth = w_in.shape[0]
    assert depth == 1, "single-layer step only"
    B, S, D = x_prompt.shape
    DB, DS, _ = x_sample.shape
    n_phys, page = cache_k.shape[1], cache_k.shape[2]
    n_pages = page_table.shape[1]
    past = n_pages * page
    assert DS == 8 and S % ROW_BLOCK == 0 and (DB * DS) % ROW_BLOCK == 0
    assert ATTN_TK % ATTN_TQ == 0 and ROW_BLOCK % ATTN_TK == 0 and S % ATTN_TK == 0

    w_qkv = w_in[0, :, :2048].astype(BF16)
    w_gl = w_in[0, :, 2048:].astype(BF16)
    wa, wp, wm = w_br_attn[0].astype(BF16), w_br_pool[0].astype(BF16), w_br_mem[0].astype(BF16)
    wout = w_out[0].astype(BF16)
    wg, wu, wd = w_ffn_gate[0].astype(BF16), w_ffn_up[0].astype(BF16), w_ffn_down[0].astype(BF16)
    w_mem = w_mem_kv[0].astype(BF16)
    wbd = jnp.zeros((POOL_WIDTH, POOL_WIDTH), F32)
    for gi in range(len(POOL_WINDOWS)):
        sl = slice(gi * POOL_GROUP_DIM, (gi + 1) * POOL_GROUP_DIM)
        wbd = wbd.at[sl, sl].set(w_pool_grp[0, gi])
    wbd = wbd.astype(BF16)
    lams = tuple(a[0].reshape(1, HEAD_DIM) for a in (lam_q1, lam_k1, lam_q2, lam_k2))

    prompt_bias, sample_bias = _bias_tables(rel_bias, ATTN_TQ, ATTN_TK, past, DS)

    xp = x_prompt.reshape(B * S, D)
    groups = ((0, 512, ((QK_SCALE * LOG2E, 0),)), (512, 512, ((1.0, HEAD_ROWS), (1.0, 0))),
              (1024, 512, ((1.0, HEAD_ROWS), (1.0, ATTN_TK))), (1536, 256, ((1.0, 0),)), (1792, 256, ((MEM_SCALE, 0),)))
    qp, kp, kp_b, vp, vp_t, up, qmp = _norm_proj(xp, norm1_g[0], w_qkv, groups,
                                                 (BF16, F32, BF16, F32, BF16, F32, BF16), ROW_BLOCK)
    mk_p, mv_p = _norm_proj(mem_prompt.reshape(B * N_MEM, D), mem_norm_g[0], w_mem,
                            ((0, MEM_WIDTH, ((1.0, 0),)), (MEM_WIDTH, MEM_WIDTH, ((1.0, 0),))), (F32, F32), ROW_BLOCK)
    op = _prompt_attn(lams, qp, kp_b, vp_t, prompt_bias, subln_g[0], B, S, ATTN_TQ, ATTN_TK)
    pop, omp = _prompt_side(up, qmp, mk_p, mv_p, wbd, pool_scale[0], B, S, ROW_BLOCK)
    x1p = _merge(xp, op, pop, omp, norm1_g[0], w_gl, wa, wp, wm, wout, ROW_BLOCK)
    yp, tail_p = _conv_ffn(x1p, norm2_g[0], wg, wu, ffn_conv_w[0], ffn_conv_b[0], wd, final_norm_g, ROW_BLOCK,
                           batch=B, seq=S)

    xs = x_sample.reshape(DB * DS, D)
    groups_s = ((0, 512, ((QK_SCALE, 0),)), (512, 512, ((1.0, HEAD_ROWS),)), (1024, 512, ((1.0, HEAD_ROWS),)),
                (1536, 256, ((1.0, 0),)), (1792, 256, ((MEM_SCALE, 0),)))
    qs, ks, vs, us, qms = _norm_proj(xs, norm1_g[0], w_qkv, groups_s, (F32,) * 5, ROW_BLOCK)
    seq3 = lambda a: a.reshape(DB, DS, a.shape[-1])
    head_rows = lambda a: a.reshape(DB, DS * N_HEADS, V_DIM)
    state16 = jnp.pad(state_pool[0], ((0, 0), (16 - POOL_STATE, 0), (0, 0)))
    mem_t = lambda a: jnp.transpose(a[0], (0, 2, 3, 1)).reshape(DB, MEM_WIDTH, N_MEM)
    pages = lambda a: a[0].reshape(n_phys, page * N_HEADS, V_DIM)
    os_, oms, pos_ = _sample_step(page_table, lams, seq3(qs), head_rows(ks), head_rows(vs), sample_bias, subln_g[0],
                                  seq3(qms), mem_t(cache_mem_k), mem_t(cache_mem_v), seq3(us), state16, wbd,
                                  pool_scale[0], pages(cache_k), pages(cache_v))
    flat = lambda a: a.reshape(DB * DS, a.shape[-1])
    x1s = _merge(xs, flat(os_), flat(pos_), flat(oms), norm1_g[0], w_gl, wa, wp, wm, wout, ROW_BLOCK)
    conv_prefix = jnp.pad(state_ffn_conv[0], ((0, 0), (0, DS - 2), (0, 0))).reshape(DB * DS, D_FF)
    ys, gate_s = _conv_ffn(x1s, norm2_g[0], wg, wu, ffn_conv_w[0], ffn_conv_b[0], wd, final_norm_g, ROW_BLOCK,
                           prefix=conv_prefix)

    heads = lambda a, n: a.reshape(1, n, -1, N_HEADS, V_DIM)
    new_pool_p = up.reshape(B, S, POOL_WIDTH)[:, S - POOL_STATE:][None]
    new_pool_s = jnp.concatenate([state_pool[0][:, DS:], seq3(us)], axis=1)[None]
    new_conv_p = tail_p[:, 6:8][None]
    new_conv_s = gate_s.reshape(DB, DS, D_FF)[:, DS - 2:][None]
    mem_heads = lambda a: a.reshape(1, B, N_MEM, N_HEADS, MEM_HEAD_DIM)
    return (yp.reshape(B, S, D), ys.reshape(DB, DS, D), heads(kp, B), heads(vp, B), heads(ks, DB), heads(vs, DB),
            new_pool_p, new_pool_s, new_conv_p, new_conv_s, mem_heads(mk_p), mem_heads(mv_p))
```

```python
import functools
import math

import jax
import jax.numpy as jnp
from jax import lax
from jax.experimental import pallas as pl
from jax.experimental.pallas import tpu as pltpu

F32 = jnp.float32
BF16 = jnp.bfloat16

D_MODEL = 1024
N_HEADS = 4
HEAD_DIM = 64
V_DIM = 2 * HEAD_DIM
ATTN_WIDTH = N_HEADS * V_DIM
POOL_WINDOWS = (2, 4, 8, 16)
POOL_GROUP_DIM = 64
POOL_WIDTH = 256
POOL_STATE = 15
MEM_HEAD_DIM = 64
MEM_WIDTH = 256
N_MEM = 256
D_FF = 2816
N_BUCKETS = 32
MAX_DISTANCE = 128
EPS = 1e-6
NEG_INF = -1e30
QK_SCALE = HEAD_DIM ** -0.5
MEM_SCALE = MEM_HEAD_DIM ** -0.5
LAM_INIT = 0.8 - 0.6 * math.exp(-0.3 * 0)
SUBLN_SCALE = 1.0 - LAM_INIT
LOG2E = math.log2(math.e)

ATTN_TQ = 256
ATTN_TK = 256
ROW_BLOCK = 512
FFN_CHUNK = 256
SAMPLE_KPAD = 128
PAGE_SLOTS = 3
VMEM_LIMIT = 56 * 1024 * 1024
HEAD_ROWS = -1
LANES = 128


def _cparams(*sem):
    return pltpu.CompilerParams(dimension_semantics=sem, vmem_limit_bytes=VMEM_LIMIT)


def _rmsnorm(x, g):
    return x * lax.rsqrt(jnp.mean(x * x, axis=-1, keepdims=True) + EPS) * g


def _dot(a, b):
    return jnp.dot(a, b, preferred_element_type=F32)


def _dot_nt(a, b):
    return lax.dot_general(a, b, (((1,), (1,)), ((), ())), preferred_element_type=F32)


def _softmax_rows(s):
    m = jnp.max(s, axis=-1, keepdims=True)
    p = jnp.exp(s - m)
    return p * (1.0 / jnp.sum(p, axis=-1, keepdims=True))


def _lam_value(lq1, lk1, lq2, lk2):
    a = jnp.sum(lq1 * lk1, axis=-1, keepdims=True)
    b = jnp.sum(lq2 * lk2, axis=-1, keepdims=True)
    return jnp.exp(a) - jnp.exp(b) + LAM_INIT


def _norm_proj_kernel(x_ref, g_ref, w_ref, *out_refs, groups):
    h = _rmsnorm(x_ref[...], g_ref[...]).astype(BF16)
    k = 0
    for start, size, outs in groups:
        y = _dot(h, w_ref[:, start:start + size])
        for scale, key_block in outs:
            o_ref = out_refs[k]
            k += 1
            ys = y if scale == 1.0 else y * scale
            if key_block == HEAD_ROWS:
                n = ys.shape[0]
                for hd in range(N_HEADS):
                    o_ref[pl.ds(hd, n, stride=N_HEADS), :] = ys[:, hd * V_DIM:(hd + 1) * V_DIM].astype(o_ref.dtype)
            elif key_block:
                yt = ys.T
                for b in range(o_ref.shape[0]):
                    o_ref[b] = yt[:, b * key_block:(b + 1) * key_block].astype(o_ref.dtype)
            else:
                o_ref[...] = ys.astype(o_ref.dtype)


def _norm_proj(x2d, g, w_bf16, groups, out_dtypes, tm):
    rows, d = x2d.shape
    n_cols = w_bf16.shape[1]
    out_shape, out_specs = [], []
    flat_outs = [(size, kb) for _, size, outs in groups for _, kb in outs]
    for (size, kb), dt in zip(flat_outs, out_dtypes):
        if kb == HEAD_ROWS:
            out_shape.append(jax.ShapeDtypeStruct((rows * N_HEADS, V_DIM), dt))
            out_specs.append(pl.BlockSpec((tm * N_HEADS, V_DIM), lambda i: (i, 0)))
        elif kb:
            out_shape.append(jax.ShapeDtypeStruct((rows // kb, size, kb), dt))
            out_specs.append(pl.BlockSpec((tm // kb, size, kb), lambda i: (i, 0, 0)))
        else:
            out_shape.append(jax.ShapeDtypeStruct((rows, size), dt))
            out_specs.append(pl.BlockSpec((tm, size), lambda i: (i, 0)))
    return pl.pallas_call(
        functools.partial(_norm_proj_kernel, groups=groups),
        grid=(rows // tm,),
        in_specs=[pl.BlockSpec((tm, d), lambda i: (i, 0)),
                  pl.BlockSpec((1, d), lambda i: (0, 0)),
                  pl.BlockSpec((d, n_cols), lambda i: (0, 0), pipeline_mode=pl.Buffered(1))],
        out_specs=out_specs,
        out_shape=out_shape,
        compiler_params=_cparams("parallel"),
        name="norm_proj",
    )(x2d, g.reshape(1, d), w_bf16)


def _rel_bucket(rel):
    n = jnp.maximum(rel, 0)
    max_exact = N_BUCKETS // 2
    nf = jnp.maximum(n, 1).astype(F32)
    large = max_exact + jnp.floor(jnp.log(nf / max_exact) / math.log(MAX_DISTANCE / max_exact)
                                  * (N_BUCKETS - max_exact)).astype(jnp.int32)
    large = jnp.minimum(large, N_BUCKETS - 1)
    return jnp.where(n < max_exact, n, large)


def _bias_from_rel(rel, rb_ref, head, visible):
    bucket = _rel_bucket(rel)
    acc = jnp.zeros(rel.shape, F32)
    for b in range(N_BUCKETS):
        acc = jnp.where(bucket == b, rb_ref[b, head], acc)
    return jnp.where(visible, acc, NEG_INF)


def _bias_kernel(rb_ref, pb_ref, sb_ref, *, tq, tk, past, n_new):
    n_kinds = pb_ref.shape[1]
    key = lax.broadcasted_iota(jnp.int32, (tk, tq), 0)
    qry = lax.broadcasted_iota(jnp.int32, (tk, tq), 1)
    for kind in range(n_kinds):
        rel = kind * tq + qry - key
        for h in range(N_HEADS):
            if kind * tq - (tk - 1) >= MAX_DISTANCE:
                tile = jnp.full((tk, tq), rb_ref[N_BUCKETS - 1, h], F32) * LOG2E
            else:
                tile = _bias_from_rel(rel, rb_ref, h, rel >= 0) * LOG2E
            pb_ref[h, kind, :, :tq] = tile
            pb_ref[h, kind, :, tq:] = tile
    rows_per_head = 2 * n_new
    n_cols = sb_ref.shape[1]
    qi = lax.broadcasted_iota(jnp.int32, (rows_per_head, n_cols), 0) % n_new
    key = lax.broadcasted_iota(jnp.int32, (rows_per_head, n_cols), 1)
    rel = past + qi - key
    for h in range(N_HEADS):
        sb_ref[h * rows_per_head:(h + 1) * rows_per_head, :] = _bias_from_rel(rel, rb_ref, h, rel >= 0)


def _bias_tables(rel_bias, tq, tk, past, n_new):
    n_kinds = -(-(tk + MAX_DISTANCE - 1) // tq) + 1
    n_cols = past + SAMPLE_KPAD
    return pl.pallas_call(
        functools.partial(_bias_kernel, tq=tq, tk=tk, past=past, n_new=n_new),
        in_specs=[pl.BlockSpec(memory_space=pltpu.SMEM)],
        out_specs=[pl.BlockSpec(memory_space=pltpu.VMEM), pl.BlockSpec(memory_space=pltpu.VMEM)],
        out_shape=[jax.ShapeDtypeStruct((N_HEADS, n_kinds, tk, 2 * tq), F32),
                   jax.ShapeDtypeStruct((N_HEADS * 2 * n_new, n_cols), F32)],
        compiler_params=pltpu.CompilerParams(vmem_limit_bytes=VMEM_LIMIT),
        name="bias_tables",
    )(rel_bias)


def _subln(o, g):
    return _rmsnorm(o, g) * SUBLN_SCALE


def _two_map_queries(q_h):
    lane = lax.broadcasted_iota(jnp.int32, q_h.shape, 1)
    zero = jnp.zeros_like(q_h)
    return jnp.concatenate([jnp.where(lane < HEAD_DIM, q_h, zero), jnp.where(lane >= HEAD_DIM, q_h, zero)], axis=0)


def _prompt_attn_kernel(lq1_ref, lk1_ref, lq2_ref, lk2_ref, q_ref, k_ref, vt_ref, bias_ref, g_ref, o_ref,
                        q2_ref, m_ref, l_ref, acc_ref, s0_ref, *, tq, tk):
    i = pl.program_id(1)
    lam = _lam_value(lq1_ref[...], lk1_ref[...], lq2_ref[...], lk2_ref[...])
    n_kinds = bias_ref.shape[1]
    for h in range(N_HEADS):
        q2_ref[h] = _two_map_queries(q_ref[:, h * V_DIM:(h + 1) * V_DIM])
    m_ref[...] = jnp.full(m_ref.shape, NEG_INF, F32)
    l_ref[...] = jnp.zeros(l_ref.shape, F32)
    acc_ref[...] = jnp.zeros(acc_ref.shape, F32)

    last = (i * tq + tq - 1) // tk

    def scores(j, h):
        r0 = pl.multiple_of(j * tk, tk)
        kind = jnp.minimum((i * tq - j * tk) // tq, n_kinds - 1)
        hs = slice(h * V_DIM, (h + 1) * V_DIM)
        return _dot_nt(k_ref[pl.ds(r0, tk), hs], q2_ref[h]) + bias_ref[h, kind]

    ahead = 2
    for h in range(ahead):
        s0_ref[h] = scores(0, h)

    ones_rows = jnp.ones((16, tk), BF16)

    def body(j, carry):
        pending = [s0_ref[h] for h in range(ahead)]
        j_next = jnp.minimum(j + 1, last)
        for h in range(N_HEADS):
            hs = slice(h * V_DIM, (h + 1) * V_DIM)
            s = pending.pop(0)
            pending.append(scores(j, h + ahead) if h + ahead < N_HEADS else scores(j_next, h + ahead - N_HEADS))
            m_old = m_ref[h]
            m_new = jnp.maximum(m_old, jnp.max(s, axis=0, keepdims=True))
            alpha = jnp.exp2(m_old - m_new)
            p = jnp.exp2(s - m_new).astype(BF16)
            pv = _dot(jnp.concatenate([vt_ref[j, hs, :], ones_rows], axis=0), p)
            l_ref[h] = alpha * l_ref[h] + pv[V_DIM:V_DIM + 1]
            m_ref[h] = m_new
            acc_ref[h] = alpha * acc_ref[h] + pv[:V_DIM]
        for h in range(ahead):
            s0_ref[h] = pending[h]
        return carry

    lax.fori_loop(0, last + 1, body, 0)
    for h in range(N_HEADS):
        out_t = acc_ref[h] / l_ref[h]
        o_t = out_t[:, :tq] - lam * out_t[:, tq:]
        o_ref[:, h * V_DIM:(h + 1) * V_DIM] = _subln(o_t.T, g_ref[...]).astype(o_ref.dtype)


def _prompt_attn(lams, q, k, vt, bias, subln_g, batch, seq, tq, tk):
    vec = pl.BlockSpec((1, HEAD_DIM), lambda b, i: (0, 0))
    nq = seq // tq
    return pl.pallas_call(
        functools.partial(_prompt_attn_kernel, tq=tq, tk=tk),
        grid=(batch, nq),
        in_specs=[vec, vec, vec, vec,
                  pl.BlockSpec((tq, ATTN_WIDTH), lambda b, i: (b * nq + i, 0)),
                  pl.BlockSpec((seq, ATTN_WIDTH), lambda b, i: (b, 0)),
                  pl.BlockSpec((seq // tk, ATTN_WIDTH, tk), lambda b, i: (b, 0, 0)),
                  pl.BlockSpec(bias.shape, lambda b, i: (0, 0, 0, 0), pipeline_mode=pl.Buffered(1)),
                  pl.BlockSpec((1, V_DIM), lambda b, i: (0, 0))],
        out_specs=pl.BlockSpec((tq, ATTN_WIDTH), lambda b, i: (b * nq + i, 0)),
        out_shape=jax.ShapeDtypeStruct((batch * seq, ATTN_WIDTH), BF16),
        scratch_shapes=[pltpu.VMEM((N_HEADS, 2 * tq, V_DIM), BF16),
                        pltpu.VMEM((N_HEADS, 1, 2 * tq), F32),
                        pltpu.VMEM((N_HEADS, 1, 2 * tq), F32),
                        pltpu.VMEM((N_HEADS, V_DIM, 2 * tq), F32),
                        pltpu.VMEM((2, tk, 2 * tq), F32)],
        compiler_params=_cparams("parallel", "parallel"),
        name="prompt_attn",
    )(*lams, q, k, vt, bias, subln_g.reshape(1, V_DIM))


def _pool_branch(ext, u, pos0, wbd, scale):
    n, c = u.shape
    p = ext.shape[0] - n
    s2 = ext + pltpu.roll(ext, 1, 0)
    s4 = s2 + pltpu.roll(s2, 2, 0)
    s8 = s4 + pltpu.roll(s4, 4, 0)
    s16 = s8 + pltpu.roll(s8, 8, 0)
    lane = lax.broadcasted_iota(jnp.int32, (n, c), 1)
    grp = lane // POOL_GROUP_DIM
    win = jnp.where(grp == 0, s2[p:], jnp.where(grp == 1, s4[p:], jnp.where(grp == 2, s8[p:], s16[p:])))
    width = jnp.where(grp == 0, 2, jnp.where(grp == 1, 4, jnp.where(grp == 2, 8, 16)))
    pos = pos0 + lax.broadcasted_iota(jnp.int32, (n, c), 0)
    cnt = jnp.minimum(pos + 1, width).astype(F32)
    d = win / cnt - u
    return _dot(d.astype(BF16), wbd) * scale


def _prompt_side_kernel(u_ref, uprev_ref, qm_ref, mkt_ref, mvt_ref, wbd_ref, ps_ref, po_ref, om_ref, *, tm):
    i = pl.program_id(1)
    u = u_ref[...]
    prev = jnp.where(i > 0, uprev_ref[...], 0.0)
    ext = jnp.concatenate([prev, u], axis=0)
    po_ref[...] = _pool_branch(ext, u, i * tm, wbd_ref[...], ps_ref[...]).astype(po_ref.dtype)
    heads = [slice(h * MEM_HEAD_DIM, (h + 1) * MEM_HEAD_DIM) for h in range(N_HEADS)]
    scores = [_dot(qm_ref[:, sl], mkt_ref[sl, :].astype(BF16)) for sl in heads]
    probs = [_softmax_rows(s).astype(BF16) for s in scores]
    for sl, p in zip(heads, probs):
        om_ref[:, sl] = _dot_nt(p, mvt_ref[sl, :].astype(BF16)).astype(om_ref.dtype)


def _prompt_side(u, qm, mk, mv, wbd, pool_scale, batch, seq, tm):
    nblk = seq // tm
    halo = 16
    row = lambda b, i: (b * nblk + i, 0)
    return pl.pallas_call(
        functools.partial(_prompt_side_kernel, tm=tm),
        grid=(batch, nblk),
        in_specs=[pl.BlockSpec((tm, POOL_WIDTH), row),
                  pl.BlockSpec((halo, POOL_WIDTH),
                               lambda b, i: (jnp.maximum((b * nblk + i) * (tm // halo) - 1, 0), 0)),
                  pl.BlockSpec((tm, MEM_WIDTH), row),
                  pl.BlockSpec((None, MEM_WIDTH, N_MEM), lambda b, i: (b, 0, 0)),
                  pl.BlockSpec((None, MEM_WIDTH, N_MEM), lambda b, i: (b, 0, 0)),
                  pl.BlockSpec((POOL_WIDTH, POOL_WIDTH), lambda b, i: (0, 0)),
                  pl.BlockSpec((1, POOL_WIDTH), lambda b, i: (0, 0))],
        out_specs=[pl.BlockSpec((tm, POOL_WIDTH), row), pl.BlockSpec((tm, MEM_WIDTH), row)],
        out_shape=[jax.ShapeDtypeStruct((batch * seq, POOL_WIDTH), BF16),
                   jax.ShapeDtypeStruct((batch * seq, MEM_WIDTH), BF16)],
        compiler_params=_cparams("parallel", "parallel"),
        name="prompt_side",
    )(u, u, qm, mk, mv, wbd, pool_scale.reshape(1, POOL_WIDTH))


def _sample_step_kernel(pt_ref, lq1_ref, lk1_ref, lq2_ref, lk2_ref, q_ref, kn_ref, vn_ref, bias_ref, g_ref,
                        qm_ref, mkt_ref, mvt_ref, u_ref, st_ref, wbd_ref, ps_ref, ck_hbm, cv_hbm,
                        o_ref, om_ref, po_ref, s_ref, kbuf, vbuf, sem, *, n_pages, n_new, past):
    seq = pl.program_id(0)
    slot = seq % PAGE_SLOTS
    lam = _lam_value(lq1_ref[...], lk1_ref[...], lq2_ref[...], lk2_ref[...])
    page = kbuf.shape[2] // N_HEADS
    n_tail = s_ref.shape[1] - n_pages * page
    pad = jnp.zeros((n_tail - n_new, V_DIM), F32)
    hr = 2 * n_new
    caches = ((ck_hbm, kbuf), (cv_hbm, vbuf))

    def page_copy(which, slot_, p, page_id):
        hbm, buf = caches[which]
        return pltpu.make_async_copy(hbm.at[page_id], buf.at[slot_, p], sem.at[slot_, which])

    def start_pages(seq_, slot_):
        for p in range(n_pages):
            page_id = pt_ref[seq_, p]
            page_copy(0, slot_, p, page_id).start()
            page_copy(1, slot_, p, page_id).start()

    def wait_pages(which):
        for p in range(n_pages):
            page_copy(which, slot, p, 0).wait()

    ahead = PAGE_SLOTS - 1

    @pl.when(seq == 0)
    def _():
        for s0 in range(ahead):
            start_pages(s0, s0)

    @pl.when(seq + ahead < pl.num_programs(0))
    def _():
        start_pages(seq + ahead, (seq + ahead) % PAGE_SLOTS)

    def page_pair(buf, c, h):
        rows = pl.ds(h, page, stride=N_HEADS)
        return jnp.concatenate([buf[slot, c, rows, :], buf[slot, c + 1, rows, :]], axis=0).astype(BF16)

    def head_rows(ref, h):
        return ref[pl.ds(h, n_new, stride=N_HEADS), :]

    wait_pages(0)
    for h in range(N_HEADS):
        q2 = _two_map_queries(q_ref[:, h * V_DIM:(h + 1) * V_DIM]).astype(BF16)
        for c in range(0, n_pages, 2):
            s_ref[h * hr:(h + 1) * hr, c * page:(c + 2) * page] = _dot_nt(q2, page_pair(kbuf, c, h))
        k_tail = jnp.concatenate([head_rows(kn_ref, h), pad], axis=0).astype(BF16)
        s_ref[h * hr:(h + 1) * hr, n_pages * page:] = _dot_nt(q2, k_tail)
    prob = _softmax_rows(s_ref[...] + bias_ref[...])
    wait_pages(1)
    for h in range(N_HEADS):
        a = prob[h * hr:h * hr + n_new] - lam * prob[h * hr + n_new:(h + 1) * hr]
        v_tail = jnp.concatenate([head_rows(vn_ref, h), pad], axis=0).astype(BF16)
        o_h = _dot(a[:, n_pages * page:].astype(BF16), v_tail)
        for c in range(0, n_pages, 2):
            o_h = o_h + _dot(a[:, c * page:(c + 2) * page].astype(BF16), page_pair(vbuf, c, h))
        o_ref[:, h * V_DIM:(h + 1) * V_DIM] = _subln(o_h, g_ref[...]).astype(o_ref.dtype)
    qm = qm_ref[...]
    head_of_lane = lax.broadcasted_iota(jnp.int32, qm.shape, 1) // MEM_HEAD_DIM
    qm4 = jnp.concatenate([jnp.where(head_of_lane == h, qm, 0.0) for h in range(N_HEADS)], axis=0).astype(BF16)
    pm = _softmax_rows(_dot(qm4, mkt_ref[...].astype(BF16)))
    om_all = _dot_nt(pm.astype(BF16), mvt_ref[...].astype(BF16))
    om = jnp.zeros(qm.shape, F32)
    for h in range(N_HEADS):
        om = jnp.where(head_of_lane == h, om_all[h * n_new:(h + 1) * n_new], om)
    om_ref[...] = om.astype(om_ref.dtype)
    u = u_ref[...]
    ext = jnp.concatenate([st_ref[...], u], axis=0)
    po_ref[...] = _pool_branch(ext, u, past, wbd_ref[...], ps_ref[...]).astype(po_ref.dtype)


def _sample_step(page_table, lams, q, k_new, v_new, bias, subln_g, qm, mem_kt, mem_vt, u, state16, wbd, pool_scale,
                 cache_k, cache_v):
    n_seq, n_pages = page_table.shape
    n_new = q.shape[1]
    rows = cache_k.shape[1]
    past = n_pages * rows // N_HEADS
    assert n_pages % 2 == 0 and n_seq >= PAGE_SLOTS
    vec = pl.BlockSpec((1, HEAD_DIM), lambda s, pt: (0, 0))
    per_seq = lambda a: pl.BlockSpec((None,) + a.shape[1:], lambda s, pt: (s,) + (0,) * (a.ndim - 1))
    const = lambda a: pl.BlockSpec(a.shape, lambda s, pt: (0,) * a.ndim)
    hbm = pl.BlockSpec(memory_space=pl.ANY)
    g = subln_g.reshape(1, V_DIM)
    ps = pool_scale.reshape(1, POOL_WIDTH)
    in_specs = [vec, vec, vec, vec, per_seq(q), per_seq(k_new), per_seq(v_new), const(bias), const(g),
                per_seq(qm), per_seq(mem_kt), per_seq(mem_vt), per_seq(u), per_seq(state16), const(wbd), const(ps),
                hbm, hbm]
    out3 = lambda width: jax.ShapeDtypeStruct((n_seq, n_new, width), F32)
    out_spec = lambda width: pl.BlockSpec((None, n_new, width), lambda s, pt: (s, 0, 0))
    page_buffers = pltpu.VMEM((PAGE_SLOTS, n_pages, rows, V_DIM), F32)
    grid_spec = pltpu.PrefetchScalarGridSpec(
        num_scalar_prefetch=1,
        grid=(n_seq,),
        in_specs=in_specs,
        out_specs=[out_spec(ATTN_WIDTH), out_spec(MEM_WIDTH), out_spec(POOL_WIDTH)],
        scratch_shapes=[pltpu.VMEM(bias.shape, F32), page_buffers, page_buffers,
                        pltpu.SemaphoreType.DMA((PAGE_SLOTS, 2))],
    )
    return pl.pallas_call(
        functools.partial(_sample_step_kernel, n_pages=n_pages, n_new=n_new, past=past),
        grid_spec=grid_spec,
        out_shape=[out3(ATTN_WIDTH), out3(MEM_WIDTH), out3(POOL_WIDTH)],
        compiler_params=_cparams("arbitrary"),
        name="sample_step",
    )(page_table, *lams, q, k_new, v_new, bias, g, qm, mem_kt, mem_vt, u, state16, wbd, ps, cache_k, cache_v)


def _merge_kernel(x_ref, o_ref, po_ref, om_ref, g_ref, wgl_ref, wa_ref, wp_ref, wm_ref, wout_ref, x1_ref):
    x = x_ref[...]
    h = _rmsnorm(x, g_ref[...]).astype(BF16)
    d = D_MODEL
    merged = jax.nn.sigmoid(_dot(h, wgl_ref[:, 0:d])) * _dot(o_ref[...].astype(BF16), wa_ref[...])
    merged = merged + jax.nn.sigmoid(_dot(h, wgl_ref[:, d:2 * d])) * _dot(po_ref[...].astype(BF16), wp_ref[...])
    merged = merged + jax.nn.sigmoid(_dot(h, wgl_ref[:, 2 * d:3 * d])) * _dot(om_ref[...].astype(BF16), wm_ref[...])
    x1_ref[...] = x + _dot(merged.astype(BF16), wout_ref[...])


def _merge(x2d, o, po, om, norm_g, wgl, wa, wp, wm, wout, tm):
    rows, d = x2d.shape
    row = lambda width: pl.BlockSpec((tm, width), lambda i: (i, 0))
    full = lambda a: pl.BlockSpec(a.shape, lambda i: (0, 0), pipeline_mode=pl.Buffered(1))
    g = norm_g.reshape(1, d)
    return pl.pallas_call(
        _merge_kernel,
        grid=(rows // tm,),
        in_specs=[row(d), row(ATTN_WIDTH), row(POOL_WIDTH), row(MEM_WIDTH), full(g), full(wgl), full(wa), full(wp),
                  full(wm), full(wout)],
        out_specs=row(d),
        out_shape=jax.ShapeDtypeStruct((rows, d), F32),
        compiler_params=_cparams("parallel"),
        name="merge",
    )(x2d, o, po, om, g, wgl, wa, wp, wm, wout)


def _conv_ffn_kernel(x_ref, g2_ref, wg_ref, wu_ref, cw_ref, cb_ref, wd_ref, gf_ref, *refs, tm, fc, seq_rows):
    if seq_rows:
        st0_ref, st1_ref, y_ref, t0_ref, t1_ref, act_ref, pre_ref, gate_ref = refs
        n_seqs = tm // seq_rows
        lanes = pre_ref.shape[2]
        pre_ref[...] = jnp.zeros_like(pre_ref)
        for j in range(D_FF // lanes):
            pre_ref[j, pl.ds(0, n_seqs, stride=seq_rows), :] = st0_ref[:, j * lanes:(j + 1) * lanes]
            pre_ref[j, pl.ds(1, n_seqs, stride=seq_rows), :] = st1_ref[:, j * lanes:(j + 1) * lanes]
    else:
        y_ref, tail_ref, carry_ref, act_ref = refs

        @pl.when(pl.program_id(1) == 0)
        def _():
            carry_ref[...] = jnp.zeros_like(carry_ref)

    x = x_ref[...]
    h = _rmsnorm(x, g2_ref[...]).astype(BF16)
    for c0 in range(0, D_FF, fc):
        cs = slice(c0, c0 + fc)
        gate = _dot(h, wg_ref[:, cs])
        up = _dot(h, wu_ref[:, cs])
        if seq_rows:
            slabs = range(c0 // lanes, (c0 + fc) // lanes)
            pre = jnp.concatenate([pre_ref[j] for j in slabs], axis=1)
            t = lax.broadcasted_iota(jnp.int32, (tm, fc), 0) % seq_rows
            back1 = jnp.where(t >= 1, pltpu.roll(gate, 1, 0), pltpu.roll(pre, tm - 1, 0))
            back2 = jnp.where(t >= 2, pltpu.roll(gate, 2, 0), pre)
            for j in slabs:
                gate_ref[j] = gate[:, j * lanes - c0:(j + 1) * lanes - c0]
        else:
            ext = jnp.concatenate([carry_ref[:, cs], gate], axis=0)
            back1 = pltpu.roll(ext, 1, 0)[8:]
            back2 = pltpu.roll(ext, 2, 0)[8:]
            carry_ref[:, cs] = gate[tm - 8:]
        gc = cb_ref[:, cs] + cw_ref[0:1, cs] * back2
        gc = gc + cw_ref[1:2, cs] * back1
        gc = gc + cw_ref[2:3, cs] * gate
        act_ref[:, cs] = (jax.nn.gelu(gc) * up).astype(BF16)
    if seq_rows:
        for j in range(D_FF // lanes):
            t0_ref[:, j * lanes:(j + 1) * lanes] = gate_ref[j, pl.ds(seq_rows - 2, n_seqs, stride=seq_rows), :]
            t1_ref[:, j * lanes:(j + 1) * lanes] = gate_ref[j, pl.ds(seq_rows - 1, n_seqs, stride=seq_rows), :]
    else:
        tail_ref[...] = carry_ref[...]
    y_ref[...] = _rmsnorm(x + _dot(act_ref[...], wd_ref[...]), gf_ref[...])


def _conv_ffn(x1, norm_g, wg, wu, conv_w, conv_b, wd, final_g, tm, batch=None, seq=None, prefix=None):
    rows, d = x1.shape
    g2 = norm_g.reshape(1, d)
    gf = final_g.reshape(1, d)
    cb = conv_b.reshape(1, D_FF)
    if prefix is None:
        nblk = seq // tm
        row = lambda width: pl.BlockSpec((tm, width), lambda b, i: (b * nblk + i, 0))
        full = lambda a: pl.BlockSpec(a.shape, lambda b, i: (0, 0), pipeline_mode=pl.Buffered(1))
        return pl.pallas_call(
            functools.partial(_conv_ffn_kernel, tm=tm, fc=FFN_CHUNK, seq_rows=0),
            grid=(batch, nblk),
            in_specs=[row(d), full(g2), full(wg), full(wu), full(conv_w), full(cb), full(wd), full(gf)],
            out_specs=[row(d), pl.BlockSpec((None, 8, D_FF), lambda b, i: (b, 0, 0))],
            out_shape=[jax.ShapeDtypeStruct((rows, d), F32), jax.ShapeDtypeStruct((batch, 8, D_FF), F32)],
            scratch_shapes=[pltpu.VMEM((8, D_FF), F32), pltpu.VMEM((tm, D_FF), BF16)],
            compiler_params=_cparams("parallel", "arbitrary"),
            name="conv_ffn_prompt",
        )(x1, g2, wg, wu, conv_w, cb, wd, gf)
    seq_rows = 8
    row = lambda width: pl.BlockSpec((tm, width), lambda i: (i, 0))
    per_seq = pl.BlockSpec((tm // seq_rows, D_FF), lambda i: (i, 0))
    full = lambda a: pl.BlockSpec(a.shape, lambda i: (0, 0), pipeline_mode=pl.Buffered(1))
    tails = jax.ShapeDtypeStruct((rows // seq_rows, D_FF), F32)
    return pl.pallas_call(
        functools.partial(_conv_ffn_kernel, tm=tm, fc=FFN_CHUNK, seq_rows=seq_rows),
        grid=(rows // tm,),
        in_specs=[row(d), full(g2), full(wg), full(wu), full(conv_w), full(cb), full(wd), full(gf), per_seq, per_seq],
        out_specs=[row(d), per_seq, per_seq],
        out_shape=[jax.ShapeDtypeStruct((rows, d), F32), tails, tails],
        scratch_shapes=[pltpu.VMEM((tm, D_FF), BF16), pltpu.VMEM((D_FF // LANES, tm, LANES), F32),
                        pltpu.VMEM((D_FF // LANES, tm, LANES), F32)],
        compiler_params=_cparams("parallel"),
        name="conv_ffn_sample",
    )(x1, g2, wg, wu, conv_w, cb, wd, gf, *prefix)


def kernel(x_prompt, x_sample, mem_prompt, cache_k, cache_v, page_table, state_pool, state_ffn_conv, cache_mem_k, cache_mem_v, norm1_g, w_in, lam_q1, lam_k1, lam_q2, lam_k2, subln_g, w_pool_grp, pool_scale, w_br_attn, w_br_pool, w_br_mem, mem_norm_g, w_mem_kv, w_out, norm2_g, w_ffn_gate, w_ffn_up, ffn_conv_w, ffn_conv_b, w_ffn_down, rel_bias, final_norm_g):
    depth = w_in.shape[0]
    assert depth == 1, "single-layer step only"
    B, S, D = x_prompt.shape
    DB, DS, _ = x_sample.shape
    n_phys, page = cache_k.shape[1], cache_k.shape[2]
    n_pages = page_table.shape[1]
    past = n_pages * page
    assert DS == 8 and S % ROW_BLOCK == 0 and (DB * DS) % ROW_BLOCK == 0
    assert ATTN_TK % ATTN_TQ == 0 and ROW_BLOCK % ATTN_TK == 0 and S % ATTN_TK == 0

    w_qkv = w_in[0, :, :2048].astype(BF16)
    w_gl = w_in[0, :, 2048:].astype(BF16)
    wa, wp, wm = w_br_attn[0].astype(BF16), w_br_pool[0].astype(BF16), w_br_mem[0].astype(BF16)
    wout = w_out[0].astype(BF16)
    wg, wu, wd = w_ffn_gate[0].astype(BF16), w_ffn_up[0].astype(BF16), w_ffn_down[0].astype(BF16)
    w_mem = w_mem_kv[0].astype(BF16)
    wbd = jnp.zeros((POOL_WIDTH, POOL_WIDTH), F32)
    for gi in range(len(POOL_WINDOWS)):
        sl = slice(gi * POOL_GROUP_DIM, (gi + 1) * POOL_GROUP_DIM)
        wbd = wbd.at[sl, sl].set(w_pool_grp[0, gi])
    wbd = wbd.astype(BF16)
    lams = tuple(a[0].reshape(1, HEAD_DIM) for a in (lam_q1, lam_k1, lam_q2, lam_k2))

    prompt_bias, sample_bias = _bias_tables(rel_bias, ATTN_TQ, ATTN_TK, past, DS)

    xp = x_prompt.reshape(B * S, D)
    groups = ((0, 512, ((QK_SCALE * LOG2E, 0),)), (512, 512, ((1.0, HEAD_ROWS), (1.0, 0))),
              (1024, 512, ((1.0, HEAD_ROWS), (1.0, ATTN_TK))), (1536, 256, ((1.0, 0),)), (1792, 256, ((MEM_SCALE, 0),)))
    qp, kp, kp_b, vp, vp_t, up, qmp = _norm_proj(xp, norm1_g[0], w_qkv, groups,
                                                 (BF16, F32, BF16, F32, BF16, F32, BF16), ROW_BLOCK)
    mk_p, mv_p = _norm_proj(mem_prompt.reshape(B * N_MEM, D), mem_norm_g[0], w_mem,
                            ((0, MEM_WIDTH, ((1.0, N_MEM),)), (MEM_WIDTH, MEM_WIDTH, ((1.0, N_MEM),))), (F32, F32), ROW_BLOCK)
    op = _prompt_attn(lams, qp, kp_b, vp_t, prompt_bias, subln_g[0], B, S, ATTN_TQ, ATTN_TK)
    pop, omp = _prompt_side(up, qmp, mk_p, mv_p, wbd, pool_scale[0], B, S, ROW_BLOCK)
    x1p = _merge(xp, op, pop, omp, norm1_g[0], w_gl, wa, wp, wm, wout, ROW_BLOCK)
    yp, tail_p = _conv_ffn(x1p, norm2_g[0], wg, wu, ffn_conv_w[0], ffn_conv_b[0], wd, final_norm_g, ROW_BLOCK,
                           batch=B, seq=S)

    xs = x_sample.reshape(DB * DS, D)
    groups_s = ((0, 512, ((QK_SCALE, 0),)), (512, 512, ((1.0, HEAD_ROWS),)), (1024, 512, ((1.0, HEAD_ROWS),)),
                (1536, 256, ((1.0, 0),)), (1792, 256, ((MEM_SCALE, 0),)))
    qs, ks, vs, us, qms = _norm_proj(xs, norm1_g[0], w_qkv, groups_s, (F32,) * 5, ROW_BLOCK)
    seq3 = lambda a: a.reshape(DB, DS, a.shape[-1])
    head_rows = lambda a: a.reshape(DB, DS * N_HEADS, V_DIM)
    state16 = jnp.pad(state_pool[0], ((0, 0), (16 - POOL_STATE, 0), (0, 0)))
    mem_t = lambda a: jnp.transpose(a[0], (0, 2, 3, 1)).reshape(DB, MEM_WIDTH, N_MEM)
    pages = lambda a: a[0].reshape(n_phys, page * N_HEADS, V_DIM)
    os_, oms, pos_ = _sample_step(page_table, lams, seq3(qs), head_rows(ks), head_rows(vs), sample_bias, subln_g[0],
                                  seq3(qms), mem_t(cache_mem_k), mem_t(cache_mem_v), seq3(us), state16, wbd,
                                  pool_scale[0], pages(cache_k), pages(cache_v))
    flat = lambda a: a.reshape(DB * DS, a.shape[-1])
    x1s = _merge(xs, flat(os_), flat(pos_), flat(oms), norm1_g[0], w_gl, wa, wp, wm, wout, ROW_BLOCK)
    ys, tail0_s, tail1_s = _conv_ffn(x1s, norm2_g[0], wg, wu, ffn_conv_w[0], ffn_conv_b[0], wd, final_norm_g, ROW_BLOCK,
                                     prefix=(state_ffn_conv[0, :, 0], state_ffn_conv[0, :, 1]))

    heads = lambda a, n: a.reshape(1, n, -1, N_HEADS, V_DIM)
    new_pool_p = up.reshape(B, S, POOL_WIDTH)[:, S - POOL_STATE:][None]
    new_pool_s = jnp.concatenate([state_pool[0][:, DS:], seq3(us)], axis=1)[None]
    new_conv_p = tail_p[:, 6:8][None]
    new_conv_s = jnp.stack([tail0_s, tail1_s], axis=1)[None]
    mem_heads = lambda a: jnp.transpose(a.reshape(B, N_HEADS, MEM_HEAD_DIM, N_MEM), (0, 3, 1, 2))[None]
    return (yp.reshape(B, S, D), ys.reshape(DB, DS, D), heads(kp, B), heads(vp, B), heads(ks, DB), heads(vs, DB),
            new_pool_p, new_pool_s, new_conv_p, new_conv_s, mem_heads(mk_p), mem_heads(mv_p))
```

```python
import functools
import math

import jax
import jax.numpy as jnp
from jax import lax
from jax.experimental import pallas as pl
from jax.experimental.pallas import tpu as pltpu

F32 = jnp.float32
BF16 = jnp.bfloat16

D_MODEL = 1024
N_HEADS = 4
HEAD_DIM = 64
V_DIM = 2 * HEAD_DIM
ATTN_WIDTH = N_HEADS * V_DIM
POOL_WINDOWS = (2, 4, 8, 16)
POOL_GROUP_DIM = 64
POOL_WIDTH = 256
POOL_STATE = 15
MEM_HEAD_DIM = 64
MEM_WIDTH = 256
N_MEM = 256
D_FF = 2816
N_BUCKETS = 32
MAX_DISTANCE = 128
EPS = 1e-6
NEG_INF = -1e30
QK_SCALE = HEAD_DIM ** -0.5
MEM_SCALE = MEM_HEAD_DIM ** -0.5
LAM_INIT = 0.8 - 0.6 * math.exp(-0.3 * 0)
SUBLN_SCALE = 1.0 - LAM_INIT
LOG2E = math.log2(math.e)

ATTN_TQ = 256
ATTN_TK = 256
ROW_BLOCK = 512
FFN_CHUNK = 256
SAMPLE_KPAD = 128
PAGE_SLOTS = 3
VMEM_LIMIT = 56 * 1024 * 1024
HEAD_ROWS = -1
LANES = 128


def _cparams(*sem):
    return pltpu.CompilerParams(dimension_semantics=sem, vmem_limit_bytes=VMEM_LIMIT)


def _rmsnorm(x, g):
    return x * lax.rsqrt(jnp.mean(x * x, axis=-1, keepdims=True) + EPS) * g


def _dot(a, b):
    return jnp.dot(a, b, preferred_element_type=F32)


def _dot_nt(a, b):
    return lax.dot_general(a, b, (((1,), (1,)), ((), ())), preferred_element_type=F32)


def _softmax_rows(s):
    m = jnp.max(s, axis=-1, keepdims=True)
    p = jnp.exp(s - m)
    return p * (1.0 / jnp.sum(p, axis=-1, keepdims=True))


def _lam_value(lq1, lk1, lq2, lk2):
    a = jnp.sum(lq1 * lk1, axis=-1, keepdims=True)
    b = jnp.sum(lq2 * lk2, axis=-1, keepdims=True)
    return jnp.exp(a) - jnp.exp(b) + LAM_INIT


def _norm_proj_kernel(x_ref, g_ref, w_ref, *out_refs, groups):
    h = _rmsnorm(x_ref[...], g_ref[...]).astype(BF16)
    k = 0
    for start, size, outs in groups:
        y = _dot(h, w_ref[:, start:start + size])
        for scale, key_block in outs:
            o_ref = out_refs[k]
            k += 1
            ys = y if scale == 1.0 else y * scale
            if key_block == HEAD_ROWS:
                n = ys.shape[0]
                for hd in range(N_HEADS):
                    o_ref[pl.ds(hd, n, stride=N_HEADS), :] = ys[:, hd * V_DIM:(hd + 1) * V_DIM].astype(o_ref.dtype)
            elif key_block:
                yt = ys.T
                for b in range(o_ref.shape[0]):
                    o_ref[b] = yt[:, b * key_block:(b + 1) * key_block].astype(o_ref.dtype)
            else:
                o_ref[...] = ys.astype(o_ref.dtype)


def _norm_proj_specs(rows, d, n_cols, groups, out_dtypes, tm):
    out_shape, out_specs = [], []
    flat_outs = [(size, kb) for _, size, outs in groups for _, kb in outs]
    for (size, kb), dt in zip(flat_outs, out_dtypes):
        if kb == HEAD_ROWS:
            out_shape.append(jax.ShapeDtypeStruct((rows * N_HEADS, V_DIM), dt))
            out_specs.append(pl.BlockSpec((tm * N_HEADS, V_DIM), lambda i, *_: (i, 0)))
        elif kb:
            out_shape.append(jax.ShapeDtypeStruct((rows // kb, size, kb), dt))
            out_specs.append(pl.BlockSpec((tm // kb, size, kb), lambda i, *_: (i, 0, 0)))
        else:
            out_shape.append(jax.ShapeDtypeStruct((rows, size), dt))
            out_specs.append(pl.BlockSpec((tm, size), lambda i, *_: (i, 0)))
    in_specs = [pl.BlockSpec((tm, d), lambda i, *_: (i, 0)),
                pl.BlockSpec((1, d), lambda i, *_: (0, 0)),
                pl.BlockSpec((d, n_cols), lambda i, *_: (0, 0), pipeline_mode=pl.Buffered(1))]
    return in_specs, out_specs, out_shape


def _norm_proj(x2d, g, w_bf16, groups, out_dtypes, tm):
    rows, d = x2d.shape
    in_specs, out_specs, out_shape = _norm_proj_specs(rows, d, w_bf16.shape[1], groups, out_dtypes, tm)
    return pl.pallas_call(
        functools.partial(_norm_proj_kernel, groups=groups),
        grid=(rows // tm,),
        in_specs=in_specs,
        out_specs=out_specs,
        out_shape=out_shape,
        compiler_params=_cparams("parallel"),
        name="norm_proj",
    )(x2d, g.reshape(1, d), w_bf16)


def _rel_bucket(rel):
    n = jnp.maximum(rel, 0)
    max_exact = N_BUCKETS // 2
    nf = jnp.maximum(n, 1).astype(F32)
    large = max_exact + jnp.floor(jnp.log(nf / max_exact) / math.log(MAX_DISTANCE / max_exact)
                                  * (N_BUCKETS - max_exact)).astype(jnp.int32)
    large = jnp.minimum(large, N_BUCKETS - 1)
    return jnp.where(n < max_exact, n, large)


def _bias_from_rel(rel, rb_ref, head, visible):
    bucket = _rel_bucket(rel)
    acc = jnp.zeros(rel.shape, F32)
    for b in range(N_BUCKETS):
        acc = jnp.where(bucket == b, rb_ref[b, head], acc)
    return jnp.where(visible, acc, NEG_INF)


def _bias_kernel(rb_ref, pb_ref, sb_ref, *, tq, tk, past, n_new):
    n_kinds = pb_ref.shape[1]
    key = lax.broadcasted_iota(jnp.int32, (tk, tq), 0)
    qry = lax.broadcasted_iota(jnp.int32, (tk, tq), 1)
    for kind in range(n_kinds):
        rel = kind * tq + qry - key
        for h in range(N_HEADS):
            if kind * tq - (tk - 1) >= MAX_DISTANCE:
                tile = jnp.full((tk, tq), rb_ref[N_BUCKETS - 1, h], F32) * LOG2E
            else:
                tile = _bias_from_rel(rel, rb_ref, h, rel >= 0) * LOG2E
            pb_ref[h, kind, :, :tq] = tile
            pb_ref[h, kind, :, tq:] = tile
    rows_per_head = 2 * n_new
    n_cols = sb_ref.shape[1]
    qi = lax.broadcasted_iota(jnp.int32, (rows_per_head, n_cols), 0) % n_new
    key = lax.broadcasted_iota(jnp.int32, (rows_per_head, n_cols), 1)
    rel = past + qi - key
    for h in range(N_HEADS):
        sb_ref[h * rows_per_head:(h + 1) * rows_per_head, :] = _bias_from_rel(rel, rb_ref, h, rel >= 0)


def _bias_tables(rel_bias, tq, tk, past, n_new):
    n_kinds = -(-(tk + MAX_DISTANCE - 1) // tq) + 1
    n_cols = past + SAMPLE_KPAD
    return pl.pallas_call(
        functools.partial(_bias_kernel, tq=tq, tk=tk, past=past, n_new=n_new),
        in_specs=[pl.BlockSpec(memory_space=pltpu.SMEM)],
        out_specs=[pl.BlockSpec(memory_space=pltpu.VMEM), pl.BlockSpec(memory_space=pltpu.VMEM)],
        out_shape=[jax.ShapeDtypeStruct((N_HEADS, n_kinds, tk, 2 * tq), F32),
                   jax.ShapeDtypeStruct((N_HEADS * 2 * n_new, n_cols), F32)],
        compiler_params=pltpu.CompilerParams(vmem_limit_bytes=VMEM_LIMIT),
        name="bias_tables",
    )(rel_bias)


def _subln(o, g):
    return _rmsnorm(o, g) * SUBLN_SCALE


def _two_map_queries(q_h):
    lane = lax.broadcasted_iota(jnp.int32, q_h.shape, 1)
    zero = jnp.zeros_like(q_h)
    return jnp.concatenate([jnp.where(lane < HEAD_DIM, q_h, zero), jnp.where(lane >= HEAD_DIM, q_h, zero)], axis=0)


def _prompt_attn_kernel(lq1_ref, lk1_ref, lq2_ref, lk2_ref, q_ref, k_ref, vt_ref, bias_ref, g_ref, o_ref,
                        q2_ref, m_ref, l_ref, acc_ref, s0_ref, *, tq, tk):
    i = pl.program_id(1)
    lam = _lam_value(lq1_ref[...], lk1_ref[...], lq2_ref[...], lk2_ref[...])
    n_kinds = bias_ref.shape[1]
    for h in range(N_HEADS):
        q2_ref[h] = _two_map_queries(q_ref[:, h * V_DIM:(h + 1) * V_DIM])
    m_ref[...] = jnp.full(m_ref.shape, NEG_INF, F32)
    l_ref[...] = jnp.zeros(l_ref.shape, F32)
    acc_ref[...] = jnp.zeros(acc_ref.shape, F32)

    last = (i * tq + tq - 1) // tk

    def scores(j, h):
        r0 = pl.multiple_of(j * tk, tk)
        kind = jnp.minimum((i * tq - j * tk) // tq, n_kinds - 1)
        hs = slice(h * V_DIM, (h + 1) * V_DIM)
        return _dot_nt(k_ref[pl.ds(r0, tk), hs], q2_ref[h]) + bias_ref[h, kind]

    ahead = 2
    for h in range(ahead):
        s0_ref[h] = scores(0, h)

    ones_rows = jnp.ones((16, tk), BF16)

    def body(j, carry):
        pending = [s0_ref[h] for h in range(ahead)]
        j_next = jnp.minimum(j + 1, last)
        for h in range(N_HEADS):
            hs = slice(h * V_DIM, (h + 1) * V_DIM)
            s = pending.pop(0)
            pending.append(scores(j, h + ahead) if h + ahead < N_HEADS else scores(j_next, h + ahead - N_HEADS))
            m_old = m_ref[h]
            m_new = jnp.maximum(m_old, jnp.max(s, axis=0, keepdims=True))
            alpha = jnp.exp2(m_old - m_new)
            p = jnp.exp2(s - m_new).astype(BF16)
            pv = _dot(jnp.concatenate([vt_ref[j, hs, :], ones_rows], axis=0), p)
            l_ref[h] = alpha * l_ref[h] + pv[V_DIM:V_DIM + 1]
            m_ref[h] = m_new
            acc_ref[h] = alpha * acc_ref[h] + pv[:V_DIM]
        for h in range(ahead):
            s0_ref[h] = pending[h]
        return carry

    lax.fori_loop(0, last + 1, body, 0)
    for h in range(N_HEADS):
        out_t = acc_ref[h] / l_ref[h]
        o_t = out_t[:, :tq] - lam * out_t[:, tq:]
        o_ref[:, h * V_DIM:(h + 1) * V_DIM] = _subln(o_t.T, g_ref[...]).astype(o_ref.dtype)


def _prompt_attn(lams, q, k, vt, bias, subln_g, batch, seq, tq, tk):
    vec = pl.BlockSpec((1, HEAD_DIM), lambda b, i: (0, 0))
    nq = seq // tq
    return pl.pallas_call(
        functools.partial(_prompt_attn_kernel, tq=tq, tk=tk),
        grid=(batch, nq),
        in_specs=[vec, vec, vec, vec,
                  pl.BlockSpec((tq, ATTN_WIDTH), lambda b, i: (b * nq + i, 0)),
                  pl.BlockSpec((seq, ATTN_WIDTH), lambda b, i: (b, 0)),
                  pl.BlockSpec((seq // tk, ATTN_WIDTH, tk), lambda b, i: (b, 0, 0)),
                  pl.BlockSpec(bias.shape, lambda b, i: (0, 0, 0, 0), pipeline_mode=pl.Buffered(1)),
                  pl.BlockSpec((1, V_DIM), lambda b, i: (0, 0))],
        out_specs=pl.BlockSpec((tq, ATTN_WIDTH), lambda b, i: (b * nq + i, 0)),
        out_shape=jax.ShapeDtypeStruct((batch * seq, ATTN_WIDTH), BF16),
        scratch_shapes=[pltpu.VMEM((N_HEADS, 2 * tq, V_DIM), BF16),
                        pltpu.VMEM((N_HEADS, 1, 2 * tq), F32),
                        pltpu.VMEM((N_HEADS, 1, 2 * tq), F32),
                        pltpu.VMEM((N_HEADS, V_DIM, 2 * tq), F32),
                        pltpu.VMEM((2, tk, 2 * tq), F32)],
        compiler_params=_cparams("parallel", "parallel"),
        name="prompt_attn",
    )(*lams, q, k, vt, bias, subln_g.reshape(1, V_DIM))


def _pool_branch(ext, u, pos0, wbd, scale):
    n, c = u.shape
    p = ext.shape[0] - n
    s2 = ext + pltpu.roll(ext, 1, 0)
    s4 = s2 + pltpu.roll(s2, 2, 0)
    s8 = s4 + pltpu.roll(s4, 4, 0)
    s16 = s8 + pltpu.roll(s8, 8, 0)
    lane = lax.broadcasted_iota(jnp.int32, (n, c), 1)
    grp = lane // POOL_GROUP_DIM
    win = jnp.where(grp == 0, s2[p:], jnp.where(grp == 1, s4[p:], jnp.where(grp == 2, s8[p:], s16[p:])))
    width = jnp.where(grp == 0, 2, jnp.where(grp == 1, 4, jnp.where(grp == 2, 8, 16)))
    pos = pos0 + lax.broadcasted_iota(jnp.int32, (n, c), 0)
    cnt = jnp.minimum(pos + 1, width).astype(F32)
    d = win / cnt - u
    return _dot(d.astype(BF16), wbd) * scale


def _prompt_side_kernel(u_ref, uprev_ref, qm_ref, mkt_ref, mvt_ref, wbd_ref, ps_ref, po_ref, om_ref, *, tm):
    i = pl.program_id(1)
    u = u_ref[...]
    prev = jnp.where(i > 0, uprev_ref[...], 0.0)
    ext = jnp.concatenate([prev, u], axis=0)
    po_ref[...] = _pool_branch(ext, u, i * tm, wbd_ref[...], ps_ref[...]).astype(po_ref.dtype)
    heads = [slice(h * MEM_HEAD_DIM, (h + 1) * MEM_HEAD_DIM) for h in range(N_HEADS)]
    scores = [_dot(qm_ref[:, sl], mkt_ref[sl, :].astype(BF16)) for sl in heads]
    probs = [_softmax_rows(s).astype(BF16) for s in scores]
    for sl, p in zip(heads, probs):
        om_ref[:, sl] = _dot_nt(p, mvt_ref[sl, :].astype(BF16)).astype(om_ref.dtype)


def _prompt_side(u, qm, mk, mv, wbd, pool_scale, batch, seq, tm):
    nblk = seq // tm
    halo = 16
    row = lambda b, i: (b * nblk + i, 0)
    return pl.pallas_call(
        functools.partial(_prompt_side_kernel, tm=tm),
        grid=(batch, nblk),
        in_specs=[pl.BlockSpec((tm, POOL_WIDTH), row),
                  pl.BlockSpec((halo, POOL_WIDTH),
                               lambda b, i: (jnp.maximum((b * nblk + i) * (tm // halo) - 1, 0), 0)),
                  pl.BlockSpec((tm, MEM_WIDTH), row),
                  pl.BlockSpec((None, MEM_WIDTH, N_MEM), lambda b, i: (b, 0, 0)),
                  pl.BlockSpec((None, MEM_WIDTH, N_MEM), lambda b, i: (b, 0, 0)),
                  pl.BlockSpec((POOL_WIDTH, POOL_WIDTH), lambda b, i: (0, 0)),
                  pl.BlockSpec((1, POOL_WIDTH), lambda b, i: (0, 0))],
        out_specs=[pl.BlockSpec((tm, POOL_WIDTH), row), pl.BlockSpec((tm, MEM_WIDTH), row)],
        out_shape=[jax.ShapeDtypeStruct((batch * seq, POOL_WIDTH), BF16),
                   jax.ShapeDtypeStruct((batch * seq, MEM_WIDTH), BF16)],
        compiler_params=_cparams("parallel", "parallel"),
        name="prompt_side",
    )(u, u, qm, mk, mv, wbd, pool_scale.reshape(1, POOL_WIDTH))


def _sample_sequence(seq, n_seq, pt_ref, lq1_ref, lk1_ref, lq2_ref, lk2_ref, q_ref, kn_ref, vn_ref, bias_ref, g_ref,
                     qm_ref, mkt_ref, mvt_ref, u_ref, st_ref, wbd_ref, ps_ref, ck_hbm, cv_hbm,
                     o_ref, om_ref, po_ref, s_ref, kbuf, vbuf, sem, *, n_pages, n_new, past):
    slot = seq % PAGE_SLOTS
    lam = _lam_value(lq1_ref[...], lk1_ref[...], lq2_ref[...], lk2_ref[...])
    page = kbuf.shape[2] // N_HEADS
    n_tail = s_ref.shape[1] - n_pages * page
    pad = jnp.zeros((n_tail - n_new, V_DIM), F32)
    hr = 2 * n_new
    caches = ((ck_hbm, kbuf), (cv_hbm, vbuf))

    def page_copy(which, slot_, p, page_id):
        hbm, buf = caches[which]
        return pltpu.make_async_copy(hbm.at[page_id], buf.at[slot_, p], sem.at[slot_, which])

    def start_pages(seq_, slot_):
        for p in range(n_pages):
            page_id = pt_ref[seq_, p]
            page_copy(0, slot_, p, page_id).start()
            page_copy(1, slot_, p, page_id).start()

    def wait_pages(which):
        for p in range(n_pages):
            page_copy(which, slot, p, 0).wait()

    ahead = PAGE_SLOTS - 1

    @pl.when(seq == 0)
    def _():
        for s0 in range(ahead):
            start_pages(s0, s0)

    @pl.when(seq + ahead < n_seq)
    def _():
        start_pages(seq + ahead, (seq + ahead) % PAGE_SLOTS)

    def page_pair(buf, c, h):
        rows = pl.ds(h, page, stride=N_HEADS)
        return jnp.concatenate([buf[slot, c, rows, :], buf[slot, c + 1, rows, :]], axis=0).astype(BF16)

    def head_rows(ref, h):
        return ref[pl.ds(h, n_new, stride=N_HEADS), :]

    wait_pages(0)
    for h in range(N_HEADS):
        q2 = _two_map_queries(q_ref[:, h * V_DIM:(h + 1) * V_DIM]).astype(BF16)
        for c in range(0, n_pages, 2):
            s_ref[h * hr:(h + 1) * hr, c * page:(c + 2) * page] = _dot_nt(q2, page_pair(kbuf, c, h))
        k_tail = jnp.concatenate([head_rows(kn_ref, h), pad], axis=0).astype(BF16)
        s_ref[h * hr:(h + 1) * hr, n_pages * page:] = _dot_nt(q2, k_tail)
    prob = _softmax_rows(s_ref[...] + bias_ref[...])
    wait_pages(1)
    for h in range(N_HEADS):
        a = prob[h * hr:h * hr + n_new] - lam * prob[h * hr + n_new:(h + 1) * hr]
        v_tail = jnp.concatenate([head_rows(vn_ref, h), pad], axis=0).astype(BF16)
        o_h = _dot(a[:, n_pages * page:].astype(BF16), v_tail)
        for c in range(0, n_pages, 2):
            o_h = o_h + _dot(a[:, c * page:(c + 2) * page].astype(BF16), page_pair(vbuf, c, h))
        o_ref[:, h * V_DIM:(h + 1) * V_DIM] = _subln(o_h, g_ref[...]).astype(o_ref.dtype)
    qm = qm_ref[...]
    head_of_lane = lax.broadcasted_iota(jnp.int32, qm.shape, 1) // MEM_HEAD_DIM
    qm4 = jnp.concatenate([jnp.where(head_of_lane == h, qm, 0.0) for h in range(N_HEADS)], axis=0).astype(BF16)
    pm = _softmax_rows(_dot(qm4, mkt_ref[...].astype(BF16)))
    om_all = _dot_nt(pm.astype(BF16), mvt_ref[...].astype(BF16))
    om = jnp.zeros(qm.shape, F32)
    for h in range(N_HEADS):
        om = jnp.where(head_of_lane == h, om_all[h * n_new:(h + 1) * n_new], om)
    om_ref[...] = om.astype(om_ref.dtype)
    u = u_ref[...]
    ext = jnp.concatenate([st_ref[...], u], axis=0)
    po_ref[...] = _pool_branch(ext, u, past, wbd_ref[...], ps_ref[...]).astype(po_ref.dtype)


def _proj_and_sample_kernel(pt_ref, x_ref, g1_ref, w_ref, lq1_ref, lk1_ref, lq2_ref, lk2_ref, q_ref, kn_ref, vn_ref,
                            bias_ref, g_ref, qm_ref, mkt_ref, mvt_ref, u_ref, st_ref, wbd_ref, ps_ref, ck_hbm, cv_hbm,
                            *refs, groups, n_proj_out, seqs_per_step, n_seq, n_pages, n_new, past):
    proj_out = refs[:n_proj_out]
    o_ref, om_ref, po_ref, s_ref, kbuf, vbuf, sem = refs[n_proj_out:]
    _norm_proj_kernel(x_ref, g1_ref, w_ref, *proj_out, groups=groups)
    first = pl.program_id(0) * seqs_per_step
    for j in range(seqs_per_step):
        _sample_sequence(first + j, n_seq, pt_ref, lq1_ref, lk1_ref, lq2_ref, lk2_ref, q_ref.at[j], kn_ref.at[j],
                         vn_ref.at[j], bias_ref, g_ref, qm_ref.at[j], mkt_ref.at[j], mvt_ref.at[j], u_ref.at[j],
                         st_ref.at[j], wbd_ref, ps_ref, ck_hbm, cv_hbm, o_ref.at[j], om_ref.at[j], po_ref.at[j],
                         s_ref, kbuf, vbuf, sem, n_pages=n_pages, n_new=n_new, past=past)


def _proj_and_sample(x2d, norm_g, w_bf16, groups, out_dtypes, tm, page_table, lams, q, k_new, v_new, bias, subln_g,
                     qm, mem_kt, mem_vt, u, state16, wbd, pool_scale, cache_k, cache_v):
    rows, d = x2d.shape
    n_steps = rows // tm
    n_seq, n_pages = page_table.shape
    n_new = q.shape[1]
    page_rows = cache_k.shape[1]
    past = n_pages * page_rows // N_HEADS
    assert n_pages % 2 == 0 and n_seq >= PAGE_SLOTS and n_seq % n_steps == 0
    sps = n_seq // n_steps
    proj_in, proj_out_specs, proj_out_shape = _norm_proj_specs(rows, d, w_bf16.shape[1], groups, out_dtypes, tm)
    vec = pl.BlockSpec((1, HEAD_DIM), lambda i, pt: (0, 0))
    per_seq = lambda a: pl.BlockSpec((sps,) + a.shape[1:], lambda i, pt: (i,) + (0,) * (a.ndim - 1))
    const = lambda a: pl.BlockSpec(a.shape, lambda i, pt: (0,) * a.ndim)
    hbm = pl.BlockSpec(memory_space=pl.ANY)
    g = subln_g.reshape(1, V_DIM)
    ps = pool_scale.reshape(1, POOL_WIDTH)
    in_specs = proj_in + [vec, vec, vec, vec, per_seq(q), per_seq(k_new), per_seq(v_new), const(bias), const(g),
                          per_seq(qm), per_seq(mem_kt), per_seq(mem_vt), per_seq(u), per_seq(state16), const(wbd),
                          const(ps), hbm, hbm]
    out3 = lambda width: jax.ShapeDtypeStruct((n_seq, n_new, width), F32)
    out_spec = lambda width: pl.BlockSpec((sps, n_new, width), lambda i, pt: (i, 0, 0))
    page_buffers = pltpu.VMEM((PAGE_SLOTS, n_pages, page_rows, V_DIM), F32)
    grid_spec = pltpu.PrefetchScalarGridSpec(
        num_scalar_prefetch=1,
        grid=(n_steps,),
        in_specs=in_specs,
        out_specs=proj_out_specs + [out_spec(ATTN_WIDTH), out_spec(MEM_WIDTH), out_spec(POOL_WIDTH)],
        scratch_shapes=[pltpu.VMEM(bias.shape, F32), page_buffers, page_buffers,
                        pltpu.SemaphoreType.DMA((PAGE_SLOTS, 2))],
    )
    outs = pl.pallas_call(
        functools.partial(_proj_and_sample_kernel, groups=groups, n_proj_out=len(proj_out_shape), seqs_per_step=sps,
                          n_seq=n_seq, n_pages=n_pages, n_new=n_new, past=past),
        grid_spec=grid_spec,
        out_shape=proj_out_shape + [out3(ATTN_WIDTH), out3(MEM_WIDTH), out3(POOL_WIDTH)],
        compiler_params=_cparams("arbitrary"),
        name="proj_and_sample",
    )(page_table, x2d, norm_g.reshape(1, d), w_bf16, *lams, q, k_new, v_new, bias, g, qm, mem_kt, mem_vt, u, state16,
      wbd, ps, cache_k, cache_v)
    return outs[:len(proj_out_shape)], outs[len(proj_out_shape):]


def _merge_kernel(x_ref, o_ref, po_ref, om_ref, g_ref, wgl_ref, wa_ref, wp_ref, wm_ref, wout_ref, x1_ref):
    x = x_ref[...]
    h = _rmsnorm(x, g_ref[...]).astype(BF16)
    d = D_MODEL
    merged = jax.nn.sigmoid(_dot(h, wgl_ref[:, 0:d])) * _dot(o_ref[...].astype(BF16), wa_ref[...])
    merged = merged + jax.nn.sigmoid(_dot(h, wgl_ref[:, d:2 * d])) * _dot(po_ref[...].astype(BF16), wp_ref[...])
    merged = merged + jax.nn.sigmoid(_dot(h, wgl_ref[:, 2 * d:3 * d])) * _dot(om_ref[...].astype(BF16), wm_ref[...])
    x1_ref[...] = x + _dot(merged.astype(BF16), wout_ref[...])


def _merge(x2d, o, po, om, norm_g, wgl, wa, wp, wm, wout, tm):
    rows, d = x2d.shape
    row = lambda width: pl.BlockSpec((tm, width), lambda i: (i, 0))
    full = lambda a: pl.BlockSpec(a.shape, lambda i: (0, 0), pipeline_mode=pl.Buffered(1))
    g = norm_g.reshape(1, d)
    return pl.pallas_call(
        _merge_kernel,
        grid=(rows // tm,),
        in_specs=[row(d), row(ATTN_WIDTH), row(POOL_WIDTH), row(MEM_WIDTH), full(g), full(wgl), full(wa), full(wp),
                  full(wm), full(wout)],
        out_specs=row(d),
        out_shape=jax.ShapeDtypeStruct((rows, d), F32),
        compiler_params=_cparams("parallel"),
        name="merge",
    )(x2d, o, po, om, g, wgl, wa, wp, wm, wout)


def _conv_ffn_kernel(x_ref, g2_ref, wg_ref, wu_ref, cw_ref, cb_ref, wd_ref, gf_ref, *refs, tm, fc, seq_rows):
    if seq_rows:
        st0_ref, st1_ref, y_ref, t0_ref, t1_ref, act_ref, pre_ref, gate_ref = refs
        n_seqs = tm // seq_rows
        lanes = pre_ref.shape[2]
        pre_ref[...] = jnp.zeros_like(pre_ref)
        for j in range(D_FF // lanes):
            pre_ref[j, pl.ds(0, n_seqs, stride=seq_rows), :] = st0_ref[:, j * lanes:(j + 1) * lanes]
            pre_ref[j, pl.ds(1, n_seqs, stride=seq_rows), :] = st1_ref[:, j * lanes:(j + 1) * lanes]
    else:
        y_ref, tail_ref, carry_ref, act_ref = refs

        @pl.when(pl.program_id(1) == 0)
        def _():
            carry_ref[...] = jnp.zeros_like(carry_ref)

    x = x_ref[...]
    h = _rmsnorm(x, g2_ref[...]).astype(BF16)
    for c0 in range(0, D_FF, fc):
        cs = slice(c0, c0 + fc)
        gate = _dot(h, wg_ref[:, cs])
        up = _dot(h, wu_ref[:, cs])
        if seq_rows:
            slabs = range(c0 // lanes, (c0 + fc) // lanes)
            pre = jnp.concatenate([pre_ref[j] for j in slabs], axis=1)
            t = lax.broadcasted_iota(jnp.int32, (tm, fc), 0) % seq_rows
            back1 = jnp.where(t >= 1, pltpu.roll(gate, 1, 0), pltpu.roll(pre, tm - 1, 0))
            back2 = jnp.where(t >= 2, pltpu.roll(gate, 2, 0), pre)
            for j in slabs:
                gate_ref[j] = gate[:, j * lanes - c0:(j + 1) * lanes - c0]
        else:
            ext = jnp.concatenate([carry_ref[:, cs], gate], axis=0)
            back1 = pltpu.roll(ext, 1, 0)[8:]
            back2 = pltpu.roll(ext, 2, 0)[8:]
            carry_ref[:, cs] = gate[tm - 8:]
        gc = cb_ref[:, cs] + cw_ref[0:1, cs] * back2
        gc = gc + cw_ref[1:2, cs] * back1
        gc = gc + cw_ref[2:3, cs] * gate
        act_ref[:, cs] = (jax.nn.gelu(gc) * up).astype(BF16)
    if seq_rows:
        for j in range(D_FF // lanes):
            t0_ref[:, j * lanes:(j + 1) * lanes] = gate_ref[j, pl.ds(seq_rows - 2, n_seqs, stride=seq_rows), :]
            t1_ref[:, j * lanes:(j + 1) * lanes] = gate_ref[j, pl.ds(seq_rows - 1, n_seqs, stride=seq_rows), :]
    else:
        tail_ref[...] = carry_ref[...]
    y_ref[...] = _rmsnorm(x + _dot(act_ref[...], wd_ref[...]), gf_ref[...])


def _conv_ffn(x1, norm_g, wg, wu, conv_w, conv_b, wd, final_g, tm, batch=None, seq=None, prefix=None):
    rows, d = x1.shape
    g2 = norm_g.reshape(1, d)
    gf = final_g.reshape(1, d)
    cb = conv_b.reshape(1, D_FF)
    if prefix is None:
        nblk = seq // tm
        row = lambda width: pl.BlockSpec((tm, width), lambda b, i: (b * nblk + i, 0))
        full = lambda a: pl.BlockSpec(a.shape, lambda b, i: (0, 0), pipeline_mode=pl.Buffered(1))
        return pl.pallas_call(
            functools.partial(_conv_ffn_kernel, tm=tm, fc=FFN_CHUNK, seq_rows=0),
            grid=(batch, nblk),
            in_specs=[row(d), full(g2), full(wg), full(wu), full(conv_w), full(cb), full(wd), full(gf)],
            out_specs=[row(d), pl.BlockSpec((None, 8, D_FF), lambda b, i: (b, 0, 0))],
            out_shape=[jax.ShapeDtypeStruct((rows, d), F32), jax.ShapeDtypeStruct((batch, 8, D_FF), F32)],
            scratch_shapes=[pltpu.VMEM((8, D_FF), F32), pltpu.VMEM((tm, D_FF), BF16)],
            compiler_params=_cparams("parallel", "arbitrary"),
            name="conv_ffn_prompt",
        )(x1, g2, wg, wu, conv_w, cb, wd, gf)
    seq_rows = 8
    row = lambda width: pl.BlockSpec((tm, width), lambda i: (i, 0))
    per_seq = pl.BlockSpec((tm // seq_rows, D_FF), lambda i: (i, 0))
    full = lambda a: pl.BlockSpec(a.shape, lambda i: (0, 0), pipeline_mode=pl.Buffered(1))
    tails = jax.ShapeDtypeStruct((rows // seq_rows, D_FF), F32)
    return pl.pallas_call(
        functools.partial(_conv_ffn_kernel, tm=tm, fc=FFN_CHUNK, seq_rows=seq_rows),
        grid=(rows // tm,),
        in_specs=[row(d), full(g2), full(wg), full(wu), full(conv_w), full(cb), full(wd), full(gf), per_seq, per_seq],
        out_specs=[row(d), per_seq, per_seq],
        out_shape=[jax.ShapeDtypeStruct((rows, d), F32), tails, tails],
        scratch_shapes=[pltpu.VMEM((tm, D_FF), BF16), pltpu.VMEM((D_FF // LANES, tm, LANES), F32),
                        pltpu.VMEM((D_FF // LANES, tm, LANES), F32)],
        compiler_params=_cparams("parallel"),
        name="conv_ffn_sample",
    )(x1, g2, wg, wu, conv_w, cb, wd, gf, *prefix)


def kernel(x_prompt, x_sample, mem_prompt, cache_k, cache_v, page_table, state_pool, state_ffn_conv, cache_mem_k, cache_mem_v, norm1_g, w_in, lam_q1, lam_k1, lam_q2, lam_k2, subln_g, w_pool_grp, pool_scale, w_br_attn, w_br_pool, w_br_mem, mem_norm_g, w_mem_kv, w_out, norm2_g, w_ffn_gate, w_ffn_up, ffn_conv_w, ffn_conv_b, w_ffn_down, rel_bias, final_norm_g):
    depth = w_in.shape[0]
    assert depth == 1, "single-layer step only"
    B, S, D = x_prompt.shape
    DB, DS, _ = x_sample.shape
    n_phys, page = cache_k.shape[1], cache_k.shape[2]
    n_pages = page_table.shape[1]
    past = n_pages * page
    assert DS == 8 and S % ROW_BLOCK == 0 and (DB * DS) % ROW_BLOCK == 0
    assert ATTN_TK % ATTN_TQ == 0 and ROW_BLOCK % ATTN_TK == 0 and S % ATTN_TK == 0

    w_qkv = w_in[0, :, :2048].astype(BF16)
    w_gl = w_in[0, :, 2048:].astype(BF16)
    wa, wp, wm = w_br_attn[0].astype(BF16), w_br_pool[0].astype(BF16), w_br_mem[0].astype(BF16)
    wout = w_out[0].astype(BF16)
    wg, wu, wd = w_ffn_gate[0].astype(BF16), w_ffn_up[0].astype(BF16), w_ffn_down[0].astype(BF16)
    w_mem = w_mem_kv[0].astype(BF16)
    wbd = jnp.zeros((POOL_WIDTH, POOL_WIDTH), F32)
    for gi in range(len(POOL_WINDOWS)):
        sl = slice(gi * POOL_GROUP_DIM, (gi + 1) * POOL_GROUP_DIM)
        wbd = wbd.at[sl, sl].set(w_pool_grp[0, gi])
    wbd = wbd.astype(BF16)
    lams = tuple(a[0].reshape(1, HEAD_DIM) for a in (lam_q1, lam_k1, lam_q2, lam_k2))

    prompt_bias, sample_bias = _bias_tables(rel_bias, ATTN_TQ, ATTN_TK, past, DS)

    xs = x_sample.reshape(DB * DS, D)
    groups_s = ((0, 512, ((QK_SCALE, 0),)), (512, 512, ((1.0, HEAD_ROWS),)), (1024, 512, ((1.0, HEAD_ROWS),)),
                (1536, 256, ((1.0, 0),)), (1792, 256, ((MEM_SCALE, 0),)))
    qs, ks, vs, us, qms = _norm_proj(xs, norm1_g[0], w_qkv, groups_s, (F32,) * 5, ROW_BLOCK)
    seq3 = lambda a: a.reshape(DB, DS, a.shape[-1])
    head_rows = lambda a: a.reshape(DB, DS * N_HEADS, V_DIM)
    state16 = jnp.pad(state_pool[0], ((0, 0), (16 - POOL_STATE, 0), (0, 0)))
    mem_t = lambda a: jnp.transpose(a[0], (0, 2, 3, 1)).reshape(DB, MEM_WIDTH, N_MEM)
    pages = lambda a: a[0].reshape(n_phys, page * N_HEADS, V_DIM)

    xp = x_prompt.reshape(B * S, D)
    groups = ((0, 512, ((QK_SCALE * LOG2E, 0),)), (512, 512, ((1.0, HEAD_ROWS), (1.0, 0))),
              (1024, 512, ((1.0, HEAD_ROWS), (1.0, ATTN_TK))), (1536, 256, ((1.0, 0),)), (1792, 256, ((MEM_SCALE, 0),)))
    (qp, kp, kp_b, vp, vp_t, up, qmp), (os_, oms, pos_) = _proj_and_sample(
        xp, norm1_g[0], w_qkv, groups, (BF16, F32, BF16, F32, BF16, F32, BF16), ROW_BLOCK,
        page_table, lams, seq3(qs), head_rows(ks), head_rows(vs), sample_bias, subln_g[0], seq3(qms),
        mem_t(cache_mem_k), mem_t(cache_mem_v), seq3(us), state16, wbd, pool_scale[0], pages(cache_k), pages(cache_v))
    mk_p, mv_p = _norm_proj(mem_prompt.reshape(B * N_MEM, D), mem_norm_g[0], w_mem,
                            ((0, MEM_WIDTH, ((1.0, N_MEM),)), (MEM_WIDTH, MEM_WIDTH, ((1.0, N_MEM),))), (F32, F32), ROW_BLOCK)
    op = _prompt_attn(lams, qp, kp_b, vp_t, prompt_bias, subln_g[0], B, S, ATTN_TQ, ATTN_TK)
    pop, omp = _prompt_side(up, qmp, mk_p, mv_p, wbd, pool_scale[0], B, S, ROW_BLOCK)
    x1p = _merge(xp, op, pop, omp, norm1_g[0], w_gl, wa, wp, wm, wout, ROW_BLOCK)
    yp, tail_p = _conv_ffn(x1p, norm2_g[0], wg, wu, ffn_conv_w[0], ffn_conv_b[0], wd, final_norm_g, ROW_BLOCK,
                           batch=B, seq=S)

    flat = lambda a: a.reshape(DB * DS, a.shape[-1])
    x1s = _merge(xs, flat(os_), flat(pos_), flat(oms), norm1_g[0], w_gl, wa, wp, wm, wout, ROW_BLOCK)
    ys, tail0_s, tail1_s = _conv_ffn(x1s, norm2_g[0], wg, wu, ffn_conv_w[0], ffn_conv_b[0], wd, final_norm_g, ROW_BLOCK,
                                     prefix=(state_ffn_conv[0, :, 0], state_ffn_conv[0, :, 1]))

    heads = lambda a, n: a.reshape(1, n, -1, N_HEADS, V_DIM)
    new_pool_p = up.reshape(B, S, POOL_WIDTH)[:, S - POOL_STATE:][None]
    new_pool_s = jnp.concatenate([state_pool[0][:, DS:], seq3(us)], axis=1)[None]
    new_conv_p = tail_p[:, 6:8][None]
    new_conv_s = jnp.stack([tail0_s, tail1_s], axis=1)[None]
    mem_heads = lambda a: jnp.transpose(a.reshape(B, N_HEADS, MEM_HEAD_DIM, N_MEM), (0, 3, 1, 2))[None]
    return (yp.reshape(B, S, D), ys.reshape(DB, DS, D), heads(kp, B), heads(vp, B), heads(ks, DB), heads(vs, DB),
            new_pool_p, new_pool_s, new_conv_p, new_conv_s, mem_heads(mk_p), mem_heads(mv_p))
```

```python
import functools
import math

import jax
import jax.numpy as jnp
from jax import lax
from jax.experimental import pallas as pl
from jax.experimental.pallas import tpu as pltpu

F32 = jnp.float32
BF16 = jnp.bfloat16

D_MODEL = 1024
N_HEADS = 4
HEAD_DIM = 64
V_DIM = 2 * HEAD_DIM
ATTN_WIDTH = N_HEADS * V_DIM
POOL_WINDOWS = (2, 4, 8, 16)
POOL_GROUP_DIM = 64
POOL_WIDTH = 256
POOL_STATE = 15
MEM_HEAD_DIM = 64
MEM_WIDTH = 256
N_MEM = 256
D_FF = 2816
N_BUCKETS = 32
MAX_DISTANCE = 128
EPS = 1e-6
NEG_INF = -1e30
QK_SCALE = HEAD_DIM ** -0.5
MEM_SCALE = MEM_HEAD_DIM ** -0.5
LAM_INIT = 0.8 - 0.6 * math.exp(-0.3 * 0)
SUBLN_SCALE = 1.0 - LAM_INIT
LOG2E = math.log2(math.e)

ATTN_TQ = 256
ATTN_TK = 256
ROW_BLOCK = 512
FFN_CHUNK = 256
SAMPLE_KPAD = 128
PAGE_SLOTS = 4
VMEM_LIMIT = 60 * 1024 * 1024
HEAD_ROWS = -1
LANES = 128


def _cparams(*sem):
    return pltpu.CompilerParams(dimension_semantics=sem, vmem_limit_bytes=VMEM_LIMIT)


def _rmsnorm(x, g):
    return x * lax.rsqrt(jnp.mean(x * x, axis=-1, keepdims=True) + EPS) * g


def _dot(a, b):
    return jnp.dot(a, b, preferred_element_type=F32)


def _dot_nt(a, b):
    return lax.dot_general(a, b, (((1,), (1,)), ((), ())), preferred_element_type=F32)


def _softmax_rows(s):
    m = jnp.max(s, axis=-1, keepdims=True)
    p = jnp.exp(s - m)
    return p * (1.0 / jnp.sum(p, axis=-1, keepdims=True))


def _lam_value(lq1, lk1, lq2, lk2):
    a = jnp.sum(lq1 * lk1, axis=-1, keepdims=True)
    b = jnp.sum(lq2 * lk2, axis=-1, keepdims=True)
    return jnp.exp(a) - jnp.exp(b) + LAM_INIT


def _norm_proj_kernel(x_ref, g_ref, w_ref, *out_refs, groups):
    h = _rmsnorm(x_ref[...], g_ref[...]).astype(BF16)
    k = 0
    for start, size, outs in groups:
        y = _dot(h, w_ref[:, start:start + size])
        for scale, key_block in outs:
            o_ref = out_refs[k]
            k += 1
            ys = y if scale == 1.0 else y * scale
            if key_block == HEAD_ROWS:
                n = ys.shape[0]
                for hd in range(N_HEADS):
                    o_ref[pl.ds(hd, n, stride=N_HEADS), :] = ys[:, hd * V_DIM:(hd + 1) * V_DIM].astype(o_ref.dtype)
            elif key_block:
                yt = ys.T
                for b in range(o_ref.shape[0]):
                    o_ref[b] = yt[:, b * key_block:(b + 1) * key_block].astype(o_ref.dtype)
            else:
                o_ref[...] = ys.astype(o_ref.dtype)


def _norm_proj_specs(rows, d, n_cols, groups, out_dtypes, tm):
    out_shape, out_specs = [], []
    flat_outs = [(size, kb) for _, size, outs in groups for _, kb in outs]
    for (size, kb), dt in zip(flat_outs, out_dtypes):
        if kb == HEAD_ROWS:
            out_shape.append(jax.ShapeDtypeStruct((rows * N_HEADS, V_DIM), dt))
            out_specs.append(pl.BlockSpec((tm * N_HEADS, V_DIM), lambda i, *_: (i, 0)))
        elif kb:
            out_shape.append(jax.ShapeDtypeStruct((rows // kb, size, kb), dt))
            out_specs.append(pl.BlockSpec((tm // kb, size, kb), lambda i, *_: (i, 0, 0)))
        else:
            out_shape.append(jax.ShapeDtypeStruct((rows, size), dt))
            out_specs.append(pl.BlockSpec((tm, size), lambda i, *_: (i, 0)))
    in_specs = [pl.BlockSpec((tm, d), lambda i, *_: (i, 0)),
                pl.BlockSpec((1, d), lambda i, *_: (0, 0)),
                pl.BlockSpec((d, n_cols), lambda i, *_: (0, 0), pipeline_mode=pl.Buffered(1))]
    return in_specs, out_specs, out_shape


def _norm_proj(x2d, g, w_bf16, groups, out_dtypes, tm):
    rows, d = x2d.shape
    in_specs, out_specs, out_shape = _norm_proj_specs(rows, d, w_bf16.shape[1], groups, out_dtypes, tm)
    return pl.pallas_call(
        functools.partial(_norm_proj_kernel, groups=groups),
        grid=(rows // tm,),
        in_specs=in_specs,
        out_specs=out_specs,
        out_shape=out_shape,
        compiler_params=_cparams("parallel"),
        name="norm_proj",
    )(x2d, g.reshape(1, d), w_bf16)


def _rel_bucket(rel):
    n = jnp.maximum(rel, 0)
    max_exact = N_BUCKETS // 2
    nf = jnp.maximum(n, 1).astype(F32)
    large = max_exact + jnp.floor(jnp.log(nf / max_exact) / math.log(MAX_DISTANCE / max_exact)
                                  * (N_BUCKETS - max_exact)).astype(jnp.int32)
    large = jnp.minimum(large, N_BUCKETS - 1)
    return jnp.where(n < max_exact, n, large)


def _bias_from_rel(rel, rb_ref, head, visible):
    bucket = _rel_bucket(rel)
    acc = jnp.zeros(rel.shape, F32)
    for b in range(N_BUCKETS):
        acc = jnp.where(bucket == b, rb_ref[b, head], acc)
    return jnp.where(visible, acc, NEG_INF)


def _bias_kernel(rb_ref, pb_ref, sb_ref, *, tq, tk, past, n_new):
    n_kinds = pb_ref.shape[1]
    key = lax.broadcasted_iota(jnp.int32, (tk, tq), 0)
    qry = lax.broadcasted_iota(jnp.int32, (tk, tq), 1)
    for kind in range(n_kinds):
        rel = kind * tq + qry - key
        for h in range(N_HEADS):
            if kind * tq - (tk - 1) >= MAX_DISTANCE:
                tile = jnp.full((tk, tq), rb_ref[N_BUCKETS - 1, h], F32) * LOG2E
            else:
                tile = _bias_from_rel(rel, rb_ref, h, rel >= 0) * LOG2E
            pb_ref[h, kind, :, :tq] = tile
            pb_ref[h, kind, :, tq:] = tile
    rows_per_head = 2 * n_new
    n_cols = sb_ref.shape[1]
    qi = lax.broadcasted_iota(jnp.int32, (rows_per_head, n_cols), 0) % n_new
    key = lax.broadcasted_iota(jnp.int32, (rows_per_head, n_cols), 1)
    rel = past + qi - key
    for h in range(N_HEADS):
        sb_ref[h * rows_per_head:(h + 1) * rows_per_head, :] = _bias_from_rel(rel, rb_ref, h, rel >= 0)


def _bias_tables(rel_bias, tq, tk, past, n_new):
    n_kinds = -(-(tk + MAX_DISTANCE - 1) // tq) + 1
    n_cols = past + SAMPLE_KPAD
    return pl.pallas_call(
        functools.partial(_bias_kernel, tq=tq, tk=tk, past=past, n_new=n_new),
        in_specs=[pl.BlockSpec(memory_space=pltpu.SMEM)],
        out_specs=[pl.BlockSpec(memory_space=pltpu.VMEM), pl.BlockSpec(memory_space=pltpu.VMEM)],
        out_shape=[jax.ShapeDtypeStruct((N_HEADS, n_kinds, tk, 2 * tq), F32),
                   jax.ShapeDtypeStruct((N_HEADS * 2 * n_new, n_cols), F32)],
        compiler_params=pltpu.CompilerParams(vmem_limit_bytes=VMEM_LIMIT),
        name="bias_tables",
    )(rel_bias)


def _subln(o, g):
    return _rmsnorm(o, g) * SUBLN_SCALE


def _two_map_queries(q_h):
    lane = lax.broadcasted_iota(jnp.int32, q_h.shape, 1)
    zero = jnp.zeros_like(q_h)
    return jnp.concatenate([jnp.where(lane < HEAD_DIM, q_h, zero), jnp.where(lane >= HEAD_DIM, q_h, zero)], axis=0)


def _prompt_attn_kernel(lq1_ref, lk1_ref, lq2_ref, lk2_ref, q_ref, k_ref, vt_ref, bias_ref, g_ref, o_ref,
                        q2_ref, m_ref, l_ref, acc_ref, s0_ref, *, tq, tk):
    i = pl.program_id(1)
    lam = _lam_value(lq1_ref[...], lk1_ref[...], lq2_ref[...], lk2_ref[...])
    n_kinds = bias_ref.shape[1]
    for h in range(N_HEADS):
        q2_ref[h] = _two_map_queries(q_ref[:, h * V_DIM:(h + 1) * V_DIM])
    m_ref[...] = jnp.full(m_ref.shape, NEG_INF, F32)
    l_ref[...] = jnp.zeros(l_ref.shape, F32)
    acc_ref[...] = jnp.zeros(acc_ref.shape, F32)

    last = (i * tq + tq - 1) // tk

    def scores(j, h):
        r0 = pl.multiple_of(j * tk, tk)
        kind = jnp.minimum((i * tq - j * tk) // tq, n_kinds - 1)
        hs = slice(h * V_DIM, (h + 1) * V_DIM)
        return _dot_nt(k_ref[pl.ds(r0, tk), hs], q2_ref[h]) + bias_ref[h, kind]

    ahead = 2
    for h in range(ahead):
        s0_ref[h] = scores(0, h)

    ones_rows = jnp.ones((16, tk), BF16)

    def body(j, carry):
        pending = [s0_ref[h] for h in range(ahead)]
        j_next = jnp.minimum(j + 1, last)
        for h in range(N_HEADS):
            hs = slice(h * V_DIM, (h + 1) * V_DIM)
            s = pending.pop(0)
            pending.append(scores(j, h + ahead) if h + ahead < N_HEADS else scores(j_next, h + ahead - N_HEADS))
            m_old = m_ref[h]
            m_new = jnp.maximum(m_old, jnp.max(s, axis=0, keepdims=True))
            alpha = jnp.exp2(m_old - m_new)
            p = jnp.exp2(s - m_new).astype(BF16)
            pv = _dot(jnp.concatenate([vt_ref[j, hs, :], ones_rows], axis=0), p)
            l_ref[h] = alpha * l_ref[h] + pv[V_DIM:V_DIM + 1]
            m_ref[h] = m_new
            acc_ref[h] = alpha * acc_ref[h] + pv[:V_DIM]
        for h in range(ahead):
            s0_ref[h] = pending[h]
        return carry

    lax.fori_loop(0, last + 1, body, 0)
    for h in range(N_HEADS):
        out_t = acc_ref[h] / l_ref[h]
        o_t = out_t[:, :tq] - lam * out_t[:, tq:]
        o_ref[:, h * V_DIM:(h + 1) * V_DIM] = _subln(o_t.T, g_ref[...]).astype(o_ref.dtype)


def _prompt_attn(lams, q, k, vt, bias, subln_g, batch, seq, tq, tk):
    vec = pl.BlockSpec((1, HEAD_DIM), lambda b, i: (0, 0))
    nq = seq // tq
    return pl.pallas_call(
        functools.partial(_prompt_attn_kernel, tq=tq, tk=tk),
        grid=(batch, nq),
        in_specs=[vec, vec, vec, vec,
                  pl.BlockSpec((tq, ATTN_WIDTH), lambda b, i: (b * nq + i, 0)),
                  pl.BlockSpec((seq, ATTN_WIDTH), lambda b, i: (b, 0)),
                  pl.BlockSpec((seq // tk, ATTN_WIDTH, tk), lambda b, i: (b, 0, 0)),
                  pl.BlockSpec(bias.shape, lambda b, i: (0, 0, 0, 0), pipeline_mode=pl.Buffered(1)),
                  pl.BlockSpec((1, V_DIM), lambda b, i: (0, 0))],
        out_specs=pl.BlockSpec((tq, ATTN_WIDTH), lambda b, i: (b * nq + i, 0)),
        out_shape=jax.ShapeDtypeStruct((batch * seq, ATTN_WIDTH), BF16),
        scratch_shapes=[pltpu.VMEM((N_HEADS, 2 * tq, V_DIM), BF16),
                        pltpu.VMEM((N_HEADS, 1, 2 * tq), F32),
                        pltpu.VMEM((N_HEADS, 1, 2 * tq), F32),
                        pltpu.VMEM((N_HEADS, V_DIM, 2 * tq), F32),
                        pltpu.VMEM((2, tk, 2 * tq), F32)],
        compiler_params=_cparams("parallel", "parallel"),
        name="prompt_attn",
    )(*lams, q, k, vt, bias, subln_g.reshape(1, V_DIM))


def _pool_branch(ext, u, pos0, wbd, scale):
    n, c = u.shape
    p = ext.shape[0] - n
    s2 = ext + pltpu.roll(ext, 1, 0)
    s4 = s2 + pltpu.roll(s2, 2, 0)
    s8 = s4 + pltpu.roll(s4, 4, 0)
    s16 = s8 + pltpu.roll(s8, 8, 0)
    lane = lax.broadcasted_iota(jnp.int32, (n, c), 1)
    grp = lane // POOL_GROUP_DIM
    win = jnp.where(grp == 0, s2[p:], jnp.where(grp == 1, s4[p:], jnp.where(grp == 2, s8[p:], s16[p:])))
    width = jnp.where(grp == 0, 2, jnp.where(grp == 1, 4, jnp.where(grp == 2, 8, 16)))
    pos = pos0 + lax.broadcasted_iota(jnp.int32, (n, c), 0)
    cnt = jnp.minimum(pos + 1, width).astype(F32)
    d = win / cnt - u
    return _dot(d.astype(BF16), wbd) * scale


def _prompt_side_kernel(u_ref, uprev_ref, qm_ref, mkt_ref, mvt_ref, wbd_ref, ps_ref, po_ref, om_ref, *, tm):
    i = pl.program_id(1)
    u = u_ref[...]
    prev = jnp.where(i > 0, uprev_ref[...], 0.0)
    ext = jnp.concatenate([prev, u], axis=0)
    po_ref[...] = _pool_branch(ext, u, i * tm, wbd_ref[...], ps_ref[...]).astype(po_ref.dtype)
    heads = [slice(h * MEM_HEAD_DIM, (h + 1) * MEM_HEAD_DIM) for h in range(N_HEADS)]
    scores = [_dot(qm_ref[:, sl], mkt_ref[sl, :].astype(BF16)) for sl in heads]
    probs = [_softmax_rows(s).astype(BF16) for s in scores]
    for sl, p in zip(heads, probs):
        om_ref[:, sl] = _dot_nt(p, mvt_ref[sl, :].astype(BF16)).astype(om_ref.dtype)


def _prompt_side(u, qm, mk, mv, wbd, pool_scale, batch, seq, tm):
    nblk = seq // tm
    halo = 16
    row = lambda b, i: (b * nblk + i, 0)
    return pl.pallas_call(
        functools.partial(_prompt_side_kernel, tm=tm),
        grid=(batch, nblk),
        in_specs=[pl.BlockSpec((tm, POOL_WIDTH), row),
                  pl.BlockSpec((halo, POOL_WIDTH),
                               lambda b, i: (jnp.maximum((b * nblk + i) * (tm // halo) - 1, 0), 0)),
                  pl.BlockSpec((tm, MEM_WIDTH), row),
                  pl.BlockSpec((None, MEM_WIDTH, N_MEM), lambda b, i: (b, 0, 0)),
                  pl.BlockSpec((None, MEM_WIDTH, N_MEM), lambda b, i: (b, 0, 0)),
                  pl.BlockSpec((POOL_WIDTH, POOL_WIDTH), lambda b, i: (0, 0)),
                  pl.BlockSpec((1, POOL_WIDTH), lambda b, i: (0, 0))],
        out_specs=[pl.BlockSpec((tm, POOL_WIDTH), row), pl.BlockSpec((tm, MEM_WIDTH), row)],
        out_shape=[jax.ShapeDtypeStruct((batch * seq, POOL_WIDTH), BF16),
                   jax.ShapeDtypeStruct((batch * seq, MEM_WIDTH), BF16)],
        compiler_params=_cparams("parallel", "parallel"),
        name="prompt_side",
    )(u, u, qm, mk, mv, wbd, pool_scale.reshape(1, POOL_WIDTH))


def _sample_sequence(seq, n_seq, pt_ref, lq1_ref, lk1_ref, lq2_ref, lk2_ref, q_ref, kn_ref, vn_ref, bias_ref, g_ref,
                     qm_ref, mkt_ref, mvt_ref, u_ref, st_ref, wbd_ref, ps_ref, ck_hbm, cv_hbm,
                     o_ref, om_ref, po_ref, s_ref, kbuf, vbuf, sem, *, n_pages, n_new, past):
    slot = seq % PAGE_SLOTS
    lam = _lam_value(lq1_ref[...], lk1_ref[...], lq2_ref[...], lk2_ref[...])
    page = kbuf.shape[2] // N_HEADS
    n_tail = s_ref.shape[1] - n_pages * page
    pad = jnp.zeros((n_tail - n_new, V_DIM), F32)
    hr = 2 * n_new
    caches = ((ck_hbm, kbuf), (cv_hbm, vbuf))

    def page_copy(which, slot_, p, page_id):
        hbm, buf = caches[which]
        return pltpu.make_async_copy(hbm.at[page_id], buf.at[slot_, p], sem.at[slot_, which])

    def start_pages(seq_, slot_):
        for p in range(n_pages):
            page_id = pt_ref[seq_, p]
            page_copy(0, slot_, p, page_id).start()
            page_copy(1, slot_, p, page_id).start()

    def wait_pages(which):
        for p in range(n_pages):
            page_copy(which, slot, p, 0).wait()

    ahead = PAGE_SLOTS - 1

    @pl.when(seq == 0)
    def _():
        for s0 in range(ahead):
            start_pages(s0, s0)

    @pl.when(seq + ahead < n_seq)
    def _():
        start_pages(seq + ahead, (seq + ahead) % PAGE_SLOTS)

    def page_pair(buf, c, h):
        rows = pl.ds(h, page, stride=N_HEADS)
        return jnp.concatenate([buf[slot, c, rows, :], buf[slot, c + 1, rows, :]], axis=0).astype(BF16)

    def head_rows(ref, h):
        return ref[pl.ds(h, n_new, stride=N_HEADS), :]

    wait_pages(0)
    for h in range(N_HEADS):
        q2 = _two_map_queries(q_ref[:, h * V_DIM:(h + 1) * V_DIM]).astype(BF16)
        for c in range(0, n_pages, 2):
            s_ref[h * hr:(h + 1) * hr, c * page:(c + 2) * page] = _dot_nt(q2, page_pair(kbuf, c, h))
        k_tail = jnp.concatenate([head_rows(kn_ref, h), pad], axis=0).astype(BF16)
        s_ref[h * hr:(h + 1) * hr, n_pages * page:] = _dot_nt(q2, k_tail)
    prob = _softmax_rows(s_ref[...] + bias_ref[...])
    wait_pages(1)
    for h in range(N_HEADS):
        a = prob[h * hr:h * hr + n_new] - lam * prob[h * hr + n_new:(h + 1) * hr]
        v_tail = jnp.concatenate([head_rows(vn_ref, h), pad], axis=0).astype(BF16)
        o_h = _dot(a[:, n_pages * page:].astype(BF16), v_tail)
        for c in range(0, n_pages, 2):
            o_h = o_h + _dot(a[:, c * page:(c + 2) * page].astype(BF16), page_pair(vbuf, c, h))
        o_ref[:, h * V_DIM:(h + 1) * V_DIM] = _subln(o_h, g_ref[...]).astype(o_ref.dtype)
    qm = qm_ref[...]
    head_of_lane = lax.broadcasted_iota(jnp.int32, qm.shape, 1) // MEM_HEAD_DIM
    qm4 = jnp.concatenate([jnp.where(head_of_lane == h, qm, 0.0) for h in range(N_HEADS)], axis=0).astype(BF16)
    pm = _softmax_rows(_dot(qm4, mkt_ref[...].astype(BF16)))
    om_all = _dot_nt(pm.astype(BF16), mvt_ref[...].astype(BF16))
    om = jnp.zeros(qm.shape, F32)
    for h in range(N_HEADS):
        om = jnp.where(head_of_lane == h, om_all[h * n_new:(h + 1) * n_new], om)
    om_ref[...] = om.astype(om_ref.dtype)
    u = u_ref[...]
    ext = jnp.concatenate([st_ref[...], u], axis=0)
    po_ref[...] = _pool_branch(ext, u, past, wbd_ref[...], ps_ref[...]).astype(po_ref.dtype)


def _proj_and_sample_kernel(pt_ref, x_ref, g1_ref, w_ref, lq1_ref, lk1_ref, lq2_ref, lk2_ref, q_ref, kn_ref, vn_ref,
                            bias_ref, g_ref, qm_ref, mkt_ref, mvt_ref, u_ref, st_ref, wbd_ref, ps_ref, ck_hbm, cv_hbm,
                            *refs, groups, n_proj_out, seqs_per_step, n_seq, n_pages, n_new, past):
    proj_out = refs[:n_proj_out]
    o_ref, om_ref, po_ref, s_ref, kbuf, vbuf, sem = refs[n_proj_out:]
    _norm_proj_kernel(x_ref, g1_ref, w_ref, *proj_out, groups=groups)
    first = pl.program_id(0) * seqs_per_step
    for j in range(seqs_per_step):
        _sample_sequence(first + j, n_seq, pt_ref, lq1_ref, lk1_ref, lq2_ref, lk2_ref, q_ref.at[j], kn_ref.at[j],
                         vn_ref.at[j], bias_ref, g_ref, qm_ref.at[j], mkt_ref.at[j], mvt_ref.at[j], u_ref.at[j],
                         st_ref.at[j], wbd_ref, ps_ref, ck_hbm, cv_hbm, o_ref.at[j], om_ref.at[j], po_ref.at[j],
                         s_ref, kbuf, vbuf, sem, n_pages=n_pages, n_new=n_new, past=past)


def _proj_and_sample(x2d, norm_g, w_bf16, groups, out_dtypes, tm, page_table, lams, q, k_new, v_new, bias, subln_g,
                     qm, mem_kt, mem_vt, u, state16, wbd, pool_scale, cache_k, cache_v):
    rows, d = x2d.shape
    n_steps = rows // tm
    n_seq, n_pages = page_table.shape
    n_new = q.shape[1]
    page_rows = cache_k.shape[1]
    past = n_pages * page_rows // N_HEADS
    assert n_pages % 2 == 0 and n_seq >= PAGE_SLOTS and n_seq % n_steps == 0
    sps = n_seq // n_steps
    proj_in, proj_out_specs, proj_out_shape = _norm_proj_specs(rows, d, w_bf16.shape[1], groups, out_dtypes, tm)
    vec = pl.BlockSpec((1, HEAD_DIM), lambda i, pt: (0, 0))
    per_seq = lambda a: pl.BlockSpec((sps,) + a.shape[1:], lambda i, pt: (i,) + (0,) * (a.ndim - 1))
    const = lambda a: pl.BlockSpec(a.shape, lambda i, pt: (0,) * a.ndim)
    hbm = pl.BlockSpec(memory_space=pl.ANY)
    g = subln_g.reshape(1, V_DIM)
    ps = pool_scale.reshape(1, POOL_WIDTH)
    in_specs = proj_in + [vec, vec, vec, vec, per_seq(q), per_seq(k_new), per_seq(v_new), const(bias), const(g),
                          per_seq(qm), per_seq(mem_kt), per_seq(mem_vt), per_seq(u), per_seq(state16), const(wbd),
                          const(ps), hbm, hbm]
    out3 = lambda width: jax.ShapeDtypeStruct((n_seq, n_new, width), F32)
    out_spec = lambda width: pl.BlockSpec((sps, n_new, width), lambda i, pt: (i, 0, 0))
    page_buffers = pltpu.VMEM((PAGE_SLOTS, n_pages, page_rows, V_DIM), F32)
    grid_spec = pltpu.PrefetchScalarGridSpec(
        num_scalar_prefetch=1,
        grid=(n_steps,),
        in_specs=in_specs,
        out_specs=proj_out_specs + [out_spec(ATTN_WIDTH), out_spec(MEM_WIDTH), out_spec(POOL_WIDTH)],
        scratch_shapes=[pltpu.VMEM(bias.shape, F32), page_buffers, page_buffers,
                        pltpu.SemaphoreType.DMA((PAGE_SLOTS, 2))],
    )
    outs = pl.pallas_call(
        functools.partial(_proj_and_sample_kernel, groups=groups, n_proj_out=len(proj_out_shape), seqs_per_step=sps,
                          n_seq=n_seq, n_pages=n_pages, n_new=n_new, past=past),
        grid_spec=grid_spec,
        out_shape=proj_out_shape + [out3(ATTN_WIDTH), out3(MEM_WIDTH), out3(POOL_WIDTH)],
        compiler_params=_cparams("arbitrary"),
        name="proj_and_sample",
    )(page_table, x2d, norm_g.reshape(1, d), w_bf16, *lams, q, k_new, v_new, bias, g, qm, mem_kt, mem_vt, u, state16,
      wbd, ps, cache_k, cache_v)
    return outs[:len(proj_out_shape)], outs[len(proj_out_shape):]


def _merge_kernel(x_ref, o_ref, po_ref, om_ref, g_ref, wgl_ref, wa_ref, wp_ref, wm_ref, wout_ref, x1_ref):
    x = x_ref[...]
    h = _rmsnorm(x, g_ref[...]).astype(BF16)
    d = D_MODEL
    merged = jax.nn.sigmoid(_dot(h, wgl_ref[:, 0:d])) * _dot(o_ref[...].astype(BF16), wa_ref[...])
    merged = merged + jax.nn.sigmoid(_dot(h, wgl_ref[:, d:2 * d])) * _dot(po_ref[...].astype(BF16), wp_ref[...])
    merged = merged + jax.nn.sigmoid(_dot(h, wgl_ref[:, 2 * d:3 * d])) * _dot(om_ref[...].astype(BF16), wm_ref[...])
    x1_ref[...] = x + _dot(merged.astype(BF16), wout_ref[...])


def _merge(x2d, o, po, om, norm_g, wgl, wa, wp, wm, wout, tm):
    rows, d = x2d.shape
    row = lambda width: pl.BlockSpec((tm, width), lambda i: (i, 0))
    full = lambda a: pl.BlockSpec(a.shape, lambda i: (0, 0), pipeline_mode=pl.Buffered(1))
    g = norm_g.reshape(1, d)
    return pl.pallas_call(
        _merge_kernel,
        grid=(rows // tm,),
        in_specs=[row(d), row(ATTN_WIDTH), row(POOL_WIDTH), row(MEM_WIDTH), full(g), full(wgl), full(wa), full(wp),
                  full(wm), full(wout)],
        out_specs=row(d),
        out_shape=jax.ShapeDtypeStruct((rows, d), F32),
        compiler_params=_cparams("parallel"),
        name="merge",
    )(x2d, o, po, om, g, wgl, wa, wp, wm, wout)


def _conv_ffn_kernel(x_ref, g2_ref, wg_ref, wu_ref, cw_ref, cb_ref, wd_ref, gf_ref, *refs, tm, fc, seq_rows):
    if seq_rows:
        st0_ref, st1_ref, y_ref, t0_ref, t1_ref, act_ref, pre_ref, gate_ref = refs
        n_seqs = tm // seq_rows
        lanes = pre_ref.shape[2]
        pre_ref[...] = jnp.zeros_like(pre_ref)
        for j in range(D_FF // lanes):
            pre_ref[j, pl.ds(0, n_seqs, stride=seq_rows), :] = st0_ref[:, j * lanes:(j + 1) * lanes]
            pre_ref[j, pl.ds(1, n_seqs, stride=seq_rows), :] = st1_ref[:, j * lanes:(j + 1) * lanes]
    else:
        y_ref, tail_ref, carry_ref, act_ref = refs

        @pl.when(pl.program_id(1) == 0)
        def _():
            carry_ref[...] = jnp.zeros_like(carry_ref)

    x = x_ref[...]
    h = _rmsnorm(x, g2_ref[...]).astype(BF16)
    for c0 in range(0, D_FF, fc):
        cs = slice(c0, c0 + fc)
        gate = _dot(h, wg_ref[:, cs])
        up = _dot(h, wu_ref[:, cs])
        if seq_rows:
            slabs = range(c0 // lanes, (c0 + fc) // lanes)
            pre = jnp.concatenate([pre_ref[j] for j in slabs], axis=1)
            t = lax.broadcasted_iota(jnp.int32, (tm, fc), 0) % seq_rows
            back1 = jnp.where(t >= 1, pltpu.roll(gate, 1, 0), pltpu.roll(pre, tm - 1, 0))
            back2 = jnp.where(t >= 2, pltpu.roll(gate, 2, 0), pre)
            for j in slabs:
                gate_ref[j] = gate[:, j * lanes - c0:(j + 1) * lanes - c0]
        else:
            ext = jnp.concatenate([carry_ref[:, cs], gate], axis=0)
            back1 = pltpu.roll(ext, 1, 0)[8:]
            back2 = pltpu.roll(ext, 2, 0)[8:]
            carry_ref[:, cs] = gate[tm - 8:]
        gc = cb_ref[:, cs] + cw_ref[0:1, cs] * back2
        gc = gc + cw_ref[1:2, cs] * back1
        gc = gc + cw_ref[2:3, cs] * gate
        act_ref[:, cs] = (jax.nn.gelu(gc) * up).astype(BF16)
    if seq_rows:
        for j in range(D_FF // lanes):
            t0_ref[:, j * lanes:(j + 1) * lanes] = gate_ref[j, pl.ds(seq_rows - 2, n_seqs, stride=seq_rows), :]
            t1_ref[:, j * lanes:(j + 1) * lanes] = gate_ref[j, pl.ds(seq_rows - 1, n_seqs, stride=seq_rows), :]
    else:
        tail_ref[...] = carry_ref[...]
    y_ref[...] = _rmsnorm(x + _dot(act_ref[...], wd_ref[...]), gf_ref[...])


def _conv_ffn(x1, norm_g, wg, wu, conv_w, conv_b, wd, final_g, tm, batch=None, seq=None, prefix=None):
    rows, d = x1.shape
    g2 = norm_g.reshape(1, d)
    gf = final_g.reshape(1, d)
    cb = conv_b.reshape(1, D_FF)
    if prefix is None:
        nblk = seq // tm
        row = lambda width: pl.BlockSpec((tm, width), lambda b, i: (b * nblk + i, 0))
        full = lambda a: pl.BlockSpec(a.shape, lambda b, i: (0, 0), pipeline_mode=pl.Buffered(1))
        return pl.pallas_call(
            functools.partial(_conv_ffn_kernel, tm=tm, fc=FFN_CHUNK, seq_rows=0),
            grid=(batch, nblk),
            in_specs=[row(d), full(g2), full(wg), full(wu), full(conv_w), full(cb), full(wd), full(gf)],
            out_specs=[row(d), pl.BlockSpec((None, 8, D_FF), lambda b, i: (b, 0, 0))],
            out_shape=[jax.ShapeDtypeStruct((rows, d), F32), jax.ShapeDtypeStruct((batch, 8, D_FF), F32)],
            scratch_shapes=[pltpu.VMEM((8, D_FF), F32), pltpu.VMEM((tm, D_FF), BF16)],
            compiler_params=_cparams("parallel", "arbitrary"),
            name="conv_ffn_prompt",
        )(x1, g2, wg, wu, conv_w, cb, wd, gf)
    seq_rows = 8
    row = lambda width: pl.BlockSpec((tm, width), lambda i: (i, 0))
    per_seq = pl.BlockSpec((tm // seq_rows, D_FF), lambda i: (i, 0))
    full = lambda a: pl.BlockSpec(a.shape, lambda i: (0, 0), pipeline_mode=pl.Buffered(1))
    tails = jax.ShapeDtypeStruct((rows // seq_rows, D_FF), F32)
    return pl.pallas_call(
        functools.partial(_conv_ffn_kernel, tm=tm, fc=FFN_CHUNK, seq_rows=seq_rows),
        grid=(rows // tm,),
        in_specs=[row(d), full(g2), full(wg), full(wu), full(conv_w), full(cb), full(wd), full(gf), per_seq, per_seq],
        out_specs=[row(d), per_seq, per_seq],
        out_shape=[jax.ShapeDtypeStruct((rows, d), F32), tails, tails],
        scratch_shapes=[pltpu.VMEM((tm, D_FF), BF16), pltpu.VMEM((D_FF // LANES, tm, LANES), F32),
                        pltpu.VMEM((D_FF // LANES, tm, LANES), F32)],
        compiler_params=_cparams("parallel"),
        name="conv_ffn_sample",
    )(x1, g2, wg, wu, conv_w, cb, wd, gf, *prefix)


def kernel(x_prompt, x_sample, mem_prompt, cache_k, cache_v, page_table, state_pool, state_ffn_conv, cache_mem_k, cache_mem_v, norm1_g, w_in, lam_q1, lam_k1, lam_q2, lam_k2, subln_g, w_pool_grp, pool_scale, w_br_attn, w_br_pool, w_br_mem, mem_norm_g, w_mem_kv, w_out, norm2_g, w_ffn_gate, w_ffn_up, ffn_conv_w, ffn_conv_b, w_ffn_down, rel_bias, final_norm_g):
    depth = w_in.shape[0]
    assert depth == 1, "single-layer step only"
    B, S, D = x_prompt.shape
    DB, DS, _ = x_sample.shape
    n_phys, page = cache_k.shape[1], cache_k.shape[2]
    n_pages = page_table.shape[1]
    past = n_pages * page
    assert DS == 8 and S % ROW_BLOCK == 0 and (DB * DS) % ROW_BLOCK == 0
    assert ATTN_TK % ATTN_TQ == 0 and ROW_BLOCK % ATTN_TK == 0 and S % ATTN_TK == 0

    w_qkv = w_in[0, :, :2048].astype(BF16)
    w_gl = w_in[0, :, 2048:].astype(BF16)
    wa, wp, wm = w_br_attn[0].astype(BF16), w_br_pool[0].astype(BF16), w_br_mem[0].astype(BF16)
    wout = w_out[0].astype(BF16)
    wg, wu, wd = w_ffn_gate[0].astype(BF16), w_ffn_up[0].astype(BF16), w_ffn_down[0].astype(BF16)
    w_mem = w_mem_kv[0].astype(BF16)
    wbd = jnp.zeros((POOL_WIDTH, POOL_WIDTH), F32)
    for gi in range(len(POOL_WINDOWS)):
        sl = slice(gi * POOL_GROUP_DIM, (gi + 1) * POOL_GROUP_DIM)
        wbd = wbd.at[sl, sl].set(w_pool_grp[0, gi])
    wbd = wbd.astype(BF16)
    lams = tuple(a[0].reshape(1, HEAD_DIM) for a in (lam_q1, lam_k1, lam_q2, lam_k2))

    prompt_bias, sample_bias = _bias_tables(rel_bias, ATTN_TQ, ATTN_TK, past, DS)

    xs = x_sample.reshape(DB * DS, D)
    groups_s = ((0, 512, ((QK_SCALE, 0),)), (512, 512, ((1.0, HEAD_ROWS),)), (1024, 512, ((1.0, HEAD_ROWS),)),
                (1536, 256, ((1.0, 0),)), (1792, 256, ((MEM_SCALE, 0),)))
    qs, ks, vs, us, qms = _norm_proj(xs, norm1_g[0], w_qkv, groups_s, (F32,) * 5, ROW_BLOCK)
    seq3 = lambda a: a.reshape(DB, DS, a.shape[-1])
    head_rows = lambda a: a.reshape(DB, DS * N_HEADS, V_DIM)
    state16 = jnp.pad(state_pool[0], ((0, 0), (16 - POOL_STATE, 0), (0, 0)))
    mem_t = lambda a: jnp.transpose(a[0], (0, 2, 3, 1)).reshape(DB, MEM_WIDTH, N_MEM)
    pages = lambda a: a[0].reshape(n_phys, page * N_HEADS, V_DIM)

    xp = x_prompt.reshape(B * S, D)
    groups = ((0, 512, ((QK_SCALE * LOG2E, 0),)), (512, 512, ((1.0, HEAD_ROWS), (1.0, 0))),
              (1024, 512, ((1.0, HEAD_ROWS), (1.0, ATTN_TK))), (1536, 256, ((1.0, 0),)), (1792, 256, ((MEM_SCALE, 0),)))
    (qp, kp, kp_b, vp, vp_t, up, qmp), (os_, oms, pos_) = _proj_and_sample(
        xp, norm1_g[0], w_qkv, groups, (BF16, F32, BF16, F32, BF16, F32, BF16), ROW_BLOCK,
        page_table, lams, seq3(qs), head_rows(ks), head_rows(vs), sample_bias, subln_g[0], seq3(qms),
        mem_t(cache_mem_k), mem_t(cache_mem_v), seq3(us), state16, wbd, pool_scale[0], pages(cache_k), pages(cache_v))
    mk_p, mv_p = _norm_proj(mem_prompt.reshape(B * N_MEM, D), mem_norm_g[0], w_mem,
                            ((0, MEM_WIDTH, ((1.0, N_MEM),)), (MEM_WIDTH, MEM_WIDTH, ((1.0, N_MEM),))), (F32, F32), ROW_BLOCK)
    op = _prompt_attn(lams, qp, kp_b, vp_t, prompt_bias, subln_g[0], B, S, ATTN_TQ, ATTN_TK)
    pop, omp = _prompt_side(up, qmp, mk_p, mv_p, wbd, pool_scale[0], B, S, ROW_BLOCK)
    x1p = _merge(xp, op, pop, omp, norm1_g[0], w_gl, wa, wp, wm, wout, ROW_BLOCK)
    yp, tail_p = _conv_ffn(x1p, norm2_g[0], wg, wu, ffn_conv_w[0], ffn_conv_b[0], wd, final_norm_g, ROW_BLOCK,
                           batch=B, seq=S)

    flat = lambda a: a.reshape(DB * DS, a.shape[-1])
    x1s = _merge(xs, flat(os_), flat(pos_), flat(oms), norm1_g[0], w_gl, wa, wp, wm, wout, ROW_BLOCK)
    ys, tail0_s, tail1_s = _conv_ffn(x1s, norm2_g[0], wg, wu, ffn_conv_w[0], ffn_conv_b[0], wd, final_norm_g, ROW_BLOCK,
                                     prefix=(state_ffn_conv[0, :, 0], state_ffn_conv[0, :, 1]))

    heads = lambda a, n: a.reshape(1, n, -1, N_HEADS, V_DIM)
    new_pool_p = up.reshape(B, S, POOL_WIDTH)[:, S - POOL_STATE:][None]
    new_pool_s = jnp.concatenate([state_pool[0][:, DS:], seq3(us)], axis=1)[None]
    new_conv_p = tail_p[:, 6:8][None]
    new_conv_s = jnp.stack([tail0_s, tail1_s], axis=1)[None]
    mem_heads = lambda a: jnp.transpose(a.reshape(B, N_HEADS, MEM_HEAD_DIM, N_MEM), (0, 3, 1, 2))[None]
    return (yp.reshape(B, S, D), ys.reshape(DB, DS, D), heads(kp, B), heads(vp, B), heads(ks, DB), heads(vs, DB),
            new_pool_p, new_pool_s, new_conv_p, new_conv_s, mem_heads(mk_p), mem_heads(mv_p))
```

```python
import functools
import math

import jax
import jax.numpy as jnp
from jax import lax
from jax.experimental import pallas as pl
from jax.experimental.pallas import tpu as pltpu

F32 = jnp.float32
BF16 = jnp.bfloat16

D_MODEL = 1024
N_HEADS = 4
HEAD_DIM = 64
V_DIM = 2 * HEAD_DIM
ATTN_WIDTH = N_HEADS * V_DIM
POOL_WINDOWS = (2, 4, 8, 16)
POOL_GROUP_DIM = 64
POOL_WIDTH = 256
POOL_STATE = 15
MEM_HEAD_DIM = 64
MEM_WIDTH = 256
N_MEM = 256
D_FF = 2816
N_BUCKETS = 32
MAX_DISTANCE = 128
EPS = 1e-6
NEG_INF = -1e30
QK_SCALE = HEAD_DIM ** -0.5
MEM_SCALE = MEM_HEAD_DIM ** -0.5
LAM_INIT = 0.8 - 0.6 * math.exp(-0.3 * 0)
SUBLN_SCALE = 1.0 - LAM_INIT
LOG2E = math.log2(math.e)

ATTN_TQ = 256
ATTN_TK = 256
ROW_BLOCK = 512
FFN_CHUNK = 256
SAMPLE_KPAD = 128
PAGE_SLOTS = 3
VMEM_LIMIT = 56 * 1024 * 1024
HEAD_ROWS = -1
LANES = 128


def _cparams(*sem):
    return pltpu.CompilerParams(dimension_semantics=sem, vmem_limit_bytes=VMEM_LIMIT)


def _rmsnorm(x, g):
    return x * lax.rsqrt(jnp.mean(x * x, axis=-1, keepdims=True) + EPS) * g


def _dot(a, b):
    return jnp.dot(a, b, preferred_element_type=F32)


def _dot_nt(a, b):
    return lax.dot_general(a, b, (((1,), (1,)), ((), ())), preferred_element_type=F32)


def _softmax_rows(s):
    m = jnp.max(s, axis=-1, keepdims=True)
    p = jnp.exp(s - m)
    return p * (1.0 / jnp.sum(p, axis=-1, keepdims=True))


def _lam_value(lq1, lk1, lq2, lk2):
    a = jnp.sum(lq1 * lk1, axis=-1, keepdims=True)
    b = jnp.sum(lq2 * lk2, axis=-1, keepdims=True)
    return jnp.exp(a) - jnp.exp(b) + LAM_INIT


def _norm_proj_kernel(x_ref, g_ref, w_ref, *out_refs, groups):
    h = _rmsnorm(x_ref[...], g_ref[...]).astype(BF16)
    k = 0
    for start, size, outs in groups:
        y = _dot(h, w_ref[:, start:start + size])
        for scale, key_block in outs:
            o_ref = out_refs[k]
            k += 1
            ys = y if scale == 1.0 else y * scale
            if key_block == HEAD_ROWS:
                n = ys.shape[0]
                for hd in range(N_HEADS):
                    o_ref[pl.ds(hd, n, stride=N_HEADS), :] = ys[:, hd * V_DIM:(hd + 1) * V_DIM].astype(o_ref.dtype)
            elif key_block:
                yt = ys.T
                for b in range(o_ref.shape[0]):
                    o_ref[b] = yt[:, b * key_block:(b + 1) * key_block].astype(o_ref.dtype)
            else:
                o_ref[...] = ys.astype(o_ref.dtype)


def _norm_proj_specs(rows, d, n_cols, groups, out_dtypes, tm):
    out_shape, out_specs = [], []
    flat_outs = [(size, kb) for _, size, outs in groups for _, kb in outs]
    for (size, kb), dt in zip(flat_outs, out_dtypes):
        if kb == HEAD_ROWS:
            out_shape.append(jax.ShapeDtypeStruct((rows * N_HEADS, V_DIM), dt))
            out_specs.append(pl.BlockSpec((tm * N_HEADS, V_DIM), lambda i, *_: (i, 0)))
        elif kb:
            out_shape.append(jax.ShapeDtypeStruct((rows // kb, size, kb), dt))
            out_specs.append(pl.BlockSpec((tm // kb, size, kb), lambda i, *_: (i, 0, 0)))
        else:
            out_shape.append(jax.ShapeDtypeStruct((rows, size), dt))
            out_specs.append(pl.BlockSpec((tm, size), lambda i, *_: (i, 0)))
    in_specs = [pl.BlockSpec((tm, d), lambda i, *_: (i, 0)),
                pl.BlockSpec((1, d), lambda i, *_: (0, 0)),
                pl.BlockSpec((d, n_cols), lambda i, *_: (0, 0), pipeline_mode=pl.Buffered(1))]
    return in_specs, out_specs, out_shape


def _norm_proj(x2d, g, w_bf16, groups, out_dtypes, tm):
    rows, d = x2d.shape
    in_specs, out_specs, out_shape = _norm_proj_specs(rows, d, w_bf16.shape[1], groups, out_dtypes, tm)
    return pl.pallas_call(
        functools.partial(_norm_proj_kernel, groups=groups),
        grid=(rows // tm,),
        in_specs=in_specs,
        out_specs=out_specs,
        out_shape=out_shape,
        compiler_params=_cparams("parallel"),
        name="norm_proj",
    )(x2d, g.reshape(1, d), w_bf16)


def _rel_bucket(rel):
    n = jnp.maximum(rel, 0)
    max_exact = N_BUCKETS // 2
    nf = jnp.maximum(n, 1).astype(F32)
    large = max_exact + jnp.floor(jnp.log(nf / max_exact) / math.log(MAX_DISTANCE / max_exact)
                                  * (N_BUCKETS - max_exact)).astype(jnp.int32)
    large = jnp.minimum(large, N_BUCKETS - 1)
    return jnp.where(n < max_exact, n, large)


def _bias_from_rel(rel, rb_ref, head, visible):
    bucket = _rel_bucket(rel)
    acc = jnp.zeros(rel.shape, F32)
    for b in range(N_BUCKETS):
        acc = jnp.where(bucket == b, rb_ref[b, head], acc)
    return jnp.where(visible, acc, NEG_INF)


def _bias_kernel(rb_ref, pb_ref, sb_ref, *, tq, tk, past, n_new):
    n_kinds = pb_ref.shape[1]
    key = lax.broadcasted_iota(jnp.int32, (tk, tq), 0)
    qry = lax.broadcasted_iota(jnp.int32, (tk, tq), 1)
    for kind in range(n_kinds):
        rel = kind * tq + qry - key
        for h in range(N_HEADS):
            if kind * tq - (tk - 1) >= MAX_DISTANCE:
                tile = jnp.full((tk, tq), rb_ref[N_BUCKETS - 1, h], F32) * LOG2E
            else:
                tile = _bias_from_rel(rel, rb_ref, h, rel >= 0) * LOG2E
            pb_ref[h, kind, :, :tq] = tile
            pb_ref[h, kind, :, tq:] = tile
    rows_per_head = 2 * n_new
    n_cols = sb_ref.shape[1]
    qi = lax.broadcasted_iota(jnp.int32, (rows_per_head, n_cols), 0) % n_new
    key = lax.broadcasted_iota(jnp.int32, (rows_per_head, n_cols), 1)
    rel = past + qi - key
    for h in range(N_HEADS):
        sb_ref[h * rows_per_head:(h + 1) * rows_per_head, :] = _bias_from_rel(rel, rb_ref, h, rel >= 0)


def _bias_tables(rel_bias, tq, tk, past, n_new):
    n_kinds = -(-(tk + MAX_DISTANCE - 1) // tq) + 1
    n_cols = past + SAMPLE_KPAD
    return pl.pallas_call(
        functools.partial(_bias_kernel, tq=tq, tk=tk, past=past, n_new=n_new),
        in_specs=[pl.BlockSpec(memory_space=pltpu.SMEM)],
        out_specs=[pl.BlockSpec(memory_space=pltpu.VMEM), pl.BlockSpec(memory_space=pltpu.VMEM)],
        out_shape=[jax.ShapeDtypeStruct((N_HEADS, n_kinds, tk, 2 * tq), F32),
                   jax.ShapeDtypeStruct((N_HEADS * 2 * n_new, n_cols), F32)],
        compiler_params=pltpu.CompilerParams(vmem_limit_bytes=VMEM_LIMIT),
        name="bias_tables",
    )(rel_bias)


def _subln(o, g):
    return _rmsnorm(o, g) * SUBLN_SCALE


def _two_map_queries(q_h):
    lane = lax.broadcasted_iota(jnp.int32, q_h.shape, 1)
    zero = jnp.zeros_like(q_h)
    return jnp.concatenate([jnp.where(lane < HEAD_DIM, q_h, zero), jnp.where(lane >= HEAD_DIM, q_h, zero)], axis=0)


def _prompt_attn_kernel(lq1_ref, lk1_ref, lq2_ref, lk2_ref, q_ref, k_ref, vt_ref, bias_ref, g_ref, o_ref,
                        q2_ref, m_ref, l_ref, acc_ref, s0_ref, *, tq, tk):
    i = pl.program_id(1)
    lam = _lam_value(lq1_ref[...], lk1_ref[...], lq2_ref[...], lk2_ref[...])
    n_kinds = bias_ref.shape[1]
    for h in range(N_HEADS):
        q2_ref[h] = _two_map_queries(q_ref[:, h * V_DIM:(h + 1) * V_DIM])
    m_ref[...] = jnp.full(m_ref.shape, NEG_INF, F32)
    l_ref[...] = jnp.zeros(l_ref.shape, F32)
    acc_ref[...] = jnp.zeros(acc_ref.shape, F32)

    last = (i * tq + tq - 1) // tk

    def scores(j, h):
        r0 = pl.multiple_of(j * tk, tk)
        kind = jnp.minimum((i * tq - j * tk) // tq, n_kinds - 1)
        hs = slice(h * V_DIM, (h + 1) * V_DIM)
        return _dot_nt(k_ref[pl.ds(r0, tk), hs], q2_ref[h]) + bias_ref[h, kind]

    ahead = 2
    for h in range(ahead):
        s0_ref[h] = scores(0, h)

    ones_rows = jnp.ones((16, tk), BF16)

    def body(j, carry):
        pending = [s0_ref[h] for h in range(ahead)]
        j_next = jnp.minimum(j + 1, last)
        for h in range(N_HEADS):
            hs = slice(h * V_DIM, (h + 1) * V_DIM)
            s = pending.pop(0)
            pending.append(scores(j, h + ahead) if h + ahead < N_HEADS else scores(j_next, h + ahead - N_HEADS))
            m_old = m_ref[h]
            m_new = jnp.maximum(m_old, jnp.max(s, axis=0, keepdims=True))
            alpha = jnp.exp2(m_old - m_new)
            p = jnp.exp2(s - m_new).astype(BF16)
            pv = _dot(jnp.concatenate([vt_ref[j, hs, :], ones_rows], axis=0), p)
            l_ref[h] = alpha * l_ref[h] + pv[V_DIM:V_DIM + 1]
            m_ref[h] = m_new
            acc_ref[h] = alpha * acc_ref[h] + pv[:V_DIM]
        for h in range(ahead):
            s0_ref[h] = pending[h]
        return carry

    lax.fori_loop(0, last + 1, body, 0)
    for h in range(N_HEADS):
        out_t = acc_ref[h] * (1.0 / l_ref[h])
        o_t = out_t[:, :tq] - lam * out_t[:, tq:]
        o_ref[:, h * V_DIM:(h + 1) * V_DIM] = _subln(o_t.T, g_ref[...]).astype(o_ref.dtype)


def _prompt_attn(lams, q, k, vt, bias, subln_g, batch, seq, tq, tk):
    vec = pl.BlockSpec((1, HEAD_DIM), lambda b, i: (0, 0))
    nq = seq // tq
    return pl.pallas_call(
        functools.partial(_prompt_attn_kernel, tq=tq, tk=tk),
        grid=(batch, nq),
        in_specs=[vec, vec, vec, vec,
                  pl.BlockSpec((tq, ATTN_WIDTH), lambda b, i: (b * nq + i, 0)),
                  pl.BlockSpec((seq, ATTN_WIDTH), lambda b, i: (b, 0)),
                  pl.BlockSpec((seq // tk, ATTN_WIDTH, tk), lambda b, i: (b, 0, 0)),
                  pl.BlockSpec(bias.shape, lambda b, i: (0, 0, 0, 0), pipeline_mode=pl.Buffered(1)),
                  pl.BlockSpec((1, V_DIM), lambda b, i: (0, 0))],
        out_specs=pl.BlockSpec((tq, ATTN_WIDTH), lambda b, i: (b * nq + i, 0)),
        out_shape=jax.ShapeDtypeStruct((batch * seq, ATTN_WIDTH), BF16),
        scratch_shapes=[pltpu.VMEM((N_HEADS, 2 * tq, V_DIM), BF16),
                        pltpu.VMEM((N_HEADS, 1, 2 * tq), F32),
                        pltpu.VMEM((N_HEADS, 1, 2 * tq), F32),
                        pltpu.VMEM((N_HEADS, V_DIM, 2 * tq), F32),
                        pltpu.VMEM((2, tk, 2 * tq), F32)],
        compiler_params=_cparams("parallel", "parallel"),
        name="prompt_attn",
    )(*lams, q, k, vt, bias, subln_g.reshape(1, V_DIM))


def _pool_branch(ext, u, pos0, wbd, scale):
    n, c = u.shape
    p = ext.shape[0] - n
    s2 = ext + pltpu.roll(ext, 1, 0)
    s4 = s2 + pltpu.roll(s2, 2, 0)
    s8 = s4 + pltpu.roll(s4, 4, 0)
    s16 = s8 + pltpu.roll(s8, 8, 0)
    lane = lax.broadcasted_iota(jnp.int32, (n, c), 1)
    grp = lane // POOL_GROUP_DIM
    win = jnp.where(grp == 0, s2[p:], jnp.where(grp == 1, s4[p:], jnp.where(grp == 2, s8[p:], s16[p:])))
    width = jnp.where(grp == 0, 2, jnp.where(grp == 1, 4, jnp.where(grp == 2, 8, 16)))
    pos = pos0 + lax.broadcasted_iota(jnp.int32, (n, c), 0)
    cnt = jnp.minimum(pos + 1, width).astype(F32)
    d = win / cnt - u
    return _dot(d.astype(BF16), wbd) * scale


def _prompt_side_kernel(u_ref, uprev_ref, qm_ref, mkt_ref, mvt_ref, wbd_ref, ps_ref, po_ref, om_ref, *, tm):
    i = pl.program_id(1)
    u = u_ref[...]
    prev = jnp.where(i > 0, uprev_ref[...], 0.0)
    ext = jnp.concatenate([prev, u], axis=0)
    po_ref[...] = _pool_branch(ext, u, i * tm, wbd_ref[...], ps_ref[...]).astype(po_ref.dtype)
    heads = [slice(h * MEM_HEAD_DIM, (h + 1) * MEM_HEAD_DIM) for h in range(N_HEADS)]
    scores = [_dot(qm_ref[:, sl], mkt_ref[sl, :].astype(BF16)) for sl in heads]
    probs = [_softmax_rows(s).astype(BF16) for s in scores]
    for sl, p in zip(heads, probs):
        om_ref[:, sl] = _dot_nt(p, mvt_ref[sl, :].astype(BF16)).astype(om_ref.dtype)


def _prompt_side(u, qm, mk, mv, wbd, pool_scale, batch, seq, tm):
    nblk = seq // tm
    halo = 16
    row = lambda b, i: (b * nblk + i, 0)
    return pl.pallas_call(
        functools.partial(_prompt_side_kernel, tm=tm),
        grid=(batch, nblk),
        in_specs=[pl.BlockSpec((tm, POOL_WIDTH), row),
                  pl.BlockSpec((halo, POOL_WIDTH),
                               lambda b, i: (jnp.maximum((b * nblk + i) * (tm // halo) - 1, 0), 0)),
                  pl.BlockSpec((tm, MEM_WIDTH), row),
                  pl.BlockSpec((None, MEM_WIDTH, N_MEM), lambda b, i: (b, 0, 0)),
                  pl.BlockSpec((None, MEM_WIDTH, N_MEM), lambda b, i: (b, 0, 0)),
                  pl.BlockSpec((POOL_WIDTH, POOL_WIDTH), lambda b, i: (0, 0)),
                  pl.BlockSpec((1, POOL_WIDTH), lambda b, i: (0, 0))],
        out_specs=[pl.BlockSpec((tm, POOL_WIDTH), row), pl.BlockSpec((tm, MEM_WIDTH), row)],
        out_shape=[jax.ShapeDtypeStruct((batch * seq, POOL_WIDTH), BF16),
                   jax.ShapeDtypeStruct((batch * seq, MEM_WIDTH), BF16)],
        compiler_params=_cparams("parallel", "parallel"),
        name="prompt_side",
    )(u, u, qm, mk, mv, wbd, pool_scale.reshape(1, POOL_WIDTH))


def _page_copy(caches, sem, which, slot, p, page_id):
    hbm, buf = caches[which]
    return pltpu.make_async_copy(hbm.at[page_id], buf.at[slot, p], sem.at[slot, which])


def _start_page(caches, sem, pt_ref, seq, slot, p):
    page_id = pt_ref[seq, p]
    _page_copy(caches, sem, 0, slot, p, page_id).start()
    _page_copy(caches, sem, 1, slot, p, page_id).start()


def _wait_slot(caches, sem, which, slot, n_pages):
    for p in range(n_pages):
        _page_copy(caches, sem, which, slot, p, 0).wait()


def _sample_sequence(seq, n_seq, pt_ref, lq1_ref, lk1_ref, lq2_ref, lk2_ref, q_ref, kn_ref, vn_ref, bias_ref, g_ref,
                     qm_ref, mkt_ref, mvt_ref, u_ref, st_ref, wbd_ref, ps_ref, ck_hbm, cv_hbm,
                     o_ref, om_ref, po_ref, s_ref, kbuf, vbuf, sem, *, n_pages, n_new, past):
    slot = seq % PAGE_SLOTS
    lam = _lam_value(lq1_ref[...], lk1_ref[...], lq2_ref[...], lk2_ref[...])
    page = kbuf.shape[2] // N_HEADS
    n_tail = s_ref.shape[1] - n_pages * page
    pad = jnp.zeros((n_tail - n_new, V_DIM), F32)
    hr = 2 * n_new
    caches = ((ck_hbm, kbuf), (cv_hbm, vbuf))
    ahead = PAGE_SLOTS - 1

    @pl.when(seq == 0)
    def _():
        for s0 in range(ahead):
            for p in range(n_pages):
                _start_page(caches, sem, pt_ref, s0, s0, p)

    @pl.when(seq + ahead < n_seq)
    def _():
        for p in range(n_pages):
            _start_page(caches, sem, pt_ref, seq + ahead, (seq + ahead) % PAGE_SLOTS, p)

    def page_pair(buf, c, h):
        rows = pl.ds(h, page, stride=N_HEADS)
        return jnp.concatenate([buf[slot, c, rows, :], buf[slot, c + 1, rows, :]], axis=0).astype(BF16)

    def head_rows(ref, h):
        return ref[pl.ds(h, n_new, stride=N_HEADS), :]

    _wait_slot(caches, sem, 0, slot, n_pages)
    for h in range(N_HEADS):
        q2 = _two_map_queries(q_ref[:, h * V_DIM:(h + 1) * V_DIM]).astype(BF16)
        for c in range(0, n_pages, 2):
            s_ref[h * hr:(h + 1) * hr, c * page:(c + 2) * page] = _dot_nt(q2, page_pair(kbuf, c, h))
        k_tail = jnp.concatenate([head_rows(kn_ref, h), pad], axis=0).astype(BF16)
        s_ref[h * hr:(h + 1) * hr, n_pages * page:] = _dot_nt(q2, k_tail)
    prob = _softmax_rows(s_ref[...] + bias_ref[...])
    _wait_slot(caches, sem, 1, slot, n_pages)
    for h in range(N_HEADS):
        a = prob[h * hr:h * hr + n_new] - lam * prob[h * hr + n_new:(h + 1) * hr]
        v_tail = jnp.concatenate([head_rows(vn_ref, h), pad], axis=0).astype(BF16)
        o_h = _dot(a[:, n_pages * page:].astype(BF16), v_tail)
        for c in range(0, n_pages, 2):
            o_h = o_h + _dot(a[:, c * page:(c + 2) * page].astype(BF16), page_pair(vbuf, c, h))
        o_ref[:, h * V_DIM:(h + 1) * V_DIM] = _subln(o_h, g_ref[...]).astype(o_ref.dtype)
    qm = qm_ref[...]
    head_of_lane = lax.broadcasted_iota(jnp.int32, qm.shape, 1) // MEM_HEAD_DIM
    qm4 = jnp.concatenate([jnp.where(head_of_lane == h, qm, 0.0) for h in range(N_HEADS)], axis=0).astype(BF16)
    pm = _softmax_rows(_dot(qm4, mkt_ref[...].astype(BF16)))
    om_all = _dot_nt(pm.astype(BF16), mvt_ref[...].astype(BF16))
    om = jnp.zeros(qm.shape, F32)
    for h in range(N_HEADS):
        om = jnp.where(head_of_lane == h, om_all[h * n_new:(h + 1) * n_new], om)
    om_ref[...] = om.astype(om_ref.dtype)
    u = u_ref[...]
    ext = jnp.concatenate([st_ref[...], u], axis=0)
    po_ref[...] = _pool_branch(ext, u, past, wbd_ref[...], ps_ref[...]).astype(po_ref.dtype)


def _proj_and_sample_kernel(pt_ref, x_ref, g1_ref, w_ref, lq1_ref, lk1_ref, lq2_ref, lk2_ref, q_ref, kn_ref, vn_ref,
                            bias_ref, g_ref, qm_ref, mkt_ref, mvt_ref, u_ref, st_ref, wbd_ref, ps_ref, ck_hbm, cv_hbm,
                            *refs, groups, n_proj_out, seqs_per_step, n_seq, n_pages, n_new, past):
    proj_out = refs[:n_proj_out]
    o_ref, om_ref, po_ref, s_ref, kbuf, vbuf, sem = refs[n_proj_out:]
    _norm_proj_kernel(x_ref, g1_ref, w_ref, *proj_out, groups=groups)
    first = pl.program_id(0) * seqs_per_step
    for j in range(seqs_per_step):
        _sample_sequence(first + j, n_seq, pt_ref, lq1_ref, lk1_ref, lq2_ref, lk2_ref, q_ref.at[j], kn_ref.at[j],
                         vn_ref.at[j], bias_ref, g_ref, qm_ref.at[j], mkt_ref.at[j], mvt_ref.at[j], u_ref.at[j],
                         st_ref.at[j], wbd_ref, ps_ref, ck_hbm, cv_hbm, o_ref.at[j], om_ref.at[j], po_ref.at[j],
                         s_ref, kbuf, vbuf, sem, n_pages=n_pages, n_new=n_new, past=past)


def _proj_and_sample(x2d, norm_g, w_bf16, groups, out_dtypes, tm, page_table, lams, q, k_new, v_new, bias, subln_g,
                     qm, mem_kt, mem_vt, u, state16, wbd, pool_scale, cache_k, cache_v):
    rows, d = x2d.shape
    n_steps = rows // tm
    n_seq, n_pages = page_table.shape
    n_new = q.shape[1]
    page_rows = cache_k.shape[1]
    past = n_pages * page_rows // N_HEADS
    assert n_pages % 2 == 0 and n_seq >= PAGE_SLOTS and n_seq % n_steps == 0
    sps = n_seq // n_steps
    proj_in, proj_out_specs, proj_out_shape = _norm_proj_specs(rows, d, w_bf16.shape[1], groups, out_dtypes, tm)
    vec = pl.BlockSpec((1, HEAD_DIM), lambda i, pt: (0, 0))
    per_seq = lambda a: pl.BlockSpec((sps,) + a.shape[1:], lambda i, pt: (i,) + (0,) * (a.ndim - 1))
    const = lambda a: pl.BlockSpec(a.shape, lambda i, pt: (0,) * a.ndim)
    hbm = pl.BlockSpec(memory_space=pl.ANY)
    g = subln_g.reshape(1, V_DIM)
    ps = pool_scale.reshape(1, POOL_WIDTH)
    in_specs = proj_in + [vec, vec, vec, vec, per_seq(q), per_seq(k_new), per_seq(v_new), const(bias), const(g),
                          per_seq(qm), per_seq(mem_kt), per_seq(mem_vt), per_seq(u), per_seq(state16), const(wbd),
                          const(ps), hbm, hbm]
    out3 = lambda width: jax.ShapeDtypeStruct((n_seq, n_new, width), F32)
    out_spec = lambda width: pl.BlockSpec((sps, n_new, width), lambda i, pt: (i, 0, 0))
    page_buffers = pltpu.VMEM((PAGE_SLOTS, n_pages, page_rows, V_DIM), F32)
    grid_spec = pltpu.PrefetchScalarGridSpec(
        num_scalar_prefetch=1,
        grid=(n_steps,),
        in_specs=in_specs,
        out_specs=proj_out_specs + [out_spec(ATTN_WIDTH), out_spec(MEM_WIDTH), out_spec(POOL_WIDTH)],
        scratch_shapes=[pltpu.VMEM(bias.shape, F32), page_buffers, page_buffers,
                        pltpu.SemaphoreType.DMA((PAGE_SLOTS, 2))],
    )
    outs = pl.pallas_call(
        functools.partial(_proj_and_sample_kernel, groups=groups, n_proj_out=len(proj_out_shape), seqs_per_step=sps,
                          n_seq=n_seq, n_pages=n_pages, n_new=n_new, past=past),
        grid_spec=grid_spec,
        out_shape=proj_out_shape + [out3(ATTN_WIDTH), out3(MEM_WIDTH), out3(POOL_WIDTH)],
        compiler_params=_cparams("arbitrary"),
        name="proj_and_sample",
    )(page_table, x2d, norm_g.reshape(1, d), w_bf16, *lams, q, k_new, v_new, bias, g, qm, mem_kt, mem_vt, u, state16,
      wbd, ps, cache_k, cache_v)
    return outs[:len(proj_out_shape)], outs[len(proj_out_shape):]


def _merge_kernel(x_ref, o_ref, po_ref, om_ref, g_ref, wgl_ref, wa_ref, wp_ref, wm_ref, wout_ref, x1_ref):
    x = x_ref[...]
    h = _rmsnorm(x, g_ref[...]).astype(BF16)
    d = D_MODEL
    merged = jax.nn.sigmoid(_dot(h, wgl_ref[:, 0:d])) * _dot(o_ref[...].astype(BF16), wa_ref[...])
    merged = merged + jax.nn.sigmoid(_dot(h, wgl_ref[:, d:2 * d])) * _dot(po_ref[...].astype(BF16), wp_ref[...])
    merged = merged + jax.nn.sigmoid(_dot(h, wgl_ref[:, 2 * d:3 * d])) * _dot(om_ref[...].astype(BF16), wm_ref[...])
    x1_ref[...] = x + _dot(merged.astype(BF16), wout_ref[...])


def _merge(x2d, o, po, om, norm_g, wgl, wa, wp, wm, wout, tm):
    rows, d = x2d.shape
    row = lambda width: pl.BlockSpec((tm, width), lambda i: (i, 0))
    full = lambda a: pl.BlockSpec(a.shape, lambda i: (0, 0), pipeline_mode=pl.Buffered(1))
    g = norm_g.reshape(1, d)
    return pl.pallas_call(
        _merge_kernel,
        grid=(rows // tm,),
        in_specs=[row(d), row(ATTN_WIDTH), row(POOL_WIDTH), row(MEM_WIDTH), full(g), full(wgl), full(wa), full(wp),
                  full(wm), full(wout)],
        out_specs=row(d),
        out_shape=jax.ShapeDtypeStruct((rows, d), F32),
        compiler_params=_cparams("parallel"),
        name="merge",
    )(x2d, o, po, om, g, wgl, wa, wp, wm, wout)


def _conv_ffn_kernel(x_ref, g2_ref, wg_ref, wu_ref, cw_ref, cb_ref, wd_ref, gf_ref, *refs, tm, fc, seq_rows):
    if seq_rows:
        st0_ref, st1_ref, y_ref, t0_ref, t1_ref, act_ref, pre_ref, gate_ref = refs
        n_seqs = tm // seq_rows
        lanes = pre_ref.shape[2]
        pre_ref[...] = jnp.zeros_like(pre_ref)
        for j in range(D_FF // lanes):
            pre_ref[j, pl.ds(0, n_seqs, stride=seq_rows), :] = st0_ref[:, j * lanes:(j + 1) * lanes]
            pre_ref[j, pl.ds(1, n_seqs, stride=seq_rows), :] = st1_ref[:, j * lanes:(j + 1) * lanes]
    else:
        y_ref, tail_ref, carry_ref, act_ref = refs

        @pl.when(pl.program_id(1) == 0)
        def _():
            carry_ref[...] = jnp.zeros_like(carry_ref)

    x = x_ref[...]
    h = _rmsnorm(x, g2_ref[...]).astype(BF16)
    for c0 in range(0, D_FF, fc):
        cs = slice(c0, c0 + fc)
        gate = _dot(h, wg_ref[:, cs])
        up = _dot(h, wu_ref[:, cs])
        if seq_rows:
            slabs = range(c0 // lanes, (c0 + fc) // lanes)
            pre = jnp.concatenate([pre_ref[j] for j in slabs], axis=1)
            t = lax.broadcasted_iota(jnp.int32, (tm, fc), 0) % seq_rows
            back1 = jnp.where(t >= 1, pltpu.roll(gate, 1, 0), pltpu.roll(pre, tm - 1, 0))
            back2 = jnp.where(t >= 2, pltpu.roll(gate, 2, 0), pre)
            for j in slabs:
                gate_ref[j] = gate[:, j * lanes - c0:(j + 1) * lanes - c0]
        else:
            ext = jnp.concatenate([carry_ref[:, cs], gate], axis=0)
            back1 = pltpu.roll(ext, 1, 0)[8:]
            back2 = pltpu.roll(ext, 2, 0)[8:]
            carry_ref[:, cs] = gate[tm - 8:]
        gc = cb_ref[:, cs] + cw_ref[0:1, cs] * back2
        gc = gc + cw_ref[1:2, cs] * back1
        gc = gc + cw_ref[2:3, cs] * gate
        act_ref[:, cs] = (jax.nn.gelu(gc) * up).astype(BF16)
    if seq_rows:
        for j in range(D_FF // lanes):
            t0_ref[:, j * lanes:(j + 1) * lanes] = gate_ref[j, pl.ds(seq_rows - 2, n_seqs, stride=seq_rows), :]
            t1_ref[:, j * lanes:(j + 1) * lanes] = gate_ref[j, pl.ds(seq_rows - 1, n_seqs, stride=seq_rows), :]
    else:
        tail_ref[...] = carry_ref[...]
    y_ref[...] = _rmsnorm(x + _dot(act_ref[...], wd_ref[...]), gf_ref[...])


def _conv_ffn(x1, norm_g, wg, wu, conv_w, conv_b, wd, final_g, tm, batch=None, seq=None, prefix=None):
    rows, d = x1.shape
    g2 = norm_g.reshape(1, d)
    gf = final_g.reshape(1, d)
    cb = conv_b.reshape(1, D_FF)
    if prefix is None:
        nblk = seq // tm
        row = lambda width: pl.BlockSpec((tm, width), lambda b, i: (b * nblk + i, 0))
        full = lambda a: pl.BlockSpec(a.shape, lambda b, i: (0, 0), pipeline_mode=pl.Buffered(1))
        return pl.pallas_call(
            functools.partial(_conv_ffn_kernel, tm=tm, fc=FFN_CHUNK, seq_rows=0),
            grid=(batch, nblk),
            in_specs=[row(d), full(g2), full(wg), full(wu), full(conv_w), full(cb), full(wd), full(gf)],
            out_specs=[row(d), pl.BlockSpec((None, 8, D_FF), lambda b, i: (b, 0, 0))],
            out_shape=[jax.ShapeDtypeStruct((rows, d), F32), jax.ShapeDtypeStruct((batch, 8, D_FF), F32)],
            scratch_shapes=[pltpu.VMEM((8, D_FF), F32), pltpu.VMEM((tm, D_FF), BF16)],
            compiler_params=_cparams("parallel", "arbitrary"),
            name="conv_ffn_prompt",
        )(x1, g2, wg, wu, conv_w, cb, wd, gf)
    seq_rows = 8
    row = lambda width: pl.BlockSpec((tm, width), lambda i: (i, 0))
    per_seq = pl.BlockSpec((tm // seq_rows, D_FF), lambda i: (i, 0))
    full = lambda a: pl.BlockSpec(a.shape, lambda i: (0, 0), pipeline_mode=pl.Buffered(1))
    tails = jax.ShapeDtypeStruct((rows // seq_rows, D_FF), F32)
    return pl.pallas_call(
        functools.partial(_conv_ffn_kernel, tm=tm, fc=FFN_CHUNK, seq_rows=seq_rows),
        grid=(rows // tm,),
        in_specs=[row(d), full(g2), full(wg), full(wu), full(conv_w), full(cb), full(wd), full(gf), per_seq, per_seq],
        out_specs=[row(d), per_seq, per_seq],
        out_shape=[jax.ShapeDtypeStruct((rows, d), F32), tails, tails],
        scratch_shapes=[pltpu.VMEM((tm, D_FF), BF16), pltpu.VMEM((D_FF // LANES, tm, LANES), F32),
                        pltpu.VMEM((D_FF // LANES, tm, LANES), F32)],
        compiler_params=_cparams("parallel"),
        name="conv_ffn_sample",
    )(x1, g2, wg, wu, conv_w, cb, wd, gf, *prefix)


def kernel(x_prompt, x_sample, mem_prompt, cache_k, cache_v, page_table, state_pool, state_ffn_conv, cache_mem_k, cache_mem_v, norm1_g, w_in, lam_q1, lam_k1, lam_q2, lam_k2, subln_g, w_pool_grp, pool_scale, w_br_attn, w_br_pool, w_br_mem, mem_norm_g, w_mem_kv, w_out, norm2_g, w_ffn_gate, w_ffn_up, ffn_conv_w, ffn_conv_b, w_ffn_down, rel_bias, final_norm_g):
    depth = w_in.shape[0]
    assert depth == 1, "single-layer step only"
    B, S, D = x_prompt.shape
    DB, DS, _ = x_sample.shape
    n_phys, page = cache_k.shape[1], cache_k.shape[2]
    n_pages = page_table.shape[1]
    past = n_pages * page
    assert DS == 8 and S % ROW_BLOCK == 0 and (DB * DS) % ROW_BLOCK == 0
    assert ATTN_TK % ATTN_TQ == 0 and ROW_BLOCK % ATTN_TK == 0 and S % ATTN_TK == 0

    w_qkv = w_in[0, :, :2048].astype(BF16)
    w_gl = w_in[0, :, 2048:].astype(BF16)
    wa, wp, wm = w_br_attn[0].astype(BF16), w_br_pool[0].astype(BF16), w_br_mem[0].astype(BF16)
    wout = w_out[0].astype(BF16)
    wg, wu, wd = w_ffn_gate[0].astype(BF16), w_ffn_up[0].astype(BF16), w_ffn_down[0].astype(BF16)
    w_mem = w_mem_kv[0].astype(BF16)
    wbd = jnp.zeros((POOL_WIDTH, POOL_WIDTH), F32)
    for gi in range(len(POOL_WINDOWS)):
        sl = slice(gi * POOL_GROUP_DIM, (gi + 1) * POOL_GROUP_DIM)
        wbd = wbd.at[sl, sl].set(w_pool_grp[0, gi])
    wbd = wbd.astype(BF16)
    lams = tuple(a[0].reshape(1, HEAD_DIM) for a in (lam_q1, lam_k1, lam_q2, lam_k2))

    prompt_bias, sample_bias = _bias_tables(rel_bias, ATTN_TQ, ATTN_TK, past, DS)

    xs = x_sample.reshape(DB * DS, D)
    groups_s = ((0, 512, ((QK_SCALE, 0),)), (512, 512, ((1.0, HEAD_ROWS),)), (1024, 512, ((1.0, HEAD_ROWS),)),
                (1536, 256, ((1.0, 0),)), (1792, 256, ((MEM_SCALE, 0),)))
    qs, ks, vs, us, qms = _norm_proj(xs, norm1_g[0], w_qkv, groups_s, (F32,) * 5, ROW_BLOCK)
    seq3 = lambda a: a.reshape(DB, DS, a.shape[-1])
    head_rows = lambda a: a.reshape(DB, DS * N_HEADS, V_DIM)
    state16 = jnp.pad(state_pool[0], ((0, 0), (16 - POOL_STATE, 0), (0, 0)))
    mem_t = lambda a: jnp.transpose(a[0], (0, 2, 3, 1)).reshape(DB, MEM_WIDTH, N_MEM)
    pages = lambda a: a[0].reshape(n_phys, page * N_HEADS, V_DIM)

    xp = x_prompt.reshape(B * S, D)
    groups = ((0, 512, ((QK_SCALE * LOG2E, 0),)), (512, 512, ((1.0, HEAD_ROWS), (1.0, 0))),
              (1024, 512, ((1.0, HEAD_ROWS), (1.0, ATTN_TK))), (1536, 256, ((1.0, 0),)), (1792, 256, ((MEM_SCALE, 0),)))
    (qp, kp, kp_b, vp, vp_t, up, qmp), (os_, oms, pos_) = _proj_and_sample(
        xp, norm1_g[0], w_qkv, groups, (BF16, F32, BF16, F32, BF16, F32, BF16), ROW_BLOCK,
        page_table, lams, seq3(qs), head_rows(ks), head_rows(vs), sample_bias, subln_g[0], seq3(qms),
        mem_t(cache_mem_k), mem_t(cache_mem_v), seq3(us), state16, wbd, pool_scale[0], pages(cache_k), pages(cache_v))
    mk_p, mv_p = _norm_proj(mem_prompt.reshape(B * N_MEM, D), mem_norm_g[0], w_mem,
                            ((0, MEM_WIDTH, ((1.0, N_MEM),)), (MEM_WIDTH, MEM_WIDTH, ((1.0, N_MEM),))), (F32, F32), ROW_BLOCK)
    op = _prompt_attn(lams, qp, kp_b, vp_t, prompt_bias, subln_g[0], B, S, ATTN_TQ, ATTN_TK)
    pop, omp = _prompt_side(up, qmp, mk_p, mv_p, wbd, pool_scale[0], B, S, ROW_BLOCK)
    x1p = _merge(xp, op, pop, omp, norm1_g[0], w_gl, wa, wp, wm, wout, ROW_BLOCK)
    yp, tail_p = _conv_ffn(x1p, norm2_g[0], wg, wu, ffn_conv_w[0], ffn_conv_b[0], wd, final_norm_g, ROW_BLOCK,
                           batch=B, seq=S)

    flat = lambda a: a.reshape(DB * DS, a.shape[-1])
    x1s = _merge(xs, flat(os_), flat(pos_), flat(oms), norm1_g[0], w_gl, wa, wp, wm, wout, ROW_BLOCK)
    ys, tail0_s, tail1_s = _conv_ffn(x1s, norm2_g[0], wg, wu, ffn_conv_w[0], ffn_conv_b[0], wd, final_norm_g, ROW_BLOCK,
                                     prefix=(state_ffn_conv[0, :, 0], state_ffn_conv[0, :, 1]))

    heads = lambda a, n: a.reshape(1, n, -1, N_HEADS, V_DIM)
    new_pool_p = up.reshape(B, S, POOL_WIDTH)[:, S - POOL_STATE:][None]
    new_pool_s = jnp.concatenate([state_pool[0][:, DS:], seq3(us)], axis=1)[None]
    new_conv_p = tail_p[:, 6:8][None]
    new_conv_s = jnp.stack([tail0_s, tail1_s], axis=1)[None]
    mem_heads = lambda a: jnp.transpose(a.reshape(B, N_HEADS, MEM_HEAD_DIM, N_MEM), (0, 3, 1, 2))[None]
    return (yp.reshape(B, S, D), ys.reshape(DB, DS, D), heads(kp, B), heads(vp, B), heads(ks, DB), heads(vs, DB),
            new_pool_p, new_pool_s, new_conv_p, new_conv_s, mem_heads(mk_p), mem_heads(mv_p))
```

```python
import functools
import math

import jax
import jax.numpy as jnp
from jax import lax
from jax.experimental import pallas as pl
from jax.experimental.pallas import tpu as pltpu

F32 = jnp.float32
BF16 = jnp.bfloat16

D_MODEL = 1024
N_HEADS = 4
HEAD_DIM = 64
V_DIM = 2 * HEAD_DIM
ATTN_WIDTH = N_HEADS * V_DIM
POOL_WINDOWS = (2, 4, 8, 16)
POOL_GROUP_DIM = 64
POOL_WIDTH = 256
POOL_STATE = 15
MEM_HEAD_DIM = 64
MEM_WIDTH = 256
N_MEM = 256
D_FF = 2816
N_BUCKETS = 32
MAX_DISTANCE = 128
EPS = 1e-6
NEG_INF = -1e30
QK_SCALE = HEAD_DIM ** -0.5
MEM_SCALE = MEM_HEAD_DIM ** -0.5
LAM_INIT = 0.8 - 0.6 * math.exp(-0.3 * 0)
SUBLN_SCALE = 1.0 - LAM_INIT
LOG2E = math.log2(math.e)

ATTN_TQ = 256
ATTN_TK = 256
ROW_BLOCK = 512
FFN_CHUNK = 256
SAMPLE_KPAD = 128
PAGE_SLOTS = 3
VMEM_LIMIT = 56 * 1024 * 1024
HEAD_ROWS = -1
LANES = 128


def _cparams(*sem):
    return pltpu.CompilerParams(dimension_semantics=sem, vmem_limit_bytes=VMEM_LIMIT)


def _rmsnorm(x, g):
    return x * lax.rsqrt(jnp.mean(x * x, axis=-1, keepdims=True) + EPS) * g


def _dot(a, b):
    return jnp.dot(a, b, preferred_element_type=F32)


def _dot_nt(a, b):
    return lax.dot_general(a, b, (((1,), (1,)), ((), ())), preferred_element_type=F32)


def _softmax_rows(s):
    m = jnp.max(s, axis=-1, keepdims=True)
    p = jnp.exp(s - m)
    return p * (1.0 / jnp.sum(p, axis=-1, keepdims=True))


def _lam_value(lq1, lk1, lq2, lk2):
    a = jnp.sum(lq1 * lk1, axis=-1, keepdims=True)
    b = jnp.sum(lq2 * lk2, axis=-1, keepdims=True)
    return jnp.exp(a) - jnp.exp(b) + LAM_INIT


def _norm_proj_kernel(x_ref, g_ref, w_ref, *out_refs, groups):
    h = _rmsnorm(x_ref[...], g_ref[...]).astype(BF16)
    k = 0
    for start, size, outs in groups:
        y = _dot(h, w_ref[:, start:start + size])
        for scale, key_block in outs:
            o_ref = out_refs[k]
            k += 1
            ys = y if scale == 1.0 else y * scale
            if key_block == HEAD_ROWS:
                n = ys.shape[0]
                for hd in range(N_HEADS):
                    o_ref[pl.ds(hd, n, stride=N_HEADS), :] = ys[:, hd * V_DIM:(hd + 1) * V_DIM].astype(o_ref.dtype)
            elif key_block:
                yt = ys.T
                for b in range(o_ref.shape[0]):
                    o_ref[b] = yt[:, b * key_block:(b + 1) * key_block].astype(o_ref.dtype)
            else:
                o_ref[...] = ys.astype(o_ref.dtype)


def _norm_proj_specs(rows, d, n_cols, groups, out_dtypes, tm):
    out_shape, out_specs = [], []
    flat_outs = [(size, kb) for _, size, outs in groups for _, kb in outs]
    for (size, kb), dt in zip(flat_outs, out_dtypes):
        if kb == HEAD_ROWS:
            out_shape.append(jax.ShapeDtypeStruct((rows * N_HEADS, V_DIM), dt))
            out_specs.append(pl.BlockSpec((tm * N_HEADS, V_DIM), lambda i, *_: (i, 0)))
        elif kb:
            out_shape.append(jax.ShapeDtypeStruct((rows // kb, size, kb), dt))
            out_specs.append(pl.BlockSpec((tm // kb, size, kb), lambda i, *_: (i, 0, 0)))
        else:
            out_shape.append(jax.ShapeDtypeStruct((rows, size), dt))
            out_specs.append(pl.BlockSpec((tm, size), lambda i, *_: (i, 0)))
    in_specs = [pl.BlockSpec((tm, d), lambda i, *_: (i, 0)),
                pl.BlockSpec((1, d), lambda i, *_: (0, 0)),
                pl.BlockSpec((d, n_cols), lambda i, *_: (0, 0), pipeline_mode=pl.Buffered(1))]
    return in_specs, out_specs, out_shape


def _norm_proj(x2d, g, w_bf16, groups, out_dtypes, tm):
    rows, d = x2d.shape
    in_specs, out_specs, out_shape = _norm_proj_specs(rows, d, w_bf16.shape[1], groups, out_dtypes, tm)
    return pl.pallas_call(
        functools.partial(_norm_proj_kernel, groups=groups),
        grid=(rows // tm,),
        in_specs=in_specs,
        out_specs=out_specs,
        out_shape=out_shape,
        compiler_params=_cparams("parallel"),
        name="norm_proj",
    )(x2d, g.reshape(1, d), w_bf16)


def _rel_bucket(rel):
    n = jnp.maximum(rel, 0)
    max_exact = N_BUCKETS // 2
    nf = jnp.maximum(n, 1).astype(F32)
    large = max_exact + jnp.floor(jnp.log(nf / max_exact) / math.log(MAX_DISTANCE / max_exact)
                                  * (N_BUCKETS - max_exact)).astype(jnp.int32)
    large = jnp.minimum(large, N_BUCKETS - 1)
    return jnp.where(n < max_exact, n, large)


def _bias_from_rel(rel, rb_ref, head, visible):
    bucket = _rel_bucket(rel)
    acc = jnp.zeros(rel.shape, F32)
    for b in range(N_BUCKETS):
        acc = jnp.where(bucket == b, rb_ref[b, head], acc)
    return jnp.where(visible, acc, NEG_INF)


def _bias_kernel(rb_ref, pb_ref, sb_ref, *, tq, tk, past, n_new):
    n_kinds = pb_ref.shape[1]
    key = lax.broadcasted_iota(jnp.int32, (tk, tq), 0)
    qry = lax.broadcasted_iota(jnp.int32, (tk, tq), 1)
    for kind in range(n_kinds):
        rel = kind * tq + qry - key
        for h in range(N_HEADS):
            if kind * tq - (tk - 1) >= MAX_DISTANCE:
                tile = jnp.full((tk, tq), rb_ref[N_BUCKETS - 1, h], F32) * LOG2E
            else:
                tile = _bias_from_rel(rel, rb_ref, h, rel >= 0) * LOG2E
            pb_ref[h, kind, :, :tq] = tile
            pb_ref[h, kind, :, tq:] = tile
    rows_per_head = 2 * n_new
    n_cols = sb_ref.shape[1]
    qi = lax.broadcasted_iota(jnp.int32, (rows_per_head, n_cols), 0) % n_new
    key = lax.broadcasted_iota(jnp.int32, (rows_per_head, n_cols), 1)
    rel = past + qi - key
    for h in range(N_HEADS):
        sb_ref[h * rows_per_head:(h + 1) * rows_per_head, :] = _bias_from_rel(rel, rb_ref, h, rel >= 0)


def _bias_tables(rel_bias, tq, tk, past, n_new):
    n_kinds = -(-(tk + MAX_DISTANCE - 1) // tq) + 1
    n_cols = past + SAMPLE_KPAD
    return pl.pallas_call(
        functools.partial(_bias_kernel, tq=tq, tk=tk, past=past, n_new=n_new),
        in_specs=[pl.BlockSpec(memory_space=pltpu.SMEM)],
        out_specs=[pl.BlockSpec(memory_space=pltpu.VMEM), pl.BlockSpec(memory_space=pltpu.VMEM)],
        out_shape=[jax.ShapeDtypeStruct((N_HEADS, n_kinds, tk, 2 * tq), F32),
                   jax.ShapeDtypeStruct((N_HEADS * 2 * n_new, n_cols), F32)],
        compiler_params=pltpu.CompilerParams(vmem_limit_bytes=VMEM_LIMIT),
        name="bias_tables",
    )(rel_bias)


def _subln(o, g):
    return _rmsnorm(o, g) * SUBLN_SCALE


def _two_map_queries(q_h):
    lane = lax.broadcasted_iota(jnp.int32, q_h.shape, 1)
    zero = jnp.zeros_like(q_h)
    return jnp.concatenate([jnp.where(lane < HEAD_DIM, q_h, zero), jnp.where(lane >= HEAD_DIM, q_h, zero)], axis=0)


def _prompt_attn_kernel(lq1_ref, lk1_ref, lq2_ref, lk2_ref, q_ref, k_ref, vt_ref, bias_ref, g_ref, o_ref,
                        q2_ref, m_ref, l_ref, acc_ref, s0_ref, *, tq, tk):
    i = pl.program_id(1)
    lam = _lam_value(lq1_ref[...], lk1_ref[...], lq2_ref[...], lk2_ref[...])
    n_kinds = bias_ref.shape[1]
    for h in range(N_HEADS):
        q2_ref[h] = _two_map_queries(q_ref[:, h * V_DIM:(h + 1) * V_DIM])
    m_ref[...] = jnp.full(m_ref.shape, NEG_INF, F32)
    l_ref[...] = jnp.zeros(l_ref.shape, F32)
    acc_ref[...] = jnp.zeros(acc_ref.shape, F32)

    last = (i * tq + tq - 1) // tk

    def scores(j, h):
        r0 = pl.multiple_of(j * tk, tk)
        kind = jnp.minimum((i * tq - j * tk) // tq, n_kinds - 1)
        hs = slice(h * V_DIM, (h + 1) * V_DIM)
        return _dot_nt(k_ref[pl.ds(r0, tk), hs], q2_ref[h]) + bias_ref[h, kind]

    ahead = 2
    for h in range(ahead):
        s0_ref[h] = scores(0, h)

    ones_rows = jnp.ones((16, tk), BF16)

    def body(j, carry):
        pending = [s0_ref[h] for h in range(ahead)]
        j_next = jnp.minimum(j + 1, last)
        for h in range(N_HEADS):
            hs = slice(h * V_DIM, (h + 1) * V_DIM)
            s = pending.pop(0)
            pending.append(scores(j, h + ahead) if h + ahead < N_HEADS else scores(j_next, h + ahead - N_HEADS))
            m_old = m_ref[h]
            m_new = jnp.maximum(m_old, jnp.max(s, axis=0, keepdims=True))
            alpha = jnp.exp2(m_old - m_new)
            p = jnp.exp2(s - m_new).astype(BF16)
            pv = _dot(jnp.concatenate([vt_ref[j, hs, :], ones_rows], axis=0), p)
            l_ref[h] = alpha * l_ref[h] + pv[V_DIM:V_DIM + 1]
            m_ref[h] = m_new
            acc_ref[h] = alpha * acc_ref[h] + pv[:V_DIM]
        for h in range(ahead):
            s0_ref[h] = pending[h]
        return carry

    lax.fori_loop(0, last + 1, body, 0)
    for h in range(N_HEADS):
        out_t = acc_ref[h] * (1.0 / l_ref[h])
        o_t = out_t[:, :tq] - lam * out_t[:, tq:]
        o_ref[:, h * V_DIM:(h + 1) * V_DIM] = _subln(o_t.T, g_ref[...]).astype(o_ref.dtype)


def _prompt_attn(lams, q, k, vt, bias, subln_g, batch, seq, tq, tk):
    vec = pl.BlockSpec((1, HEAD_DIM), lambda b, i: (0, 0))
    nq = seq // tq
    return pl.pallas_call(
        functools.partial(_prompt_attn_kernel, tq=tq, tk=tk),
        grid=(batch, nq),
        in_specs=[vec, vec, vec, vec,
                  pl.BlockSpec((tq, ATTN_WIDTH), lambda b, i: (b * nq + i, 0)),
                  pl.BlockSpec((seq, ATTN_WIDTH), lambda b, i: (b, 0)),
                  pl.BlockSpec((seq // tk, ATTN_WIDTH, tk), lambda b, i: (b, 0, 0)),
                  pl.BlockSpec(bias.shape, lambda b, i: (0, 0, 0, 0), pipeline_mode=pl.Buffered(1)),
                  pl.BlockSpec((1, V_DIM), lambda b, i: (0, 0))],
        out_specs=pl.BlockSpec((tq, ATTN_WIDTH), lambda b, i: (b * nq + i, 0)),
        out_shape=jax.ShapeDtypeStruct((batch * seq, ATTN_WIDTH), BF16),
        scratch_shapes=[pltpu.VMEM((N_HEADS, 2 * tq, V_DIM), BF16),
                        pltpu.VMEM((N_HEADS, 1, 2 * tq), F32),
                        pltpu.VMEM((N_HEADS, 1, 2 * tq), F32),
                        pltpu.VMEM((N_HEADS, V_DIM, 2 * tq), F32),
                        pltpu.VMEM((2, tk, 2 * tq), F32)],
        compiler_params=_cparams("parallel", "parallel"),
        name="prompt_attn",
    )(*lams, q, k, vt, bias, subln_g.reshape(1, V_DIM))


def _pool_branch(ext, u, pos0, wbd, scale):
    n, c = u.shape
    p = ext.shape[0] - n
    s2 = ext + pltpu.roll(ext, 1, 0)
    s4 = s2 + pltpu.roll(s2, 2, 0)
    s8 = s4 + pltpu.roll(s4, 4, 0)
    s16 = s8 + pltpu.roll(s8, 8, 0)
    lane = lax.broadcasted_iota(jnp.int32, (n, c), 1)
    grp = lane // POOL_GROUP_DIM
    win = jnp.where(grp == 0, s2[p:], jnp.where(grp == 1, s4[p:], jnp.where(grp == 2, s8[p:], s16[p:])))
    width = jnp.where(grp == 0, 2, jnp.where(grp == 1, 4, jnp.where(grp == 2, 8, 16)))
    pos = pos0 + lax.broadcasted_iota(jnp.int32, (n, c), 0)
    cnt = jnp.minimum(pos + 1, width).astype(F32)
    d = win / cnt - u
    return _dot(d.astype(BF16), wbd) * scale


def _prompt_side_kernel(u_ref, uprev_ref, qm_ref, mkt_ref, mvt_ref, wbd_ref, ps_ref, po_ref, om_ref, *, tm):
    i = pl.program_id(1)
    u = u_ref[...]
    prev = jnp.where(i > 0, uprev_ref[...], 0.0)
    ext = jnp.concatenate([prev, u], axis=0)
    po_ref[...] = _pool_branch(ext, u, i * tm, wbd_ref[...], ps_ref[...]).astype(po_ref.dtype)
    heads = [slice(h * MEM_HEAD_DIM, (h + 1) * MEM_HEAD_DIM) for h in range(N_HEADS)]
    scores = [_dot(qm_ref[:, sl], mkt_ref[sl, :].astype(BF16)) for sl in heads]
    probs = [_softmax_rows(s).astype(BF16) for s in scores]
    for sl, p in zip(heads, probs):
        om_ref[:, sl] = _dot_nt(p, mvt_ref[sl, :].astype(BF16)).astype(om_ref.dtype)


def _prompt_side(u, qm, mk, mv, wbd, pool_scale, batch, seq, tm):
    nblk = seq // tm
    halo = 16
    row = lambda b, i: (b * nblk + i, 0)
    return pl.pallas_call(
        functools.partial(_prompt_side_kernel, tm=tm),
        grid=(batch, nblk),
        in_specs=[pl.BlockSpec((tm, POOL_WIDTH), row),
                  pl.BlockSpec((halo, POOL_WIDTH),
                               lambda b, i: (jnp.maximum((b * nblk + i) * (tm // halo) - 1, 0), 0)),
                  pl.BlockSpec((tm, MEM_WIDTH), row),
                  pl.BlockSpec((None, MEM_WIDTH, N_MEM), lambda b, i: (b, 0, 0)),
                  pl.BlockSpec((None, MEM_WIDTH, N_MEM), lambda b, i: (b, 0, 0)),
                  pl.BlockSpec((POOL_WIDTH, POOL_WIDTH), lambda b, i: (0, 0)),
                  pl.BlockSpec((1, POOL_WIDTH), lambda b, i: (0, 0))],
        out_specs=[pl.BlockSpec((tm, POOL_WIDTH), row), pl.BlockSpec((tm, MEM_WIDTH), row)],
        out_shape=[jax.ShapeDtypeStruct((batch * seq, POOL_WIDTH), BF16),
                   jax.ShapeDtypeStruct((batch * seq, MEM_WIDTH), BF16)],
        compiler_params=_cparams("parallel", "parallel"),
        name="prompt_side",
    )(u, u, qm, mk, mv, wbd, pool_scale.reshape(1, POOL_WIDTH))


def _page_copy(caches, sem, which, slot, p, page_id):
    hbm, buf = caches[which]
    return pltpu.make_async_copy(hbm.at[page_id], buf.at[slot, p], sem.at[slot, which])


def _start_page(caches, sem, pt_ref, seq, slot, p):
    page_id = pt_ref[seq, p]
    _page_copy(caches, sem, 0, slot, p, page_id).start()
    _page_copy(caches, sem, 1, slot, p, page_id).start()


def _wait_slot(caches, sem, which, slot, n_pages):
    for p in range(n_pages):
        _page_copy(caches, sem, which, slot, p, 0).wait()


def _sample_sequence(seq, n_seq, pt_ref, lq1_ref, lk1_ref, lq2_ref, lk2_ref, q_ref, kn_ref, vn_ref, bias_ref, g_ref,
                     ck_hbm, cv_hbm, o_ref, s_ref, kbuf, vbuf, sem, *, n_pages, n_new):
    slot = seq % PAGE_SLOTS
    lam = _lam_value(lq1_ref[...], lk1_ref[...], lq2_ref[...], lk2_ref[...])
    page = kbuf.shape[2] // N_HEADS
    n_tail = s_ref.shape[1] - n_pages * page
    pad = jnp.zeros((n_tail - n_new, V_DIM), F32)
    hr = 2 * n_new
    caches = ((ck_hbm, kbuf), (cv_hbm, vbuf))
    ahead = PAGE_SLOTS - 1

    @pl.when(seq == 0)
    def _():
        for s0 in range(ahead):
            for p in range(n_pages):
                _start_page(caches, sem, pt_ref, s0, s0, p)

    @pl.when(seq + ahead < n_seq)
    def _():
        for p in range(n_pages):
            _start_page(caches, sem, pt_ref, seq + ahead, (seq + ahead) % PAGE_SLOTS, p)

    def page_pair(buf, c, h):
        rows = pl.ds(h, page, stride=N_HEADS)
        return jnp.concatenate([buf[slot, c, rows, :], buf[slot, c + 1, rows, :]], axis=0).astype(BF16)

    def head_rows(ref, h):
        return ref[pl.ds(h, n_new, stride=N_HEADS), :]

    _wait_slot(caches, sem, 0, slot, n_pages)
    for h in range(N_HEADS):
        q2 = _two_map_queries(q_ref[:, h * V_DIM:(h + 1) * V_DIM]).astype(BF16)
        for c in range(0, n_pages, 2):
            s_ref[h * hr:(h + 1) * hr, c * page:(c + 2) * page] = _dot_nt(q2, page_pair(kbuf, c, h))
        k_tail = jnp.concatenate([head_rows(kn_ref, h), pad], axis=0).astype(BF16)
        s_ref[h * hr:(h + 1) * hr, n_pages * page:] = _dot_nt(q2, k_tail)
    prob = _softmax_rows(s_ref[...] + bias_ref[...])
    _wait_slot(caches, sem, 1, slot, n_pages)
    for h in range(N_HEADS):
        a = prob[h * hr:h * hr + n_new] - lam * prob[h * hr + n_new:(h + 1) * hr]
        v_tail = jnp.concatenate([head_rows(vn_ref, h), pad], axis=0).astype(BF16)
        o_h = _dot(a[:, n_pages * page:].astype(BF16), v_tail)
        for c in range(0, n_pages, 2):
            o_h = o_h + _dot(a[:, c * page:(c + 2) * page].astype(BF16), page_pair(vbuf, c, h))
        o_ref[:, h * V_DIM:(h + 1) * V_DIM] = _subln(o_h, g_ref[...]).astype(o_ref.dtype)


def _sample_side(qm_ref, mkt_ref, mvt_ref, u_ref, st_ref, wbd_ref, ps_ref, om_ref, po_ref, *, n_seqs, n_new, past):
    head_of_lane = lax.broadcasted_iota(jnp.int32, (n_new, MEM_WIDTH), 1) // MEM_HEAD_DIM
    scores = []
    for j in range(n_seqs):
        qm = qm_ref[j]
        qm4 = jnp.concatenate([jnp.where(head_of_lane == h, qm, 0.0) for h in range(N_HEADS)], axis=0).astype(BF16)
        scores.append(_dot(qm4, mkt_ref[j].astype(BF16)))
    probs = [_softmax_rows(s).astype(BF16) for s in scores]
    om_all = [_dot_nt(probs[j], mvt_ref[j].astype(BF16)) for j in range(n_seqs)]
    for j in range(n_seqs):
        om = jnp.zeros((n_new, MEM_WIDTH), F32)
        for h in range(N_HEADS):
            om = jnp.where(head_of_lane == h, om_all[j][h * n_new:(h + 1) * n_new], om)
        om_ref[j] = om.astype(om_ref.dtype)
    for j in range(n_seqs):
        u = u_ref[j]
        ext = jnp.concatenate([st_ref[j], u], axis=0)
        po_ref[j] = _pool_branch(ext, u, past, wbd_ref[...], ps_ref[...]).astype(po_ref.dtype)


def _proj_and_sample_kernel(pt_ref, x_ref, g1_ref, w_ref, lq1_ref, lk1_ref, lq2_ref, lk2_ref, q_ref, kn_ref, vn_ref,
                            bias_ref, g_ref, qm_ref, mkt_ref, mvt_ref, u_ref, st_ref, wbd_ref, ps_ref, ck_hbm, cv_hbm,
                            *refs, groups, n_proj_out, seqs_per_step, n_seq, n_pages, n_new, past):
    proj_out = refs[:n_proj_out]
    o_ref, om_ref, po_ref, s_ref, kbuf, vbuf, sem = refs[n_proj_out:]
    _norm_proj_kernel(x_ref, g1_ref, w_ref, *proj_out, groups=groups)
    first = pl.program_id(0) * seqs_per_step
    for j in range(seqs_per_step):
        _sample_sequence(first + j, n_seq, pt_ref, lq1_ref, lk1_ref, lq2_ref, lk2_ref, q_ref.at[j], kn_ref.at[j],
                         vn_ref.at[j], bias_ref, g_ref, ck_hbm, cv_hbm, o_ref.at[j], s_ref, kbuf, vbuf, sem,
                         n_pages=n_pages, n_new=n_new)
    _sample_side(qm_ref, mkt_ref, mvt_ref, u_ref, st_ref, wbd_ref, ps_ref, om_ref, po_ref,
                 n_seqs=seqs_per_step, n_new=n_new, past=past)


def _proj_and_sample(x2d, norm_g, w_bf16, groups, out_dtypes, tm, page_table, lams, q, k_new, v_new, bias, subln_g,
                     qm, mem_kt, mem_vt, u, state16, wbd, pool_scale, cache_k, cache_v):
    rows, d = x2d.shape
    n_steps = rows // tm
    n_seq, n_pages = page_table.shape
    n_new = q.shape[1]
    page_rows = cache_k.shape[1]
    past = n_pages * page_rows // N_HEADS
    assert n_pages % 2 == 0 and n_seq >= PAGE_SLOTS and n_seq % n_steps == 0
    sps = n_seq // n_steps
    proj_in, proj_out_specs, proj_out_shape = _norm_proj_specs(rows, d, w_bf16.shape[1], groups, out_dtypes, tm)
    vec = pl.BlockSpec((1, HEAD_DIM), lambda i, pt: (0, 0))
    per_seq = lambda a: pl.BlockSpec((sps,) + a.shape[1:], lambda i, pt: (i,) + (0,) * (a.ndim - 1))
    const = lambda a: pl.BlockSpec(a.shape, lambda i, pt: (0,) * a.ndim)
    hbm = pl.BlockSpec(memory_space=pl.ANY)
    g = subln_g.reshape(1, V_DIM)
    ps = pool_scale.reshape(1, POOL_WIDTH)
    in_specs = proj_in + [vec, vec, vec, vec, per_seq(q), per_seq(k_new), per_seq(v_new), const(bias), const(g),
                          per_seq(qm), per_seq(mem_kt), per_seq(mem_vt), per_seq(u), per_seq(state16), const(wbd),
                          const(ps), hbm, hbm]
    out3 = lambda width: jax.ShapeDtypeStruct((n_seq, n_new, width), F32)
    out_spec = lambda width: pl.BlockSpec((sps, n_new, width), lambda i, pt: (i, 0, 0))
    page_buffers = pltpu.VMEM((PAGE_SLOTS, n_pages, page_rows, V_DIM), F32)
    grid_spec = pltpu.PrefetchScalarGridSpec(
        num_scalar_prefetch=1,
        grid=(n_steps,),
        in_specs=in_specs,
        out_specs=proj_out_specs + [out_spec(ATTN_WIDTH), out_spec(MEM_WIDTH), out_spec(POOL_WIDTH)],
        scratch_shapes=[pltpu.VMEM(bias.shape, F32), page_buffers, page_buffers,
                        pltpu.SemaphoreType.DMA((PAGE_SLOTS, 2))],
    )
    outs = pl.pallas_call(
        functools.partial(_proj_and_sample_kernel, groups=groups, n_proj_out=len(proj_out_shape), seqs_per_step=sps,
                          n_seq=n_seq, n_pages=n_pages, n_new=n_new, past=past),
        grid_spec=grid_spec,
        out_shape=proj_out_shape + [out3(ATTN_WIDTH), out3(MEM_WIDTH), out3(POOL_WIDTH)],
        compiler_params=_cparams("arbitrary"),
        name="proj_and_sample",
    )(page_table, x2d, norm_g.reshape(1, d), w_bf16, *lams, q, k_new, v_new, bias, g, qm, mem_kt, mem_vt, u, state16,
      wbd, ps, cache_k, cache_v)
    return outs[:len(proj_out_shape)], outs[len(proj_out_shape):]


def _merge_kernel(x_ref, o_ref, po_ref, om_ref, g_ref, wgl_ref, wa_ref, wp_ref, wm_ref, wout_ref, x1_ref):
    x = x_ref[...]
    h = _rmsnorm(x, g_ref[...]).astype(BF16)
    d = D_MODEL
    merged = jax.nn.sigmoid(_dot(h, wgl_ref[:, 0:d])) * _dot(o_ref[...].astype(BF16), wa_ref[...])
    merged = merged + jax.nn.sigmoid(_dot(h, wgl_ref[:, d:2 * d])) * _dot(po_ref[...].astype(BF16), wp_ref[...])
    merged = merged + jax.nn.sigmoid(_dot(h, wgl_ref[:, 2 * d:3 * d])) * _dot(om_ref[...].astype(BF16), wm_ref[...])
    x1_ref[...] = x + _dot(merged.astype(BF16), wout_ref[...])


def _merge(x2d, o, po, om, norm_g, wgl, wa, wp, wm, wout, tm):
    rows, d = x2d.shape
    row = lambda width: pl.BlockSpec((tm, width), lambda i: (i, 0))
    full = lambda a: pl.BlockSpec(a.shape, lambda i: (0, 0), pipeline_mode=pl.Buffered(1))
    g = norm_g.reshape(1, d)
    return pl.pallas_call(
        _merge_kernel,
        grid=(rows // tm,),
        in_specs=[row(d), row(ATTN_WIDTH), row(POOL_WIDTH), row(MEM_WIDTH), full(g), full(wgl), full(wa), full(wp),
                  full(wm), full(wout)],
        out_specs=row(d),
        out_shape=jax.ShapeDtypeStruct((rows, d), F32),
        compiler_params=_cparams("parallel"),
        name="merge",
    )(x2d, o, po, om, g, wgl, wa, wp, wm, wout)


def _conv_ffn_kernel(x_ref, g2_ref, wg_ref, wu_ref, cw_ref, cb_ref, wd_ref, gf_ref, *refs, tm, fc, seq_rows):
    if seq_rows:
        st0_ref, st1_ref, y_ref, t0_ref, t1_ref, act_ref, pre_ref, gate_ref = refs
        n_seqs = tm // seq_rows
        lanes = pre_ref.shape[2]
        pre_ref[...] = jnp.zeros_like(pre_ref)
        for j in range(D_FF // lanes):
            pre_ref[j, pl.ds(0, n_seqs, stride=seq_rows), :] = st0_ref[:, j * lanes:(j + 1) * lanes]
            pre_ref[j, pl.ds(1, n_seqs, stride=seq_rows), :] = st1_ref[:, j * lanes:(j + 1) * lanes]
    else:
        y_ref, tail_ref, carry_ref, act_ref = refs

        @pl.when(pl.program_id(1) == 0)
        def _():
            carry_ref[...] = jnp.zeros_like(carry_ref)

    x = x_ref[...]
    h = _rmsnorm(x, g2_ref[...]).astype(BF16)
    for c0 in range(0, D_FF, fc):
        cs = slice(c0, c0 + fc)
        gate = _dot(h, wg_ref[:, cs])
        up = _dot(h, wu_ref[:, cs])
        if seq_rows:
            slabs = range(c0 // lanes, (c0 + fc) // lanes)
            pre = jnp.concatenate([pre_ref[j] for j in slabs], axis=1)
            t = lax.broadcasted_iota(jnp.int32, (tm, fc), 0) % seq_rows
            back1 = jnp.where(t >= 1, pltpu.roll(gate, 1, 0), pltpu.roll(pre, tm - 1, 0))
            back2 = jnp.where(t >= 2, pltpu.roll(gate, 2, 0), pre)
            for j in slabs:
                gate_ref[j] = gate[:, j * lanes - c0:(j + 1) * lanes - c0]
        else:
            ext = jnp.concatenate([carry_ref[:, cs], gate], axis=0)
            back1 = pltpu.roll(ext, 1, 0)[8:]
            back2 = pltpu.roll(ext, 2, 0)[8:]
            carry_ref[:, cs] = gate[tm - 8:]
        gc = cb_ref[:, cs] + cw_ref[0:1, cs] * back2
        gc = gc + cw_ref[1:2, cs] * back1
        gc = gc + cw_ref[2:3, cs] * gate
        act_ref[:, cs] = (jax.nn.gelu(gc) * up).astype(BF16)
    if seq_rows:
        for j in range(D_FF // lanes):
            t0_ref[:, j * lanes:(j + 1) * lanes] = gate_ref[j, pl.ds(seq_rows - 2, n_seqs, stride=seq_rows), :]
            t1_ref[:, j * lanes:(j + 1) * lanes] = gate_ref[j, pl.ds(seq_rows - 1, n_seqs, stride=seq_rows), :]
    else:
        tail_ref[...] = carry_ref[...]
    y_ref[...] = _rmsnorm(x + _dot(act_ref[...], wd_ref[...]), gf_ref[...])


def _conv_ffn(x1, norm_g, wg, wu, conv_w, conv_b, wd, final_g, tm, batch=None, seq=None, prefix=None):
    rows, d = x1.shape
    g2 = norm_g.reshape(1, d)
    gf = final_g.reshape(1, d)
    cb = conv_b.reshape(1, D_FF)
    if prefix is None:
        nblk = seq // tm
        row = lambda width: pl.BlockSpec((tm, width), lambda b, i: (b * nblk + i, 0))
        full = lambda a: pl.BlockSpec(a.shape, lambda b, i: (0, 0), pipeline_mode=pl.Buffered(1))
        return pl.pallas_call(
            functools.partial(_conv_ffn_kernel, tm=tm, fc=FFN_CHUNK, seq_rows=0),
            grid=(batch, nblk),
            in_specs=[row(d), full(g2), full(wg), full(wu), full(conv_w), full(cb), full(wd), full(gf)],
            out_specs=[row(d), pl.BlockSpec((None, 8, D_FF), lambda b, i: (b, 0, 0))],
            out_shape=[jax.ShapeDtypeStruct((rows, d), F32), jax.ShapeDtypeStruct((batch, 8, D_FF), F32)],
            scratch_shapes=[pltpu.VMEM((8, D_FF), F32), pltpu.VMEM((tm, D_FF), BF16)],
            compiler_params=_cparams("parallel", "arbitrary"),
            name="conv_ffn_prompt",
        )(x1, g2, wg, wu, conv_w, cb, wd, gf)
    seq_rows = 8
    row = lambda width: pl.BlockSpec((tm, width), lambda i: (i, 0))
    per_seq = pl.BlockSpec((tm // seq_rows, D_FF), lambda i: (i, 0))
    full = lambda a: pl.BlockSpec(a.shape, lambda i: (0, 0), pipeline_mode=pl.Buffered(1))
    tails = jax.ShapeDtypeStruct((rows // seq_rows, D_FF), F32)
    return pl.pallas_call(
        functools.partial(_conv_ffn_kernel, tm=tm, fc=FFN_CHUNK, seq_rows=seq_rows),
        grid=(rows // tm,),
        in_specs=[row(d), full(g2), full(wg), full(wu), full(conv_w), full(cb), full(wd), full(gf), per_seq, per_seq],
        out_specs=[row(d), per_seq, per_seq],
        out_shape=[jax.ShapeDtypeStruct((rows, d), F32), tails, tails],
        scratch_shapes=[pltpu.VMEM((tm, D_FF), BF16), pltpu.VMEM((D_FF // LANES, tm, LANES), F32),
                        pltpu.VMEM((D_FF // LANES, tm, LANES), F32)],
        compiler_params=_cparams("parallel"),
        name="conv_ffn_sample",
    )(x1, g2, wg, wu, conv_w, cb, wd, gf, *prefix)


def kernel(x_prompt, x_sample, mem_prompt, cache_k, cache_v, page_table, state_pool, state_ffn_conv, cache_mem_k, cache_mem_v, norm1_g, w_in, lam_q1, lam_k1, lam_q2, lam_k2, subln_g, w_pool_grp, pool_scale, w_br_attn, w_br_pool, w_br_mem, mem_norm_g, w_mem_kv, w_out, norm2_g, w_ffn_gate, w_ffn_up, ffn_conv_w, ffn_conv_b, w_ffn_down, rel_bias, final_norm_g):
    depth = w_in.shape[0]
    assert depth == 1, "single-layer step only"
    B, S, D = x_prompt.shape
    DB, DS, _ = x_sample.shape
    n_phys, page = cache_k.shape[1], cache_k.shape[2]
    n_pages = page_table.shape[1]
    past = n_pages * page
    assert DS == 8 and S % ROW_BLOCK == 0 and (DB * DS) % ROW_BLOCK == 0
    assert ATTN_TK % ATTN_TQ == 0 and ROW_BLOCK % ATTN_TK == 0 and S % ATTN_TK == 0

    w_qkv = w_in[0, :, :2048].astype(BF16)
    w_gl = w_in[0, :, 2048:].astype(BF16)
    wa, wp, wm = w_br_attn[0].astype(BF16), w_br_pool[0].astype(BF16), w_br_mem[0].astype(BF16)
    wout = w_out[0].astype(BF16)
    wg, wu, wd = w_ffn_gate[0].astype(BF16), w_ffn_up[0].astype(BF16), w_ffn_down[0].astype(BF16)
    w_mem = w_mem_kv[0].astype(BF16)
    wbd = jnp.zeros((POOL_WIDTH, POOL_WIDTH), F32)
    for gi in range(len(POOL_WINDOWS)):
        sl = slice(gi * POOL_GROUP_DIM, (gi + 1) * POOL_GROUP_DIM)
        wbd = wbd.at[sl, sl].set(w_pool_grp[0, gi])
    wbd = wbd.astype(BF16)
    lams = tuple(a[0].reshape(1, HEAD_DIM) for a in (lam_q1, lam_k1, lam_q2, lam_k2))

    prompt_bias, sample_bias = _bias_tables(rel_bias, ATTN_TQ, ATTN_TK, past, DS)

    xs = x_sample.reshape(DB * DS, D)
    groups_s = ((0, 512, ((QK_SCALE, 0),)), (512, 512, ((1.0, HEAD_ROWS),)), (1024, 512, ((1.0, HEAD_ROWS),)),
                (1536, 256, ((1.0, 0),)), (1792, 256, ((MEM_SCALE, 0),)))
    qs, ks, vs, us, qms = _norm_proj(xs, norm1_g[0], w_qkv, groups_s, (F32,) * 5, ROW_BLOCK)
    seq3 = lambda a: a.reshape(DB, DS, a.shape[-1])
    head_rows = lambda a: a.reshape(DB, DS * N_HEADS, V_DIM)
    state16 = jnp.pad(state_pool[0], ((0, 0), (16 - POOL_STATE, 0), (0, 0)))
    mem_t = lambda a: jnp.transpose(a[0], (0, 2, 3, 1)).reshape(DB, MEM_WIDTH, N_MEM)
    pages = lambda a: a[0].reshape(n_phys, page * N_HEADS, V_DIM)

    xp = x_prompt.reshape(B * S, D)
    groups = ((0, 512, ((QK_SCALE * LOG2E, 0),)), (512, 512, ((1.0, HEAD_ROWS), (1.0, 0))),
              (1024, 512, ((1.0, HEAD_ROWS), (1.0, ATTN_TK))), (1536, 256, ((1.0, 0),)), (1792, 256, ((MEM_SCALE, 0),)))
    (qp, kp, kp_b, vp, vp_t, up, qmp), (os_, oms, pos_) = _proj_and_sample(
        xp, norm1_g[0], w_qkv, groups, (BF16, F32, BF16, F32, BF16, F32, BF16), ROW_BLOCK,
        page_table, lams, seq3(qs), head_rows(ks), head_rows(vs), sample_bias, subln_g[0], seq3(qms),
        mem_t(cache_mem_k), mem_t(cache_mem_v), seq3(us), state16, wbd, pool_scale[0], pages(cache_k), pages(cache_v))
    mk_p, mv_p = _norm_proj(mem_prompt.reshape(B * N_MEM, D), mem_norm_g[0], w_mem,
                            ((0, MEM_WIDTH, ((1.0, N_MEM),)), (MEM_WIDTH, MEM_WIDTH, ((1.0, N_MEM),))), (F32, F32), ROW_BLOCK)
    op = _prompt_attn(lams, qp, kp_b, vp_t, prompt_bias, subln_g[0], B, S, ATTN_TQ, ATTN_TK)
    pop, omp = _prompt_side(up, qmp, mk_p, mv_p, wbd, pool_scale[0], B, S, ROW_BLOCK)
    x1p = _merge(xp, op, pop, omp, norm1_g[0], w_gl, wa, wp, wm, wout, ROW_BLOCK)
    yp, tail_p = _conv_ffn(x1p, norm2_g[0], wg, wu, ffn_conv_w[0], ffn_conv_b[0], wd, final_norm_g, ROW_BLOCK,
                           batch=B, seq=S)

    flat = lambda a: a.reshape(DB * DS, a.shape[-1])
    x1s = _merge(xs, flat(os_), flat(pos_), flat(oms), norm1_g[0], w_gl, wa, wp, wm, wout, ROW_BLOCK)
    ys, tail0_s, tail1_s = _conv_ffn(x1s, norm2_g[0], wg, wu, ffn_conv_w[0], ffn_conv_b[0], wd, final_norm_g, ROW_BLOCK,
                                     prefix=(state_ffn_conv[0, :, 0], state_ffn_conv[0, :, 1]))

    heads = lambda a, n: a.reshape(1, n, -1, N_HEADS, V_DIM)
    new_pool_p = up.reshape(B, S, POOL_WIDTH)[:, S - POOL_STATE:][None]
    new_pool_s = jnp.concatenate([state_pool[0][:, DS:], seq3(us)], axis=1)[None]
    new_conv_p = tail_p[:, 6:8][None]
    new_conv_s = jnp.stack([tail0_s, tail1_s], axis=1)[None]
    mem_heads = lambda a: jnp.transpose(a.reshape(B, N_HEADS, MEM_HEAD_DIM, N_MEM), (0, 3, 1, 2))[None]
    return (yp.reshape(B, S, D), ys.reshape(DB, DS, D), heads(kp, B), heads(vp, B), heads(ks, DB), heads(vs, DB),
            new_pool_p, new_pool_s, new_conv_p, new_conv_s, mem_heads(mk_p), mem_heads(mv_p))
```

```python
import functools
import math

import jax
import jax.numpy as jnp
from jax import lax
from jax.experimental import pallas as pl
from jax.experimental.pallas import tpu as pltpu

F32 = jnp.float32
BF16 = jnp.bfloat16

D_MODEL = 1024
N_HEADS = 4
HEAD_DIM = 64
V_DIM = 2 * HEAD_DIM
ATTN_WIDTH = N_HEADS * V_DIM
POOL_WINDOWS = (2, 4, 8, 16)
POOL_GROUP_DIM = 64
POOL_WIDTH = 256
POOL_STATE = 15
MEM_HEAD_DIM = 64
MEM_WIDTH = 256
N_MEM = 256
D_FF = 2816
N_BUCKETS = 32
MAX_DISTANCE = 128
EPS = 1e-6
NEG_INF = -1e30
QK_SCALE = HEAD_DIM ** -0.5
MEM_SCALE = MEM_HEAD_DIM ** -0.5
LAM_INIT = 0.8 - 0.6 * math.exp(-0.3 * 0)
SUBLN_SCALE = 1.0 - LAM_INIT
LOG2E = math.log2(math.e)

ATTN_TQ = 256
ATTN_TK = 256
ROW_BLOCK = 512
FFN_CHUNK = 256
SAMPLE_KPAD = 128
PAGE_SLOTS = 3
VMEM_LIMIT = 60 * 1024 * 1024
HEAD_ROWS = -1
LANES = 128


def _cparams(*sem):
    return pltpu.CompilerParams(dimension_semantics=sem, vmem_limit_bytes=VMEM_LIMIT)


def _rmsnorm(x, g):
    return x * lax.rsqrt(jnp.mean(x * x, axis=-1, keepdims=True) + EPS) * g


def _dot(a, b):
    return jnp.dot(a, b, preferred_element_type=F32)


def _dot_nt(a, b):
    return lax.dot_general(a, b, (((1,), (1,)), ((), ())), preferred_element_type=F32)


def _softmax_rows(s):
    m = jnp.max(s, axis=-1, keepdims=True)
    p = jnp.exp(s - m)
    return p * (1.0 / jnp.sum(p, axis=-1, keepdims=True))


def _lam_value(lq1, lk1, lq2, lk2):
    a = jnp.sum(lq1 * lk1, axis=-1, keepdims=True)
    b = jnp.sum(lq2 * lk2, axis=-1, keepdims=True)
    return jnp.exp(a) - jnp.exp(b) + LAM_INIT


def _norm_proj_kernel(x_ref, g_ref, w_ref, *out_refs, groups):
    h = _rmsnorm(x_ref[...], g_ref[...]).astype(BF16)
    k = 0
    for start, size, outs in groups:
        y = _dot(h, w_ref[:, start:start + size])
        for scale, key_block in outs:
            o_ref = out_refs[k]
            k += 1
            ys = y if scale == 1.0 else y * scale
            if key_block == HEAD_ROWS:
                n = ys.shape[0]
                for hd in range(N_HEADS):
                    o_ref[pl.ds(hd, n, stride=N_HEADS), :] = ys[:, hd * V_DIM:(hd + 1) * V_DIM].astype(o_ref.dtype)
            elif key_block:
                yt = ys.T
                for b in range(o_ref.shape[0]):
                    o_ref[b] = yt[:, b * key_block:(b + 1) * key_block].astype(o_ref.dtype)
            else:
                o_ref[...] = ys.astype(o_ref.dtype)


def _norm_proj_specs(rows, d, n_cols, groups, out_dtypes, tm):
    out_shape, out_specs = [], []
    flat_outs = [(size, kb) for _, size, outs in groups for _, kb in outs]
    for (size, kb), dt in zip(flat_outs, out_dtypes):
        if kb == HEAD_ROWS:
            out_shape.append(jax.ShapeDtypeStruct((rows * N_HEADS, V_DIM), dt))
            out_specs.append(pl.BlockSpec((tm * N_HEADS, V_DIM), lambda i, *_: (i, 0)))
        elif kb:
            out_shape.append(jax.ShapeDtypeStruct((rows // kb, size, kb), dt))
            out_specs.append(pl.BlockSpec((tm // kb, size, kb), lambda i, *_: (i, 0, 0)))
        else:
            out_shape.append(jax.ShapeDtypeStruct((rows, size), dt))
            out_specs.append(pl.BlockSpec((tm, size), lambda i, *_: (i, 0)))
    in_specs = [pl.BlockSpec((tm, d), lambda i, *_: (i, 0)),
                pl.BlockSpec((1, d), lambda i, *_: (0, 0)),
                pl.BlockSpec((d, n_cols), lambda i, *_: (0, 0), pipeline_mode=pl.Buffered(1))]
    return in_specs, out_specs, out_shape


def _norm_proj(x2d, g, w_bf16, groups, out_dtypes, tm):
    rows, d = x2d.shape
    in_specs, out_specs, out_shape = _norm_proj_specs(rows, d, w_bf16.shape[1], groups, out_dtypes, tm)
    return pl.pallas_call(
        functools.partial(_norm_proj_kernel, groups=groups),
        grid=(rows // tm,),
        in_specs=in_specs,
        out_specs=out_specs,
        out_shape=out_shape,
        compiler_params=_cparams("parallel"),
        name="norm_proj",
    )(x2d, g.reshape(1, d), w_bf16)


def _rel_bucket(rel):
    n = jnp.maximum(rel, 0)
    max_exact = N_BUCKETS // 2
    nf = jnp.maximum(n, 1).astype(F32)
    large = max_exact + jnp.floor(jnp.log(nf / max_exact) / math.log(MAX_DISTANCE / max_exact)
                                  * (N_BUCKETS - max_exact)).astype(jnp.int32)
    large = jnp.minimum(large, N_BUCKETS - 1)
    return jnp.where(n < max_exact, n, large)


def _bias_from_rel(rel, rb_ref, head, visible):
    bucket = _rel_bucket(rel)
    acc = jnp.zeros(rel.shape, F32)
    for b in range(N_BUCKETS):
        acc = jnp.where(bucket == b, rb_ref[b, head], acc)
    return jnp.where(visible, acc, NEG_INF)


def _bias_kernel(rb_ref, pb_ref, sb_ref, *, tq, tk, past, n_new):
    n_kinds = pb_ref.shape[1]
    key = lax.broadcasted_iota(jnp.int32, (tk, tq), 0)
    qry = lax.broadcasted_iota(jnp.int32, (tk, tq), 1)
    for kind in range(n_kinds):
        rel = kind * tq + qry - key
        for h in range(N_HEADS):
            if kind * tq - (tk - 1) >= MAX_DISTANCE:
                tile = jnp.full((tk, tq), rb_ref[N_BUCKETS - 1, h], F32) * LOG2E
            else:
                tile = _bias_from_rel(rel, rb_ref, h, rel >= 0) * LOG2E
            pb_ref[h, kind, :, :tq] = tile
            pb_ref[h, kind, :, tq:] = tile
    rows_per_head = 2 * n_new
    n_cols = sb_ref.shape[1]
    qi = lax.broadcasted_iota(jnp.int32, (rows_per_head, n_cols), 0) % n_new
    key = lax.broadcasted_iota(jnp.int32, (rows_per_head, n_cols), 1)
    rel = past + qi - key
    for h in range(N_HEADS):
        sb_ref[h * rows_per_head:(h + 1) * rows_per_head, :] = _bias_from_rel(rel, rb_ref, h, rel >= 0)


def _bias_tables(rel_bias, tq, tk, past, n_new):
    n_kinds = -(-(tk + MAX_DISTANCE - 1) // tq) + 1
    n_cols = past + SAMPLE_KPAD
    return pl.pallas_call(
        functools.partial(_bias_kernel, tq=tq, tk=tk, past=past, n_new=n_new),
        in_specs=[pl.BlockSpec(memory_space=pltpu.SMEM)],
        out_specs=[pl.BlockSpec(memory_space=pltpu.VMEM), pl.BlockSpec(memory_space=pltpu.VMEM)],
        out_shape=[jax.ShapeDtypeStruct((N_HEADS, n_kinds, tk, 2 * tq), F32),
                   jax.ShapeDtypeStruct((N_HEADS * 2 * n_new, n_cols), F32)],
        compiler_params=pltpu.CompilerParams(vmem_limit_bytes=VMEM_LIMIT),
        name="bias_tables",
    )(rel_bias)


def _subln(o, g):
    return _rmsnorm(o, g) * SUBLN_SCALE


def _two_map_queries(q_h):
    lane = lax.broadcasted_iota(jnp.int32, q_h.shape, 1)
    zero = jnp.zeros_like(q_h)
    return jnp.concatenate([jnp.where(lane < HEAD_DIM, q_h, zero), jnp.where(lane >= HEAD_DIM, q_h, zero)], axis=0)


def _prompt_attn_kernel(lq1_ref, lk1_ref, lq2_ref, lk2_ref, q_ref, k_ref, vt_ref, bias_ref, g_ref, o_ref,
                        q2_ref, m_ref, l_ref, acc_ref, s0_ref, *, tq, tk):
    i = pl.program_id(1)
    lam = _lam_value(lq1_ref[...], lk1_ref[...], lq2_ref[...], lk2_ref[...])
    n_kinds = bias_ref.shape[1]
    for h in range(N_HEADS):
        q2_ref[h] = _two_map_queries(q_ref[:, h * V_DIM:(h + 1) * V_DIM])
    m_ref[...] = jnp.full(m_ref.shape, NEG_INF, F32)
    l_ref[...] = jnp.zeros(l_ref.shape, F32)
    acc_ref[...] = jnp.zeros(acc_ref.shape, F32)

    last = (i * tq + tq - 1) // tk

    def scores(j, h):
        r0 = pl.multiple_of(j * tk, tk)
        kind = jnp.minimum((i * tq - j * tk) // tq, n_kinds - 1)
        hs = slice(h * V_DIM, (h + 1) * V_DIM)
        return _dot_nt(k_ref[pl.ds(r0, tk), hs], q2_ref[h]) + bias_ref[h, kind]

    ahead = 2
    for h in range(ahead):
        s0_ref[h] = scores(0, h)

    ones_rows = jnp.ones((16, tk), BF16)

    def body(j, carry):
        pending = [s0_ref[h] for h in range(ahead)]
        j_next = jnp.minimum(j + 1, last)
        for h in range(N_HEADS):
            hs = slice(h * V_DIM, (h + 1) * V_DIM)
            s = pending.pop(0)
            pending.append(scores(j, h + ahead) if h + ahead < N_HEADS else scores(j_next, h + ahead - N_HEADS))
            m_old = m_ref[h]
            m_new = jnp.maximum(m_old, jnp.max(s, axis=0, keepdims=True))
            alpha = jnp.exp2(m_old - m_new)
            p = jnp.exp2(s - m_new).astype(BF16)
            pv = _dot(jnp.concatenate([vt_ref[j, hs, :], ones_rows], axis=0), p)
            l_ref[h] = alpha * l_ref[h] + pv[V_DIM:V_DIM + 1]
            m_ref[h] = m_new
            acc_ref[h] = alpha * acc_ref[h] + pv[:V_DIM]
        for h in range(ahead):
            s0_ref[h] = pending[h]
        return carry

    lax.fori_loop(0, last + 1, body, 0)
    for h in range(N_HEADS):
        out_t = acc_ref[h] * (1.0 / l_ref[h])
        o_t = out_t[:, :tq] - lam * out_t[:, tq:]
        o_ref[:, h * V_DIM:(h + 1) * V_DIM] = _subln(o_t.T, g_ref[...]).astype(o_ref.dtype)


def _prompt_attn(lams, q, k, vt, bias, subln_g, batch, seq, tq, tk):
    vec = pl.BlockSpec((1, HEAD_DIM), lambda b, i: (0, 0))
    nq = seq // tq
    return pl.pallas_call(
        functools.partial(_prompt_attn_kernel, tq=tq, tk=tk),
        grid=(batch, nq),
        in_specs=[vec, vec, vec, vec,
                  pl.BlockSpec((tq, ATTN_WIDTH), lambda b, i: (b * nq + i, 0)),
                  pl.BlockSpec((seq, ATTN_WIDTH), lambda b, i: (b, 0)),
                  pl.BlockSpec((seq // tk, ATTN_WIDTH, tk), lambda b, i: (b, 0, 0)),
                  pl.BlockSpec(bias.shape, lambda b, i: (0, 0, 0, 0), pipeline_mode=pl.Buffered(1)),
                  pl.BlockSpec((1, V_DIM), lambda b, i: (0, 0))],
        out_specs=pl.BlockSpec((tq, ATTN_WIDTH), lambda b, i: (b * nq + i, 0)),
        out_shape=jax.ShapeDtypeStruct((batch * seq, ATTN_WIDTH), BF16),
        scratch_shapes=[pltpu.VMEM((N_HEADS, 2 * tq, V_DIM), BF16),
                        pltpu.VMEM((N_HEADS, 1, 2 * tq), F32),
                        pltpu.VMEM((N_HEADS, 1, 2 * tq), F32),
                        pltpu.VMEM((N_HEADS, V_DIM, 2 * tq), F32),
                        pltpu.VMEM((2, tk, 2 * tq), F32)],
        compiler_params=_cparams("parallel", "parallel"),
        name="prompt_attn",
    )(*lams, q, k, vt, bias, subln_g.reshape(1, V_DIM))


def _pool_branch(ext, u, pos0, wbd, scale):
    n, c = u.shape
    p = ext.shape[0] - n
    s2 = ext + pltpu.roll(ext, 1, 0)
    s4 = s2 + pltpu.roll(s2, 2, 0)
    s8 = s4 + pltpu.roll(s4, 4, 0)
    s16 = s8 + pltpu.roll(s8, 8, 0)
    lane = lax.broadcasted_iota(jnp.int32, (n, c), 1)
    grp = lane // POOL_GROUP_DIM
    win = jnp.where(grp == 0, s2[p:], jnp.where(grp == 1, s4[p:], jnp.where(grp == 2, s8[p:], s16[p:])))
    width = jnp.where(grp == 0, 2, jnp.where(grp == 1, 4, jnp.where(grp == 2, 8, 16)))
    pos = pos0 + lax.broadcasted_iota(jnp.int32, (n, c), 0)
    cnt = jnp.minimum(pos + 1, width).astype(F32)
    d = win / cnt - u
    return _dot(d.astype(BF16), wbd) * scale


def _prompt_side_kernel(u_ref, uprev_ref, qm_ref, mkt_ref, mvt_ref, wbd_ref, ps_ref, po_ref, om_ref, *, tm):
    i = pl.program_id(1)
    u = u_ref[...]
    prev = jnp.where(i > 0, uprev_ref[...], 0.0)
    ext = jnp.concatenate([prev, u], axis=0)
    po_ref[...] = _pool_branch(ext, u, i * tm, wbd_ref[...], ps_ref[...]).astype(po_ref.dtype)
    heads = [slice(h * MEM_HEAD_DIM, (h + 1) * MEM_HEAD_DIM) for h in range(N_HEADS)]
    scores = [_dot(qm_ref[:, sl], mkt_ref[sl, :].astype(BF16)) for sl in heads]
    probs = [_softmax_rows(s).astype(BF16) for s in scores]
    for sl, p in zip(heads, probs):
        om_ref[:, sl] = _dot_nt(p, mvt_ref[sl, :].astype(BF16)).astype(om_ref.dtype)


def _prompt_side(u, qm, mk, mv, wbd, pool_scale, batch, seq, tm):
    nblk = seq // tm
    halo = 16
    row = lambda b, i: (b * nblk + i, 0)
    return pl.pallas_call(
        functools.partial(_prompt_side_kernel, tm=tm),
        grid=(batch, nblk),
        in_specs=[pl.BlockSpec((tm, POOL_WIDTH), row),
                  pl.BlockSpec((halo, POOL_WIDTH),
                               lambda b, i: (jnp.maximum((b * nblk + i) * (tm // halo) - 1, 0), 0)),
                  pl.BlockSpec((tm, MEM_WIDTH), row),
                  pl.BlockSpec((None, MEM_WIDTH, N_MEM), lambda b, i: (b, 0, 0)),
                  pl.BlockSpec((None, MEM_WIDTH, N_MEM), lambda b, i: (b, 0, 0)),
                  pl.BlockSpec((POOL_WIDTH, POOL_WIDTH), lambda b, i: (0, 0)),
                  pl.BlockSpec((1, POOL_WIDTH), lambda b, i: (0, 0))],
        out_specs=[pl.BlockSpec((tm, POOL_WIDTH), row), pl.BlockSpec((tm, MEM_WIDTH), row)],
        out_shape=[jax.ShapeDtypeStruct((batch * seq, POOL_WIDTH), BF16),
                   jax.ShapeDtypeStruct((batch * seq, MEM_WIDTH), BF16)],
        compiler_params=_cparams("parallel", "parallel"),
        name="prompt_side",
    )(u, u, qm, mk, mv, wbd, pool_scale.reshape(1, POOL_WIDTH))


def _page_copy(caches, sem, which, slot, p, page_id):
    hbm, buf = caches[which]
    return pltpu.make_async_copy(hbm.at[page_id], buf.at[slot, p], sem.at[slot, which])


def _start_page(caches, sem, pt_ref, seq, slot, p):
    page_id = pt_ref[seq, p]
    _page_copy(caches, sem, 0, slot, p, page_id).start()
    _page_copy(caches, sem, 1, slot, p, page_id).start()


def _wait_slot(caches, sem, which, slot, n_pages):
    for p in range(n_pages):
        _page_copy(caches, sem, which, slot, p, 0).wait()


def _sample_sequence(seq, n_seq, pt_ref, lq1_ref, lk1_ref, lq2_ref, lk2_ref, q_ref, kn_ref, vn_ref, bias_ref, g_ref,
                     ck_hbm, cv_hbm, o_ref, s_ref, kbuf, vbuf, sem, *, n_pages, n_new):
    slot = seq % PAGE_SLOTS
    lam = _lam_value(lq1_ref[...], lk1_ref[...], lq2_ref[...], lk2_ref[...])
    page = kbuf.shape[2] // N_HEADS
    n_tail = s_ref.shape[1] - n_pages * page
    pad = jnp.zeros((n_tail - n_new, V_DIM), F32)
    hr = 2 * n_new
    caches = ((ck_hbm, kbuf), (cv_hbm, vbuf))
    ahead = PAGE_SLOTS - 1

    @pl.when(seq == 0)
    def _():
        for s0 in range(ahead):
            for p in range(n_pages):
                _start_page(caches, sem, pt_ref, s0, s0, p)

    @pl.when(seq + ahead < n_seq)
    def _():
        for p in range(n_pages):
            _start_page(caches, sem, pt_ref, seq + ahead, (seq + ahead) % PAGE_SLOTS, p)

    def page_pair(buf, c, h):
        rows = pl.ds(h, page, stride=N_HEADS)
        return jnp.concatenate([buf[slot, c, rows, :], buf[slot, c + 1, rows, :]], axis=0).astype(BF16)

    def head_rows(ref, h):
        return ref[pl.ds(h, n_new, stride=N_HEADS), :]

    _wait_slot(caches, sem, 0, slot, n_pages)
    for h in range(N_HEADS):
        q2 = _two_map_queries(q_ref[:, h * V_DIM:(h + 1) * V_DIM]).astype(BF16)
        for c in range(0, n_pages, 2):
            s_ref[h * hr:(h + 1) * hr, c * page:(c + 2) * page] = _dot_nt(q2, page_pair(kbuf, c, h))
        k_tail = jnp.concatenate([head_rows(kn_ref, h), pad], axis=0).astype(BF16)
        s_ref[h * hr:(h + 1) * hr, n_pages * page:] = _dot_nt(q2, k_tail)
    prob = _softmax_rows(s_ref[...] + bias_ref[...])
    _wait_slot(caches, sem, 1, slot, n_pages)
    for h in range(N_HEADS):
        a = prob[h * hr:h * hr + n_new] - lam * prob[h * hr + n_new:(h + 1) * hr]
        v_tail = jnp.concatenate([head_rows(vn_ref, h), pad], axis=0).astype(BF16)
        o_h = _dot(a[:, n_pages * page:].astype(BF16), v_tail)
        for c in range(0, n_pages, 2):
            o_h = o_h + _dot(a[:, c * page:(c + 2) * page].astype(BF16), page_pair(vbuf, c, h))
        o_ref[:, h * V_DIM:(h + 1) * V_DIM] = _subln(o_h, g_ref[...]).astype(o_ref.dtype)


def _sample_side(qm_ref, mkt_ref, mvt_ref, u_ref, st_ref, wbd_ref, ps_ref, om_ref, po_ref, *, n_seqs, n_new, past):
    head_of_lane = lax.broadcasted_iota(jnp.int32, (n_new, MEM_WIDTH), 1) // MEM_HEAD_DIM
    scores = []
    for j in range(n_seqs):
        qm = qm_ref[j]
        qm4 = jnp.concatenate([jnp.where(head_of_lane == h, qm, 0.0) for h in range(N_HEADS)], axis=0).astype(BF16)
        scores.append(_dot(qm4, mkt_ref[j].astype(BF16)))
    probs = [_softmax_rows(s).astype(BF16) for s in scores]
    om_all = [_dot_nt(probs[j], mvt_ref[j].astype(BF16)) for j in range(n_seqs)]
    for j in range(n_seqs):
        om = jnp.zeros((n_new, MEM_WIDTH), F32)
        for h in range(N_HEADS):
            om = jnp.where(head_of_lane == h, om_all[j][h * n_new:(h + 1) * n_new], om)
        om_ref[j] = om.astype(om_ref.dtype)
    for j in range(n_seqs):
        u = u_ref[j]
        ext = jnp.concatenate([st_ref[j], u], axis=0)
        po_ref[j] = _pool_branch(ext, u, past, wbd_ref[...], ps_ref[...]).astype(po_ref.dtype)


N_SAMPLE_INPUTS = 18


def _host_and_sample_kernel(pt_ref, *refs, host_body, n_host_in, n_host_out, seqs_per_step, n_seq, n_pages, n_new,
                            past):
    host_in = refs[:n_host_in]
    (lq1_ref, lk1_ref, lq2_ref, lk2_ref, q_ref, kn_ref, vn_ref, bias_ref, g_ref, qm_ref, mkt_ref, mvt_ref, u_ref,
     st_ref, wbd_ref, ps_ref, ck_hbm, cv_hbm) = refs[n_host_in:n_host_in + N_SAMPLE_INPUTS]
    outs = refs[n_host_in + N_SAMPLE_INPUTS:]
    host_out = outs[:n_host_out]
    o_ref, om_ref, po_ref, s_ref, kbuf, vbuf, sem = outs[n_host_out:]
    host_body(*host_in, *host_out)
    first = pl.program_id(0) * seqs_per_step
    for j in range(seqs_per_step):
        _sample_sequence(first + j, n_seq, pt_ref, lq1_ref, lk1_ref, lq2_ref, lk2_ref, q_ref.at[j], kn_ref.at[j],
                         vn_ref.at[j], bias_ref, g_ref, ck_hbm, cv_hbm, o_ref.at[j], s_ref, kbuf, vbuf, sem,
                         n_pages=n_pages, n_new=n_new)
    _sample_side(qm_ref, mkt_ref, mvt_ref, u_ref, st_ref, wbd_ref, ps_ref, om_ref, po_ref,
                 n_seqs=seqs_per_step, n_new=n_new, past=past)


def _host_and_sample(host_body, host_args, host_specs, n_steps, name, page_table, lams, q, k_new, v_new, bias, subln_g,
                     qm, mem_kt, mem_vt, u, state16, wbd, pool_scale, cache_k, cache_v):
    host_in, host_out_specs, host_out_shape = host_specs
    n_seq, n_pages = page_table.shape
    n_new = q.shape[1]
    page_rows = cache_k.shape[1]
    past = n_pages * page_rows // N_HEADS
    assert n_pages % 2 == 0 and n_seq >= PAGE_SLOTS and n_seq % n_steps == 0
    sps = n_seq // n_steps
    vec = pl.BlockSpec((1, HEAD_DIM), lambda i, pt: (0, 0))
    per_seq = lambda a: pl.BlockSpec((sps,) + a.shape[1:], lambda i, pt: (i,) + (0,) * (a.ndim - 1))
    const = lambda a: pl.BlockSpec(a.shape, lambda i, pt: (0,) * a.ndim)
    hbm = pl.BlockSpec(memory_space=pl.ANY)
    g = subln_g.reshape(1, V_DIM)
    ps = pool_scale.reshape(1, POOL_WIDTH)
    sample_args = (*lams, q, k_new, v_new, bias, g, qm, mem_kt, mem_vt, u, state16, wbd, ps, cache_k, cache_v)
    sample_specs = [vec, vec, vec, vec, per_seq(q), per_seq(k_new), per_seq(v_new), const(bias), const(g),
                    per_seq(qm), per_seq(mem_kt), per_seq(mem_vt), per_seq(u), per_seq(state16), const(wbd),
                    const(ps), hbm, hbm]
    assert len(sample_args) == len(sample_specs) == N_SAMPLE_INPUTS
    out3 = lambda width: jax.ShapeDtypeStruct((n_seq, n_new, width), F32)
    out_spec = lambda width: pl.BlockSpec((sps, n_new, width), lambda i, pt: (i, 0, 0))
    page_buffers = pltpu.VMEM((PAGE_SLOTS, n_pages, page_rows, V_DIM), F32)
    grid_spec = pltpu.PrefetchScalarGridSpec(
        num_scalar_prefetch=1,
        grid=(n_steps,),
        in_specs=list(host_in) + sample_specs,
        out_specs=list(host_out_specs) + [out_spec(ATTN_WIDTH), out_spec(MEM_WIDTH), out_spec(POOL_WIDTH)],
        scratch_shapes=[pltpu.VMEM(bias.shape, F32), page_buffers, page_buffers,
                        pltpu.SemaphoreType.DMA((PAGE_SLOTS, 2))],
    )
    outs = pl.pallas_call(
        functools.partial(_host_and_sample_kernel, host_body=host_body, n_host_in=len(host_in),
                          n_host_out=len(host_out_shape), seqs_per_step=sps, n_seq=n_seq, n_pages=n_pages,
                          n_new=n_new, past=past),
        grid_spec=grid_spec,
        out_shape=list(host_out_shape) + [out3(ATTN_WIDTH), out3(MEM_WIDTH), out3(POOL_WIDTH)],
        compiler_params=_cparams("arbitrary"),
        name=name,
    )(page_table, *host_args, *sample_args)
    return outs[:len(host_out_shape)], outs[len(host_out_shape):]


def _merge_kernel(x_ref, o_ref, po_ref, om_ref, g_ref, wgl_ref, wa_ref, wp_ref, wm_ref, wout_ref, x1_ref):
    x = x_ref[...]
    h = _rmsnorm(x, g_ref[...]).astype(BF16)
    d = D_MODEL
    merged = jax.nn.sigmoid(_dot(h, wgl_ref[:, 0:d])) * _dot(o_ref[...].astype(BF16), wa_ref[...])
    merged = merged + jax.nn.sigmoid(_dot(h, wgl_ref[:, d:2 * d])) * _dot(po_ref[...].astype(BF16), wp_ref[...])
    merged = merged + jax.nn.sigmoid(_dot(h, wgl_ref[:, 2 * d:3 * d])) * _dot(om_ref[...].astype(BF16), wm_ref[...])
    x1_ref[...] = x + _dot(merged.astype(BF16), wout_ref[...])


def _merge_specs(rows, d, g, weights, tm):
    row = lambda width: pl.BlockSpec((tm, width), lambda i, *_: (i, 0))
    full = lambda a: pl.BlockSpec(a.shape, lambda i, *_: (0, 0), pipeline_mode=pl.Buffered(1))
    in_specs = [row(d), row(ATTN_WIDTH), row(POOL_WIDTH), row(MEM_WIDTH), full(g)] + [full(w) for w in weights]
    return in_specs, [row(d)], [jax.ShapeDtypeStruct((rows, d), F32)]


def _merge(x2d, o, po, om, norm_g, wgl, wa, wp, wm, wout, tm):
    rows, d = x2d.shape
    g = norm_g.reshape(1, d)
    in_specs, out_specs, out_shape = _merge_specs(rows, d, g, (wgl, wa, wp, wm, wout), tm)
    return pl.pallas_call(
        _merge_kernel,
        grid=(rows // tm,),
        in_specs=in_specs,
        out_specs=out_specs[0],
        out_shape=out_shape[0],
        compiler_params=_cparams("parallel"),
        name="merge",
    )(x2d, o, po, om, g, wgl, wa, wp, wm, wout)


def _conv_ffn_kernel(x_ref, g2_ref, wg_ref, wu_ref, cw_ref, cb_ref, wd_ref, gf_ref, *refs, tm, fc, seq_rows):
    if seq_rows:
        st0_ref, st1_ref, y_ref, t0_ref, t1_ref, act_ref, pre_ref, gate_ref = refs
        n_seqs = tm // seq_rows
        lanes = pre_ref.shape[2]
        pre_ref[...] = jnp.zeros_like(pre_ref)
        for j in range(D_FF // lanes):
            pre_ref[j, pl.ds(0, n_seqs, stride=seq_rows), :] = st0_ref[:, j * lanes:(j + 1) * lanes]
            pre_ref[j, pl.ds(1, n_seqs, stride=seq_rows), :] = st1_ref[:, j * lanes:(j + 1) * lanes]
    else:
        y_ref, tail_ref, carry_ref, act_ref = refs

        @pl.when(pl.program_id(1) == 0)
        def _():
            carry_ref[...] = jnp.zeros_like(carry_ref)

    x = x_ref[...]
    h = _rmsnorm(x, g2_ref[...]).astype(BF16)
    for c0 in range(0, D_FF, fc):
        cs = slice(c0, c0 + fc)
        gate = _dot(h, wg_ref[:, cs])
        up = _dot(h, wu_ref[:, cs])
        if seq_rows:
            slabs = range(c0 // lanes, (c0 + fc) // lanes)
            pre = jnp.concatenate([pre_ref[j] for j in slabs], axis=1)
            t = lax.broadcasted_iota(jnp.int32, (tm, fc), 0) % seq_rows
            back1 = jnp.where(t >= 1, pltpu.roll(gate, 1, 0), pltpu.roll(pre, tm - 1, 0))
            back2 = jnp.where(t >= 2, pltpu.roll(gate, 2, 0), pre)
            for j in slabs:
                gate_ref[j] = gate[:, j * lanes - c0:(j + 1) * lanes - c0]
        else:
            ext = jnp.concatenate([carry_ref[:, cs], gate], axis=0)
            back1 = pltpu.roll(ext, 1, 0)[8:]
            back2 = pltpu.roll(ext, 2, 0)[8:]
            carry_ref[:, cs] = gate[tm - 8:]
        gc = cb_ref[:, cs] + cw_ref[0:1, cs] * back2
        gc = gc + cw_ref[1:2, cs] * back1
        gc = gc + cw_ref[2:3, cs] * gate
        act_ref[:, cs] = (jax.nn.gelu(gc) * up).astype(BF16)
    if seq_rows:
        for j in range(D_FF // lanes):
            t0_ref[:, j * lanes:(j + 1) * lanes] = gate_ref[j, pl.ds(seq_rows - 2, n_seqs, stride=seq_rows), :]
            t1_ref[:, j * lanes:(j + 1) * lanes] = gate_ref[j, pl.ds(seq_rows - 1, n_seqs, stride=seq_rows), :]
    else:
        tail_ref[...] = carry_ref[...]
    y_ref[...] = _rmsnorm(x + _dot(act_ref[...], wd_ref[...]), gf_ref[...])


def _conv_ffn(x1, norm_g, wg, wu, conv_w, conv_b, wd, final_g, tm, batch=None, seq=None, prefix=None):
    rows, d = x1.shape
    g2 = norm_g.reshape(1, d)
    gf = final_g.reshape(1, d)
    cb = conv_b.reshape(1, D_FF)
    if prefix is None:
        nblk = seq // tm
        row = lambda width: pl.BlockSpec((tm, width), lambda b, i: (b * nblk + i, 0))
        full = lambda a: pl.BlockSpec(a.shape, lambda b, i: (0, 0), pipeline_mode=pl.Buffered(1))
        return pl.pallas_call(
            functools.partial(_conv_ffn_kernel, tm=tm, fc=FFN_CHUNK, seq_rows=0),
            grid=(batch, nblk),
            in_specs=[row(d), full(g2), full(wg), full(wu), full(conv_w), full(cb), full(wd), full(gf)],
            out_specs=[row(d), pl.BlockSpec((None, 8, D_FF), lambda b, i: (b, 0, 0))],
            out_shape=[jax.ShapeDtypeStruct((rows, d), F32), jax.ShapeDtypeStruct((batch, 8, D_FF), F32)],
            scratch_shapes=[pltpu.VMEM((8, D_FF), F32), pltpu.VMEM((tm, D_FF), BF16)],
            compiler_params=_cparams("parallel", "arbitrary"),
            name="conv_ffn_prompt",
        )(x1, g2, wg, wu, conv_w, cb, wd, gf)
    seq_rows = 8
    row = lambda width: pl.BlockSpec((tm, width), lambda i: (i, 0))
    per_seq = pl.BlockSpec((tm // seq_rows, D_FF), lambda i: (i, 0))
    full = lambda a: pl.BlockSpec(a.shape, lambda i: (0, 0), pipeline_mode=pl.Buffered(1))
    tails = jax.ShapeDtypeStruct((rows // seq_rows, D_FF), F32)
    return pl.pallas_call(
        functools.partial(_conv_ffn_kernel, tm=tm, fc=FFN_CHUNK, seq_rows=seq_rows),
        grid=(rows // tm,),
        in_specs=[row(d), full(g2), full(wg), full(wu), full(conv_w), full(cb), full(wd), full(gf), per_seq, per_seq],
        out_specs=[row(d), per_seq, per_seq],
        out_shape=[jax.ShapeDtypeStruct((rows, d), F32), tails, tails],
        scratch_shapes=[pltpu.VMEM((tm, D_FF), BF16), pltpu.VMEM((D_FF // LANES, tm, LANES), F32),
                        pltpu.VMEM((D_FF // LANES, tm, LANES), F32)],
        compiler_params=_cparams("parallel"),
        name="conv_ffn_sample",
    )(x1, g2, wg, wu, conv_w, cb, wd, gf, *prefix)


def kernel(x_prompt, x_sample, mem_prompt, cache_k, cache_v, page_table, state_pool, state_ffn_conv, cache_mem_k, cache_mem_v, norm1_g, w_in, lam_q1, lam_k1, lam_q2, lam_k2, subln_g, w_pool_grp, pool_scale, w_br_attn, w_br_pool, w_br_mem, mem_norm_g, w_mem_kv, w_out, norm2_g, w_ffn_gate, w_ffn_up, ffn_conv_w, ffn_conv_b, w_ffn_down, rel_bias, final_norm_g):
    depth = w_in.shape[0]
    assert depth == 1, "single-layer step only"
    B, S, D = x_prompt.shape
    DB, DS, _ = x_sample.shape
    n_phys, page = cache_k.shape[1], cache_k.shape[2]
    n_pages = page_table.shape[1]
    past = n_pages * page
    assert DS == 8 and S % ROW_BLOCK == 0 and (DB * DS) % ROW_BLOCK == 0
    assert ATTN_TK % ATTN_TQ == 0 and ROW_BLOCK % ATTN_TK == 0 and S % ATTN_TK == 0

    w_qkv = w_in[0, :, :2048].astype(BF16)
    w_gl = w_in[0, :, 2048:].astype(BF16)
    wa, wp, wm = w_br_attn[0].astype(BF16), w_br_pool[0].astype(BF16), w_br_mem[0].astype(BF16)
    wout = w_out[0].astype(BF16)
    wg, wu, wd = w_ffn_gate[0].astype(BF16), w_ffn_up[0].astype(BF16), w_ffn_down[0].astype(BF16)
    w_mem = w_mem_kv[0].astype(BF16)
    wbd = jnp.zeros((POOL_WIDTH, POOL_WIDTH), F32)
    for gi in range(len(POOL_WINDOWS)):
        sl = slice(gi * POOL_GROUP_DIM, (gi + 1) * POOL_GROUP_DIM)
        wbd = wbd.at[sl, sl].set(w_pool_grp[0, gi])
    wbd = wbd.astype(BF16)
    lams = tuple(a[0].reshape(1, HEAD_DIM) for a in (lam_q1, lam_k1, lam_q2, lam_k2))

    prompt_bias, sample_bias = _bias_tables(rel_bias, ATTN_TQ, ATTN_TK, past, DS)

    xs = x_sample.reshape(DB * DS, D)
    groups_s = ((0, 512, ((QK_SCALE, 0),)), (512, 512, ((1.0, HEAD_ROWS),)), (1024, 512, ((1.0, HEAD_ROWS),)),
                (1536, 256, ((1.0, 0),)), (1792, 256, ((MEM_SCALE, 0),)))
    qs, ks, vs, us, qms = _norm_proj(xs, norm1_g[0], w_qkv, groups_s, (F32,) * 5, ROW_BLOCK)
    seq3 = lambda a: a.reshape(DB, DS, a.shape[-1])
    head_rows = lambda a: a.reshape(DB, DS * N_HEADS, V_DIM)
    state16 = jnp.pad(state_pool[0], ((0, 0), (16 - POOL_STATE, 0), (0, 0)))
    mem_t = lambda a: jnp.transpose(a[0], (0, 2, 3, 1)).reshape(DB, MEM_WIDTH, N_MEM)
    pages = lambda a: a[0].reshape(n_phys, page * N_HEADS, V_DIM)

    xp = x_prompt.reshape(B * S, D)
    groups = ((0, 512, ((QK_SCALE * LOG2E, 0),)), (512, 512, ((1.0, HEAD_ROWS), (1.0, 0))),
              (1024, 512, ((1.0, HEAD_ROWS), (1.0, ATTN_TK))), (1536, 256, ((1.0, 0),)), (1792, 256, ((MEM_SCALE, 0),)))
    qp, kp, kp_b, vp, vp_t, up, qmp = _norm_proj(xp, norm1_g[0], w_qkv, groups,
                                                 (BF16, F32, BF16, F32, BF16, F32, BF16), ROW_BLOCK)
    mk_p, mv_p = _norm_proj(mem_prompt.reshape(B * N_MEM, D), mem_norm_g[0], w_mem,
                            ((0, MEM_WIDTH, ((1.0, N_MEM),)), (MEM_WIDTH, MEM_WIDTH, ((1.0, N_MEM),))), (F32, F32), ROW_BLOCK)
    op = _prompt_attn(lams, qp, kp_b, vp_t, prompt_bias, subln_g[0], B, S, ATTN_TQ, ATTN_TK)
    pop, omp = _prompt_side(up, qmp, mk_p, mv_p, wbd, pool_scale[0], B, S, ROW_BLOCK)
    g1 = norm1_g[0].reshape(1, D)
    (x1p,), (os_, oms, pos_) = _host_and_sample(
        _merge_kernel, (xp, op, pop, omp, g1, w_gl, wa, wp, wm, wout),
        _merge_specs(B * S, D, g1, (w_gl, wa, wp, wm, wout), ROW_BLOCK), B * S // ROW_BLOCK, "merge_and_sample",
        page_table, lams, seq3(qs), head_rows(ks), head_rows(vs), sample_bias, subln_g[0], seq3(qms),
        mem_t(cache_mem_k), mem_t(cache_mem_v), seq3(us), state16, wbd, pool_scale[0], pages(cache_k), pages(cache_v))
    yp, tail_p = _conv_ffn(x1p, norm2_g[0], wg, wu, ffn_conv_w[0], ffn_conv_b[0], wd, final_norm_g, ROW_BLOCK,
                           batch=B, seq=S)

    flat = lambda a: a.reshape(DB * DS, a.shape[-1])
    x1s = _merge(xs, flat(os_), flat(pos_), flat(oms), norm1_g[0], w_gl, wa, wp, wm, wout, ROW_BLOCK)
    ys, tail0_s, tail1_s = _conv_ffn(x1s, norm2_g[0], wg, wu, ffn_conv_w[0], ffn_conv_b[0], wd, final_norm_g, ROW_BLOCK,
                                     prefix=(state_ffn_conv[0, :, 0], state_ffn_conv[0, :, 1]))

    heads = lambda a, n: a.reshape(1, n, -1, N_HEADS, V_DIM)
    new_pool_p = up.reshape(B, S, POOL_WIDTH)[:, S - POOL_STATE:][None]
    new_pool_s = jnp.concatenate([state_pool[0][:, DS:], seq3(us)], axis=1)[None]
    new_conv_p = tail_p[:, 6:8][None]
    new_conv_s = jnp.stack([tail0_s, tail1_s], axis=1)[None]
    mem_heads = lambda a: jnp.transpose(a.reshape(B, N_HEADS, MEM_HEAD_DIM, N_MEM), (0, 3, 1, 2))[None]
    return (yp.reshape(B, S, D), ys.reshape(DB, DS, D), heads(kp, B), heads(vp, B), heads(ks, DB), heads(vs, DB),
            new_pool_p, new_pool_s, new_conv_p, new_conv_s, mem_heads(mk_p), mem_heads(mv_p))
```

```python
import functools
import math

import jax
import jax.numpy as jnp
from jax import lax
from jax.experimental import pallas as pl
from jax.experimental.pallas import tpu as pltpu

F32 = jnp.float32
BF16 = jnp.bfloat16

D_MODEL = 1024
N_HEADS = 4
HEAD_DIM = 64
V_DIM = 2 * HEAD_DIM
ATTN_WIDTH = N_HEADS * V_DIM
POOL_WINDOWS = (2, 4, 8, 16)
POOL_GROUP_DIM = 64
POOL_WIDTH = 256
POOL_STATE = 15
MEM_HEAD_DIM = 64
MEM_WIDTH = 256
N_MEM = 256
D_FF = 2816
N_BUCKETS = 32
MAX_DISTANCE = 128
EPS = 1e-6
NEG_INF = -1e30
QK_SCALE = HEAD_DIM ** -0.5
MEM_SCALE = MEM_HEAD_DIM ** -0.5
LAM_INIT = 0.8 - 0.6 * math.exp(-0.3 * 0)
SUBLN_SCALE = 1.0 - LAM_INIT
LOG2E = math.log2(math.e)

ATTN_TQ = 256
ATTN_TK = 256
ROW_BLOCK = 512
FFN_CHUNK = 256
SAMPLE_KPAD = 128
PAGE_SLOTS = 3
VMEM_LIMIT = 60 * 1024 * 1024
HEAD_ROWS = -1
LANES = 128


def _cparams(*sem):
    return pltpu.CompilerParams(dimension_semantics=sem, vmem_limit_bytes=VMEM_LIMIT)


def _rmsnorm(x, g):
    return x * lax.rsqrt(jnp.mean(x * x, axis=-1, keepdims=True) + EPS) * g


def _dot(a, b):
    return jnp.dot(a, b, preferred_element_type=F32)


def _dot_nt(a, b):
    return lax.dot_general(a, b, (((1,), (1,)), ((), ())), preferred_element_type=F32)


def _softmax_rows(s):
    m = jnp.max(s, axis=-1, keepdims=True)
    p = jnp.exp(s - m)
    return p * (1.0 / jnp.sum(p, axis=-1, keepdims=True))


def _lam_value(lq1, lk1, lq2, lk2):
    a = jnp.sum(lq1 * lk1, axis=-1, keepdims=True)
    b = jnp.sum(lq2 * lk2, axis=-1, keepdims=True)
    return jnp.exp(a) - jnp.exp(b) + LAM_INIT


def _norm_proj_kernel(x_ref, g_ref, w_ref, *out_refs, groups):
    h = _rmsnorm(x_ref[...], g_ref[...]).astype(BF16)
    k = 0
    for start, size, outs in groups:
        y = _dot(h, w_ref[:, start:start + size])
        for scale, key_block in outs:
            o_ref = out_refs[k]
            k += 1
            ys = y if scale == 1.0 else y * scale
            if key_block == HEAD_ROWS:
                n = ys.shape[0]
                for hd in range(N_HEADS):
                    o_ref[pl.ds(hd, n, stride=N_HEADS), :] = ys[:, hd * V_DIM:(hd + 1) * V_DIM].astype(o_ref.dtype)
            elif key_block:
                yt = ys.T
                for b in range(o_ref.shape[0]):
                    o_ref[b] = yt[:, b * key_block:(b + 1) * key_block].astype(o_ref.dtype)
            else:
                o_ref[...] = ys.astype(o_ref.dtype)


def _norm_proj_specs(rows, d, n_cols, groups, out_dtypes, tm):
    out_shape, out_specs = [], []
    flat_outs = [(size, kb) for _, size, outs in groups for _, kb in outs]
    for (size, kb), dt in zip(flat_outs, out_dtypes):
        if kb == HEAD_ROWS:
            out_shape.append(jax.ShapeDtypeStruct((rows * N_HEADS, V_DIM), dt))
            out_specs.append(pl.BlockSpec((tm * N_HEADS, V_DIM), lambda i, *_: (i, 0)))
        elif kb:
            out_shape.append(jax.ShapeDtypeStruct((rows // kb, size, kb), dt))
            out_specs.append(pl.BlockSpec((tm // kb, size, kb), lambda i, *_: (i, 0, 0)))
        else:
            out_shape.append(jax.ShapeDtypeStruct((rows, size), dt))
            out_specs.append(pl.BlockSpec((tm, size), lambda i, *_: (i, 0)))
    in_specs = [pl.BlockSpec((tm, d), lambda i, *_: (i, 0)),
                pl.BlockSpec((1, d), lambda i, *_: (0, 0)),
                pl.BlockSpec((d, n_cols), lambda i, *_: (0, 0), pipeline_mode=pl.Buffered(1))]
    return in_specs, out_specs, out_shape


def _norm_proj(x2d, g, w_bf16, groups, out_dtypes, tm):
    rows, d = x2d.shape
    in_specs, out_specs, out_shape = _norm_proj_specs(rows, d, w_bf16.shape[1], groups, out_dtypes, tm)
    return pl.pallas_call(
        functools.partial(_norm_proj_kernel, groups=groups),
        grid=(rows // tm,),
        in_specs=in_specs,
        out_specs=out_specs,
        out_shape=out_shape,
        compiler_params=_cparams("parallel"),
        name="norm_proj",
    )(x2d, g.reshape(1, d), w_bf16)


def _rel_bucket(rel):
    n = jnp.maximum(rel, 0)
    max_exact = N_BUCKETS // 2
    nf = jnp.maximum(n, 1).astype(F32)
    large = max_exact + jnp.floor(jnp.log(nf / max_exact) / math.log(MAX_DISTANCE / max_exact)
                                  * (N_BUCKETS - max_exact)).astype(jnp.int32)
    large = jnp.minimum(large, N_BUCKETS - 1)
    return jnp.where(n < max_exact, n, large)


def _bias_from_rel(rel, rb_ref, head, visible):
    bucket = _rel_bucket(rel)
    acc = jnp.zeros(rel.shape, F32)
    for b in range(N_BUCKETS):
        acc = jnp.where(bucket == b, rb_ref[b, head], acc)
    return jnp.where(visible, acc, NEG_INF)


def _bias_kernel(rb_ref, pb_ref, sb_ref, *, tq, tk, past, n_new):
    n_kinds = pb_ref.shape[1]
    key = lax.broadcasted_iota(jnp.int32, (tk, tq), 0)
    qry = lax.broadcasted_iota(jnp.int32, (tk, tq), 1)
    for kind in range(n_kinds):
        rel = kind * tq + qry - key
        for h in range(N_HEADS):
            if kind * tq - (tk - 1) >= MAX_DISTANCE:
                tile = jnp.full((tk, tq), rb_ref[N_BUCKETS - 1, h], F32) * LOG2E
            else:
                tile = _bias_from_rel(rel, rb_ref, h, rel >= 0) * LOG2E
            pb_ref[h, kind, :, :tq] = tile
            pb_ref[h, kind, :, tq:] = tile
    rows_per_head = 2 * n_new
    n_cols = sb_ref.shape[1]
    qi = lax.broadcasted_iota(jnp.int32, (rows_per_head, n_cols), 0) % n_new
    key = lax.broadcasted_iota(jnp.int32, (rows_per_head, n_cols), 1)
    rel = past + qi - key
    for h in range(N_HEADS):
        sb_ref[h * rows_per_head:(h + 1) * rows_per_head, :] = _bias_from_rel(rel, rb_ref, h, rel >= 0)


def _bias_tables(rel_bias, tq, tk, past, n_new):
    n_kinds = -(-(tk + MAX_DISTANCE - 1) // tq) + 1
    n_cols = past + SAMPLE_KPAD
    return pl.pallas_call(
        functools.partial(_bias_kernel, tq=tq, tk=tk, past=past, n_new=n_new),
        in_specs=[pl.BlockSpec(memory_space=pltpu.SMEM)],
        out_specs=[pl.BlockSpec(memory_space=pltpu.VMEM), pl.BlockSpec(memory_space=pltpu.VMEM)],
        out_shape=[jax.ShapeDtypeStruct((N_HEADS, n_kinds, tk, 2 * tq), F32),
                   jax.ShapeDtypeStruct((N_HEADS * 2 * n_new, n_cols), F32)],
        compiler_params=pltpu.CompilerParams(vmem_limit_bytes=VMEM_LIMIT),
        name="bias_tables",
    )(rel_bias)


def _subln(o, g):
    return _rmsnorm(o, g) * SUBLN_SCALE


def _two_map_queries(q_h):
    lane = lax.broadcasted_iota(jnp.int32, q_h.shape, 1)
    zero = jnp.zeros_like(q_h)
    return jnp.concatenate([jnp.where(lane < HEAD_DIM, q_h, zero), jnp.where(lane >= HEAD_DIM, q_h, zero)], axis=0)


def _prompt_attn_kernel(lq1_ref, lk1_ref, lq2_ref, lk2_ref, q_ref, k_ref, vt_ref, bias_ref, g_ref, o_ref,
                        q2_ref, m_ref, l_ref, acc_ref, s0_ref, *, tq, tk):
    i = pl.program_id(1)
    lam = _lam_value(lq1_ref[...], lk1_ref[...], lq2_ref[...], lk2_ref[...])
    n_kinds = bias_ref.shape[1]
    for h in range(N_HEADS):
        q2_ref[h] = _two_map_queries(q_ref[:, h * V_DIM:(h + 1) * V_DIM])
    m_ref[...] = jnp.full(m_ref.shape, NEG_INF, F32)
    l_ref[...] = jnp.zeros(l_ref.shape, F32)
    acc_ref[...] = jnp.zeros(acc_ref.shape, F32)

    last = (i * tq + tq - 1) // tk

    def scores(j, h):
        r0 = pl.multiple_of(j * tk, tk)
        kind = jnp.minimum((i * tq - j * tk) // tq, n_kinds - 1)
        hs = slice(h * V_DIM, (h + 1) * V_DIM)
        return _dot_nt(k_ref[pl.ds(r0, tk), hs], q2_ref[h]) + bias_ref[h, kind]

    ahead = 2
    for h in range(ahead):
        s0_ref[h] = scores(0, h)

    ones_rows = jnp.ones((16, tk), BF16)

    def body(j, carry):
        pending = [s0_ref[h] for h in range(ahead)]
        j_next = jnp.minimum(j + 1, last)
        for h in range(N_HEADS):
            hs = slice(h * V_DIM, (h + 1) * V_DIM)
            s = pending.pop(0)
            pending.append(scores(j, h + ahead) if h + ahead < N_HEADS else scores(j_next, h + ahead - N_HEADS))
            m_old = m_ref[h]
            m_new = jnp.maximum(m_old, jnp.max(s, axis=0, keepdims=True))
            alpha = jnp.exp2(m_old - m_new)
            p = jnp.exp2(s - m_new).astype(BF16)
            pv = _dot(jnp.concatenate([vt_ref[j, hs, :], ones_rows], axis=0), p)
            l_ref[h] = alpha * l_ref[h] + pv[V_DIM:V_DIM + 1]
            m_ref[h] = m_new
            acc_ref[h] = alpha * acc_ref[h] + pv[:V_DIM]
        for h in range(ahead):
            s0_ref[h] = pending[h]
        return carry

    lax.fori_loop(0, last + 1, body, 0)
    for h in range(N_HEADS):
        out_t = acc_ref[h] * (1.0 / l_ref[h])
        o_t = out_t[:, :tq] - lam * out_t[:, tq:]
        o_ref[:, h * V_DIM:(h + 1) * V_DIM] = _subln(o_t.T, g_ref[...]).astype(o_ref.dtype)


def _prompt_attn(lams, q, k, vt, bias, subln_g, batch, seq, tq, tk):
    vec = pl.BlockSpec((1, HEAD_DIM), lambda b, i: (0, 0))
    nq = seq // tq
    return pl.pallas_call(
        functools.partial(_prompt_attn_kernel, tq=tq, tk=tk),
        grid=(batch, nq),
        in_specs=[vec, vec, vec, vec,
                  pl.BlockSpec((tq, ATTN_WIDTH), lambda b, i: (b * nq + i, 0)),
                  pl.BlockSpec((seq, ATTN_WIDTH), lambda b, i: (b, 0)),
                  pl.BlockSpec((seq // tk, ATTN_WIDTH, tk), lambda b, i: (b, 0, 0)),
                  pl.BlockSpec(bias.shape, lambda b, i: (0, 0, 0, 0), pipeline_mode=pl.Buffered(1)),
                  pl.BlockSpec((1, V_DIM), lambda b, i: (0, 0))],
        out_specs=pl.BlockSpec((tq, ATTN_WIDTH), lambda b, i: (b * nq + i, 0)),
        out_shape=jax.ShapeDtypeStruct((batch * seq, ATTN_WIDTH), BF16),
        scratch_shapes=[pltpu.VMEM((N_HEADS, 2 * tq, V_DIM), BF16),
                        pltpu.VMEM((N_HEADS, 1, 2 * tq), F32),
                        pltpu.VMEM((N_HEADS, 1, 2 * tq), F32),
                        pltpu.VMEM((N_HEADS, V_DIM, 2 * tq), F32),
                        pltpu.VMEM((2, tk, 2 * tq), F32)],
        compiler_params=_cparams("parallel", "parallel"),
        name="prompt_attn",
    )(*lams, q, k, vt, bias, subln_g.reshape(1, V_DIM))


def _pool_branch(ext, u, pos0, wbd, scale):
    n, c = u.shape
    p = ext.shape[0] - n
    s2 = ext + pltpu.roll(ext, 1, 0)
    s4 = s2 + pltpu.roll(s2, 2, 0)
    s8 = s4 + pltpu.roll(s4, 4, 0)
    s16 = s8 + pltpu.roll(s8, 8, 0)
    lane = lax.broadcasted_iota(jnp.int32, (n, c), 1)
    grp = lane // POOL_GROUP_DIM
    win = jnp.where(grp == 0, s2[p:], jnp.where(grp == 1, s4[p:], jnp.where(grp == 2, s8[p:], s16[p:])))
    width = jnp.where(grp == 0, 2, jnp.where(grp == 1, 4, jnp.where(grp == 2, 8, 16)))
    pos = pos0 + lax.broadcasted_iota(jnp.int32, (n, c), 0)
    cnt = jnp.minimum(pos + 1, width).astype(F32)
    d = win / cnt - u
    return _dot(d.astype(BF16), wbd) * scale


def _prompt_side_kernel(u_ref, uprev_ref, qm_ref, mkt_ref, mvt_ref, wbd_ref, ps_ref, po_ref, om_ref, *, tm):
    i = pl.program_id(1)
    u = u_ref[...]
    prev = jnp.where(i > 0, uprev_ref[...], 0.0)
    ext = jnp.concatenate([prev, u], axis=0)
    po_ref[...] = _pool_branch(ext, u, i * tm, wbd_ref[...], ps_ref[...]).astype(po_ref.dtype)
    heads = [slice(h * MEM_HEAD_DIM, (h + 1) * MEM_HEAD_DIM) for h in range(N_HEADS)]
    scores = [_dot(qm_ref[:, sl], mkt_ref[sl, :].astype(BF16)) for sl in heads]
    probs = [_softmax_rows(s).astype(BF16) for s in scores]
    for sl, p in zip(heads, probs):
        om_ref[:, sl] = _dot_nt(p, mvt_ref[sl, :].astype(BF16)).astype(om_ref.dtype)


def _prompt_side(u, qm, mk, mv, wbd, pool_scale, batch, seq, tm):
    nblk = seq // tm
    halo = 16
    row = lambda b, i: (b * nblk + i, 0)
    return pl.pallas_call(
        functools.partial(_prompt_side_kernel, tm=tm),
        grid=(batch, nblk),
        in_specs=[pl.BlockSpec((tm, POOL_WIDTH), row),
                  pl.BlockSpec((halo, POOL_WIDTH),
                               lambda b, i: (jnp.maximum((b * nblk + i) * (tm // halo) - 1, 0), 0)),
                  pl.BlockSpec((tm, MEM_WIDTH), row),
                  pl.BlockSpec((None, MEM_WIDTH, N_MEM), lambda b, i: (b, 0, 0)),
                  pl.BlockSpec((None, MEM_WIDTH, N_MEM), lambda b, i: (b, 0, 0)),
                  pl.BlockSpec((POOL_WIDTH, POOL_WIDTH), lambda b, i: (0, 0)),
                  pl.BlockSpec((1, POOL_WIDTH), lambda b, i: (0, 0))],
        out_specs=[pl.BlockSpec((tm, POOL_WIDTH), row), pl.BlockSpec((tm, MEM_WIDTH), row)],
        out_shape=[jax.ShapeDtypeStruct((batch * seq, POOL_WIDTH), BF16),
                   jax.ShapeDtypeStruct((batch * seq, MEM_WIDTH), BF16)],
        compiler_params=_cparams("parallel", "parallel"),
        name="prompt_side",
    )(u, u, qm, mk, mv, wbd, pool_scale.reshape(1, POOL_WIDTH))


def _page_copy(caches, sem, which, slot, p, page_id):
    hbm, buf = caches[which]
    return pltpu.make_async_copy(hbm.at[page_id], buf.at[slot, p], sem.at[slot, which])


def _start_page(caches, sem, pt_ref, seq, slot, p):
    page_id = pt_ref[seq, p]
    _page_copy(caches, sem, 0, slot, p, page_id).start(priority=0)
    _page_copy(caches, sem, 1, slot, p, page_id).start(priority=1)


def _wait_slot(caches, sem, which, slot, n_pages):
    for p in range(n_pages):
        _page_copy(caches, sem, which, slot, p, 0).wait()


def _sample_sequence(seq, n_seq, pt_ref, lq1_ref, lk1_ref, lq2_ref, lk2_ref, q_ref, kn_ref, vn_ref, bias_ref, g_ref,
                     ck_hbm, cv_hbm, o_ref, s_ref, kbuf, vbuf, sem, *, n_pages, n_new):
    slot = seq % PAGE_SLOTS
    lam = _lam_value(lq1_ref[...], lk1_ref[...], lq2_ref[...], lk2_ref[...])
    page = kbuf.shape[2] // N_HEADS
    n_tail = s_ref.shape[1] - n_pages * page
    pad = jnp.zeros((n_tail - n_new, V_DIM), F32)
    hr = 2 * n_new
    caches = ((ck_hbm, kbuf), (cv_hbm, vbuf))
    ahead = PAGE_SLOTS - 1

    @pl.when(seq == 0)
    def _():
        for s0 in range(ahead):
            for p in range(n_pages):
                _start_page(caches, sem, pt_ref, s0, s0, p)

    @pl.when(seq + ahead < n_seq)
    def _():
        for p in range(n_pages):
            _start_page(caches, sem, pt_ref, seq + ahead, (seq + ahead) % PAGE_SLOTS, p)

    def page_pair(buf, c, h):
        rows = pl.ds(h, page, stride=N_HEADS)
        return jnp.concatenate([buf[slot, c, rows, :], buf[slot, c + 1, rows, :]], axis=0).astype(BF16)

    def head_rows(ref, h):
        return ref[pl.ds(h, n_new, stride=N_HEADS), :]

    _wait_slot(caches, sem, 0, slot, n_pages)
    for h in range(N_HEADS):
        q2 = _two_map_queries(q_ref[:, h * V_DIM:(h + 1) * V_DIM]).astype(BF16)
        for c in range(0, n_pages, 2):
            s_ref[h * hr:(h + 1) * hr, c * page:(c + 2) * page] = _dot_nt(q2, page_pair(kbuf, c, h))
        k_tail = jnp.concatenate([head_rows(kn_ref, h), pad], axis=0).astype(BF16)
        s_ref[h * hr:(h + 1) * hr, n_pages * page:] = _dot_nt(q2, k_tail)
    prob = _softmax_rows(s_ref[...] + bias_ref[...])
    _wait_slot(caches, sem, 1, slot, n_pages)
    for h in range(N_HEADS):
        a = prob[h * hr:h * hr + n_new] - lam * prob[h * hr + n_new:(h + 1) * hr]
        v_tail = jnp.concatenate([head_rows(vn_ref, h), pad], axis=0).astype(BF16)
        o_h = _dot(a[:, n_pages * page:].astype(BF16), v_tail)
        for c in range(0, n_pages, 2):
            o_h = o_h + _dot(a[:, c * page:(c + 2) * page].astype(BF16), page_pair(vbuf, c, h))
        o_ref[:, h * V_DIM:(h + 1) * V_DIM] = _subln(o_h, g_ref[...]).astype(o_ref.dtype)


def _sample_side(qm_ref, mkt_ref, mvt_ref, u_ref, st_ref, wbd_ref, ps_ref, om_ref, po_ref, *, n_seqs, n_new, past):
    head_of_lane = lax.broadcasted_iota(jnp.int32, (n_new, MEM_WIDTH), 1) // MEM_HEAD_DIM
    scores = []
    for j in range(n_seqs):
        qm = qm_ref[j]
        qm4 = jnp.concatenate([jnp.where(head_of_lane == h, qm, 0.0) for h in range(N_HEADS)], axis=0).astype(BF16)
        scores.append(_dot(qm4, mkt_ref[j].astype(BF16)))
    probs = [_softmax_rows(s).astype(BF16) for s in scores]
    om_all = [_dot_nt(probs[j], mvt_ref[j].astype(BF16)) for j in range(n_seqs)]
    for j in range(n_seqs):
        om = jnp.zeros((n_new, MEM_WIDTH), F32)
        for h in range(N_HEADS):
            om = jnp.where(head_of_lane == h, om_all[j][h * n_new:(h + 1) * n_new], om)
        om_ref[j] = om.astype(om_ref.dtype)
    for j in range(n_seqs):
        u = u_ref[j]
        ext = jnp.concatenate([st_ref[j], u], axis=0)
        po_ref[j] = _pool_branch(ext, u, past, wbd_ref[...], ps_ref[...]).astype(po_ref.dtype)


N_SAMPLE_INPUTS = 18


def _host_and_sample_kernel(pt_ref, *refs, host_body, n_host_in, n_host_out, seqs_per_step, n_seq, n_pages, n_new,
                            past):
    host_in = refs[:n_host_in]
    (lq1_ref, lk1_ref, lq2_ref, lk2_ref, q_ref, kn_ref, vn_ref, bias_ref, g_ref, qm_ref, mkt_ref, mvt_ref, u_ref,
     st_ref, wbd_ref, ps_ref, ck_hbm, cv_hbm) = refs[n_host_in:n_host_in + N_SAMPLE_INPUTS]
    outs = refs[n_host_in + N_SAMPLE_INPUTS:]
    host_out = outs[:n_host_out]
    o_ref, om_ref, po_ref, s_ref, kbuf, vbuf, sem = outs[n_host_out:]
    host_body(*host_in, *host_out)
    first = pl.program_id(0) * seqs_per_step
    for j in range(seqs_per_step):
        _sample_sequence(first + j, n_seq, pt_ref, lq1_ref, lk1_ref, lq2_ref, lk2_ref, q_ref.at[j], kn_ref.at[j],
                         vn_ref.at[j], bias_ref, g_ref, ck_hbm, cv_hbm, o_ref.at[j], s_ref, kbuf, vbuf, sem,
                         n_pages=n_pages, n_new=n_new)
    _sample_side(qm_ref, mkt_ref, mvt_ref, u_ref, st_ref, wbd_ref, ps_ref, om_ref, po_ref,
                 n_seqs=seqs_per_step, n_new=n_new, past=past)


def _host_and_sample(host_body, host_args, host_specs, n_steps, name, page_table, lams, q, k_new, v_new, bias, subln_g,
                     qm, mem_kt, mem_vt, u, state16, wbd, pool_scale, cache_k, cache_v):
    host_in, host_out_specs, host_out_shape = host_specs
    n_seq, n_pages = page_table.shape
    n_new = q.shape[1]
    page_rows = cache_k.shape[1]
    past = n_pages * page_rows // N_HEADS
    assert n_pages % 2 == 0 and n_seq >= PAGE_SLOTS and n_seq % n_steps == 0
    sps = n_seq // n_steps
    vec = pl.BlockSpec((1, HEAD_DIM), lambda i, pt: (0, 0))
    per_seq = lambda a: pl.BlockSpec((sps,) + a.shape[1:], lambda i, pt: (i,) + (0,) * (a.ndim - 1))
    const = lambda a: pl.BlockSpec(a.shape, lambda i, pt: (0,) * a.ndim)
    hbm = pl.BlockSpec(memory_space=pl.ANY)
    g = subln_g.reshape(1, V_DIM)
    ps = pool_scale.reshape(1, POOL_WIDTH)
    sample_args = (*lams, q, k_new, v_new, bias, g, qm, mem_kt, mem_vt, u, state16, wbd, ps, cache_k, cache_v)
    sample_specs = [vec, vec, vec, vec, per_seq(q), per_seq(k_new), per_seq(v_new), const(bias), const(g),
                    per_seq(qm), per_seq(mem_kt), per_seq(mem_vt), per_seq(u), per_seq(state16), const(wbd),
                    const(ps), hbm, hbm]
    assert len(sample_args) == len(sample_specs) == N_SAMPLE_INPUTS
    out3 = lambda width: jax.ShapeDtypeStruct((n_seq, n_new, width), F32)
    out_spec = lambda width: pl.BlockSpec((sps, n_new, width), lambda i, pt: (i, 0, 0))
    page_buffers = pltpu.VMEM((PAGE_SLOTS, n_pages, page_rows, V_DIM), F32)
    grid_spec = pltpu.PrefetchScalarGridSpec(
        num_scalar_prefetch=1,
        grid=(n_steps,),
        in_specs=list(host_in) + sample_specs,
        out_specs=list(host_out_specs) + [out_spec(ATTN_WIDTH), out_spec(MEM_WIDTH), out_spec(POOL_WIDTH)],
        scratch_shapes=[pltpu.VMEM(bias.shape, F32), page_buffers, page_buffers,
                        pltpu.SemaphoreType.DMA((PAGE_SLOTS, 2))],
    )
    outs = pl.pallas_call(
        functools.partial(_host_and_sample_kernel, host_body=host_body, n_host_in=len(host_in),
                          n_host_out=len(host_out_shape), seqs_per_step=sps, n_seq=n_seq, n_pages=n_pages,
                          n_new=n_new, past=past),
        grid_spec=grid_spec,
        out_shape=list(host_out_shape) + [out3(ATTN_WIDTH), out3(MEM_WIDTH), out3(POOL_WIDTH)],
        compiler_params=_cparams("arbitrary"),
        name=name,
    )(page_table, *host_args, *sample_args)
    return outs[:len(host_out_shape)], outs[len(host_out_shape):]


def _merge_kernel(x_ref, o_ref, po_ref, om_ref, g_ref, wgl_ref, wa_ref, wp_ref, wm_ref, wout_ref, x1_ref):
    x = x_ref[...]
    h = _rmsnorm(x, g_ref[...]).astype(BF16)
    d = D_MODEL
    merged = jax.nn.sigmoid(_dot(h, wgl_ref[:, 0:d])) * _dot(o_ref[...].astype(BF16), wa_ref[...])
    merged = merged + jax.nn.sigmoid(_dot(h, wgl_ref[:, d:2 * d])) * _dot(po_ref[...].astype(BF16), wp_ref[...])
    merged = merged + jax.nn.sigmoid(_dot(h, wgl_ref[:, 2 * d:3 * d])) * _dot(om_ref[...].astype(BF16), wm_ref[...])
    x1_ref[...] = x + _dot(merged.astype(BF16), wout_ref[...])


def _merge_specs(rows, d, g, weights, tm):
    row = lambda width: pl.BlockSpec((tm, width), lambda i, *_: (i, 0))
    full = lambda a: pl.BlockSpec(a.shape, lambda i, *_: (0, 0), pipeline_mode=pl.Buffered(1))
    in_specs = [row(d), row(ATTN_WIDTH), row(POOL_WIDTH), row(MEM_WIDTH), full(g)] + [full(w) for w in weights]
    return in_specs, [row(d)], [jax.ShapeDtypeStruct((rows, d), F32)]


def _merge(x2d, o, po, om, norm_g, wgl, wa, wp, wm, wout, tm):
    rows, d = x2d.shape
    g = norm_g.reshape(1, d)
    in_specs, out_specs, out_shape = _merge_specs(rows, d, g, (wgl, wa, wp, wm, wout), tm)
    return pl.pallas_call(
        _merge_kernel,
        grid=(rows // tm,),
        in_specs=in_specs,
        out_specs=out_specs[0],
        out_shape=out_shape[0],
        compiler_params=_cparams("parallel"),
        name="merge",
    )(x2d, o, po, om, g, wgl, wa, wp, wm, wout)


def _conv_ffn_kernel(x_ref, g2_ref, wg_ref, wu_ref, cw_ref, cb_ref, wd_ref, gf_ref, *refs, tm, fc, seq_rows):
    if seq_rows:
        st0_ref, st1_ref, y_ref, t0_ref, t1_ref, act_ref, pre_ref, gate_ref = refs
        n_seqs = tm // seq_rows
        lanes = pre_ref.shape[2]
        pre_ref[...] = jnp.zeros_like(pre_ref)
        for j in range(D_FF // lanes):
            pre_ref[j, pl.ds(0, n_seqs, stride=seq_rows), :] = st0_ref[:, j * lanes:(j + 1) * lanes]
            pre_ref[j, pl.ds(1, n_seqs, stride=seq_rows), :] = st1_ref[:, j * lanes:(j + 1) * lanes]
    else:
        y_ref, tail_ref, carry_ref, act_ref = refs

        @pl.when(pl.program_id(1) == 0)
        def _():
            carry_ref[...] = jnp.zeros_like(carry_ref)

    x = x_ref[...]
    h = _rmsnorm(x, g2_ref[...]).astype(BF16)
    for c0 in range(0, D_FF, fc):
        cs = slice(c0, c0 + fc)
        gate = _dot(h, wg_ref[:, cs])
        up = _dot(h, wu_ref[:, cs])
        if seq_rows:
            slabs = range(c0 // lanes, (c0 + fc) // lanes)
            pre = jnp.concatenate([pre_ref[j] for j in slabs], axis=1)
            t = lax.broadcasted_iota(jnp.int32, (tm, fc), 0) % seq_rows
            back1 = jnp.where(t >= 1, pltpu.roll(gate, 1, 0), pltpu.roll(pre, tm - 1, 0))
            back2 = jnp.where(t >= 2, pltpu.roll(gate, 2, 0), pre)
            for j in slabs:
                gate_ref[j] = gate[:, j * lanes - c0:(j + 1) * lanes - c0]
        else:
            ext = jnp.concatenate([carry_ref[:, cs], gate], axis=0)
            back1 = pltpu.roll(ext, 1, 0)[8:]
            back2 = pltpu.roll(ext, 2, 0)[8:]
            carry_ref[:, cs] = gate[tm - 8:]
        gc = cb_ref[:, cs] + cw_ref[0:1, cs] * back2
        gc = gc + cw_ref[1:2, cs] * back1
        gc = gc + cw_ref[2:3, cs] * gate
        act_ref[:, cs] = (jax.nn.gelu(gc) * up).astype(BF16)
    if seq_rows:
        for j in range(D_FF // lanes):
            t0_ref[:, j * lanes:(j + 1) * lanes] = gate_ref[j, pl.ds(seq_rows - 2, n_seqs, stride=seq_rows), :]
            t1_ref[:, j * lanes:(j + 1) * lanes] = gate_ref[j, pl.ds(seq_rows - 1, n_seqs, stride=seq_rows), :]
    else:
        tail_ref[...] = carry_ref[...]
    y_ref[...] = _rmsnorm(x + _dot(act_ref[...], wd_ref[...]), gf_ref[...])


def _conv_ffn(x1, norm_g, wg, wu, conv_w, conv_b, wd, final_g, tm, batch=None, seq=None, prefix=None):
    rows, d = x1.shape
    g2 = norm_g.reshape(1, d)
    gf = final_g.reshape(1, d)
    cb = conv_b.reshape(1, D_FF)
    if prefix is None:
        nblk = seq // tm
        row = lambda width: pl.BlockSpec((tm, width), lambda b, i: (b * nblk + i, 0))
        full = lambda a: pl.BlockSpec(a.shape, lambda b, i: (0, 0), pipeline_mode=pl.Buffered(1))
        return pl.pallas_call(
            functools.partial(_conv_ffn_kernel, tm=tm, fc=FFN_CHUNK, seq_rows=0),
            grid=(batch, nblk),
            in_specs=[row(d), full(g2), full(wg), full(wu), full(conv_w), full(cb), full(wd), full(gf)],
            out_specs=[row(d), pl.BlockSpec((None, 8, D_FF), lambda b, i: (b, 0, 0))],
            out_shape=[jax.ShapeDtypeStruct((rows, d), F32), jax.ShapeDtypeStruct((batch, 8, D_FF), F32)],
            scratch_shapes=[pltpu.VMEM((8, D_FF), F32), pltpu.VMEM((tm, D_FF), BF16)],
            compiler_params=_cparams("parallel", "arbitrary"),
            name="conv_ffn_prompt",
        )(x1, g2, wg, wu, conv_w, cb, wd, gf)
    seq_rows = 8
    row = lambda width: pl.BlockSpec((tm, width), lambda i: (i, 0))
    per_seq = pl.BlockSpec((tm // seq_rows, D_FF), lambda i: (i, 0))
    full = lambda a: pl.BlockSpec(a.shape, lambda i: (0, 0), pipeline_mode=pl.Buffered(1))
    tails = jax.ShapeDtypeStruct((rows // seq_rows, D_FF), F32)
    return pl.pallas_call(
        functools.partial(_conv_ffn_kernel, tm=tm, fc=FFN_CHUNK, seq_rows=seq_rows),
        grid=(rows // tm,),
        in_specs=[row(d), full(g2), full(wg), full(wu), full(conv_w), full(cb), full(wd), full(gf), per_seq, per_seq],
        out_specs=[row(d), per_seq, per_seq],
        out_shape=[jax.ShapeDtypeStruct((rows, d), F32), tails, tails],
        scratch_shapes=[pltpu.VMEM((tm, D_FF), BF16), pltpu.VMEM((D_FF // LANES, tm, LANES), F32),
                        pltpu.VMEM((D_FF // LANES, tm, LANES), F32)],
        compiler_params=_cparams("parallel"),
        name="conv_ffn_sample",
    )(x1, g2, wg, wu, conv_w, cb, wd, gf, *prefix)


def kernel(x_prompt, x_sample, mem_prompt, cache_k, cache_v, page_table, state_pool, state_ffn_conv, cache_mem_k, cache_mem_v, norm1_g, w_in, lam_q1, lam_k1, lam_q2, lam_k2, subln_g, w_pool_grp, pool_scale, w_br_attn, w_br_pool, w_br_mem, mem_norm_g, w_mem_kv, w_out, norm2_g, w_ffn_gate, w_ffn_up, ffn_conv_w, ffn_conv_b, w_ffn_down, rel_bias, final_norm_g):
    depth = w_in.shape[0]
    assert depth == 1, "single-layer step only"
    B, S, D = x_prompt.shape
    DB, DS, _ = x_sample.shape
    n_phys, page = cache_k.shape[1], cache_k.shape[2]
    n_pages = page_table.shape[1]
    past = n_pages * page
    assert DS == 8 and S % ROW_BLOCK == 0 and (DB * DS) % ROW_BLOCK == 0
    assert ATTN_TK % ATTN_TQ == 0 and ROW_BLOCK % ATTN_TK == 0 and S % ATTN_TK == 0

    w_qkv = w_in[0, :, :2048].astype(BF16)
    w_gl = w_in[0, :, 2048:].astype(BF16)
    wa, wp, wm = w_br_attn[0].astype(BF16), w_br_pool[0].astype(BF16), w_br_mem[0].astype(BF16)
    wout = w_out[0].astype(BF16)
    wg, wu, wd = w_ffn_gate[0].astype(BF16), w_ffn_up[0].astype(BF16), w_ffn_down[0].astype(BF16)
    w_mem = w_mem_kv[0].astype(BF16)
    wbd = jnp.zeros((POOL_WIDTH, POOL_WIDTH), F32)
    for gi in range(len(POOL_WINDOWS)):
        sl = slice(gi * POOL_GROUP_DIM, (gi + 1) * POOL_GROUP_DIM)
        wbd = wbd.at[sl, sl].set(w_pool_grp[0, gi])
    wbd = wbd.astype(BF16)
    lams = tuple(a[0].reshape(1, HEAD_DIM) for a in (lam_q1, lam_k1, lam_q2, lam_k2))

    prompt_bias, sample_bias = _bias_tables(rel_bias, ATTN_TQ, ATTN_TK, past, DS)

    xs = x_sample.reshape(DB * DS, D)
    groups_s = ((0, 512, ((QK_SCALE, 0),)), (512, 512, ((1.0, HEAD_ROWS),)), (1024, 512, ((1.0, HEAD_ROWS),)),
                (1536, 256, ((1.0, 0),)), (1792, 256, ((MEM_SCALE, 0),)))
    qs, ks, vs, us, qms = _norm_proj(xs, norm1_g[0], w_qkv, groups_s, (F32,) * 5, ROW_BLOCK)
    seq3 = lambda a: a.reshape(DB, DS, a.shape[-1])
    head_rows = lambda a: a.reshape(DB, DS * N_HEADS, V_DIM)
    state16 = jnp.pad(state_pool[0], ((0, 0), (16 - POOL_STATE, 0), (0, 0)))
    mem_t = lambda a: jnp.transpose(a[0], (0, 2, 3, 1)).reshape(DB, MEM_WIDTH, N_MEM)
    pages = lambda a: a[0].reshape(n_phys, page * N_HEADS, V_DIM)

    xp = x_prompt.reshape(B * S, D)
    groups = ((0, 512, ((QK_SCALE * LOG2E, 0),)), (512, 512, ((1.0, HEAD_ROWS), (1.0, 0))),
              (1024, 512, ((1.0, HEAD_ROWS), (1.0, ATTN_TK))), (1536, 256, ((1.0, 0),)), (1792, 256, ((MEM_SCALE, 0),)))
    qp, kp, kp_b, vp, vp_t, up, qmp = _norm_proj(xp, norm1_g[0], w_qkv, groups,
                                                 (BF16, F32, BF16, F32, BF16, F32, BF16), ROW_BLOCK)
    mk_p, mv_p = _norm_proj(mem_prompt.reshape(B * N_MEM, D), mem_norm_g[0], w_mem,
                            ((0, MEM_WIDTH, ((1.0, N_MEM),)), (MEM_WIDTH, MEM_WIDTH, ((1.0, N_MEM),))), (F32, F32), ROW_BLOCK)
    op = _prompt_attn(lams, qp, kp_b, vp_t, prompt_bias, subln_g[0], B, S, ATTN_TQ, ATTN_TK)
    pop, omp = _prompt_side(up, qmp, mk_p, mv_p, wbd, pool_scale[0], B, S, ROW_BLOCK)
    g1 = norm1_g[0].reshape(1, D)
    (x1p,), (os_, oms, pos_) = _host_and_sample(
        _merge_kernel, (xp, op, pop, omp, g1, w_gl, wa, wp, wm, wout),
        _merge_specs(B * S, D, g1, (w_gl, wa, wp, wm, wout), ROW_BLOCK), B * S // ROW_BLOCK, "merge_and_sample",
        page_table, lams, seq3(qs), head_rows(ks), head_rows(vs), sample_bias, subln_g[0], seq3(qms),
        mem_t(cache_mem_k), mem_t(cache_mem_v), seq3(us), state16, wbd, pool_scale[0], pages(cache_k), pages(cache_v))
    yp, tail_p = _conv_ffn(x1p, norm2_g[0], wg, wu, ffn_conv_w[0], ffn_conv_b[0], wd, final_norm_g, 2 * ROW_BLOCK,
                           batch=B, seq=S)

    flat = lambda a: a.reshape(DB * DS, a.shape[-1])
    x1s = _merge(xs, flat(os_), flat(pos_), flat(oms), norm1_g[0], w_gl, wa, wp, wm, wout, ROW_BLOCK)
    ys, tail0_s, tail1_s = _conv_ffn(x1s, norm2_g[0], wg, wu, ffn_conv_w[0], ffn_conv_b[0], wd, final_norm_g, ROW_BLOCK,
                                     prefix=(state_ffn_conv[0, :, 0], state_ffn_conv[0, :, 1]))

    heads = lambda a, n: a.reshape(1, n, -1, N_HEADS, V_DIM)
    new_pool_p = up.reshape(B, S, POOL_WIDTH)[:, S - POOL_STATE:][None]
    new_pool_s = jnp.concatenate([state_pool[0][:, DS:], seq3(us)], axis=1)[None]
    new_conv_p = tail_p[:, 6:8][None]
    new_conv_s = jnp.stack([tail0_s, tail1_s], axis=1)[None]
    mem_heads = lambda a: jnp.transpose(a.reshape(B, N_HEADS, MEM_HEAD_DIM, N_MEM), (0, 3, 1, 2))[None]
    return (yp.reshape(B, S, D), ys.reshape(DB, DS, D), heads(kp, B), heads(vp, B), heads(ks, DB), heads(vs, DB),
            new_pool_p, new_pool_s, new_conv_p, new_conv_s, mem_heads(mk_p), mem_heads(mv_p))
```
